```python
import math
import jax
import jax.numpy as jnp
from jax import lax
import numpy as np

D_MODEL = 2048
BATCH = 8
SEQ = 4096
DEPTH = 2

GRID_W = 64
CTX_LEN = 256
N_MIXERS = 2
N_ATTN_LAYERS = (DEPTH + N_MIXERS - 1) // N_MIXERS
N_LRU_LAYERS = DEPTH // N_MIXERS
N_SUBLAYERS = 3
DA_HEAD_DIM = 64
DA_HEADS = D_MODEL // (2 * DA_HEAD_DIM)
DA_V_DIM = 2 * DA_HEAD_DIM
Q_BLOCK = 128
ROPE_BASE = 10000.0
AXIS_ROT_DIM = DA_HEAD_DIM // 2
D_RNN = 2560
LRU_BLOCKS = 10
LRU_BLOCK_W = D_RNN // LRU_BLOCKS
CONV_W = 4
CONV_PAD_LEFT = 2
CONV_PAD_RIGHT = CONV_W - 1 - CONV_PAD_LEFT
LRU_C = 8.0
A_MIN = 0.9
A_MAX = 0.999
D_FF = 5632
FFN_RES_WEIGHT = 0.5
ALPHA = (2.0 * DEPTH) ** 0.25
BETA = (8.0 * DEPTH) ** -0.25
LN_EPS = 1e-6

kernel_name = "hybrid_diffattn_rglru_macaron_dit"


def layer_norm(x, g, b):
    xf = x.astype(jnp.float32)
    mu = jnp.mean(xf, axis=-1, keepdims=True)
    var = jnp.mean(jnp.square(xf - mu), axis=-1, keepdims=True)
    y = (xf - mu) * lax.rsqrt(var + LN_EPS)
    return (y * g.astype(jnp.float32) + b.astype(jnp.float32)).astype(x.dtype)


def rms_norm(x, g):
    xf = x.astype(jnp.float32)
    y = xf * lax.rsqrt(jnp.mean(jnp.square(xf), axis=-1, keepdims=True) + LN_EPS)
    return (y * g.astype(jnp.float32)).astype(x.dtype)


def modulate(x, shift, scale):
    return x * (1 + scale) + shift


def split_mod(mod):
    return mod[..., 0, :, :], mod[..., 1, :, :], mod[..., 2, :, :]


def swiglu(u, w_gate, w_up, w_down):
    return (jax.nn.silu(u @ w_gate) * (u @ w_up)) @ w_down


def ffn_sublayer(x, mod, w_gate, w_up, w_down, g, b):
    shift, scale, gate = split_mod(mod)
    h = swiglu(modulate(x, shift, scale), w_gate, w_up, w_down)
    return layer_norm(ALPHA * x + FFN_RES_WEIGHT * gate * h, g, b)


def axial_rope_tables(n_tokens):
    rows = n_tokens // GRID_W
    row = jnp.repeat(jnp.arange(rows, dtype=jnp.float32), GRID_W)
    col = jnp.tile(jnp.arange(GRID_W, dtype=jnp.float32), rows)
    inv_freq = ROPE_BASE ** (-jnp.arange(0, AXIS_ROT_DIM, 2, dtype=jnp.float32) / AXIS_ROT_DIM)
    ang_r = row[:, None] * inv_freq[None, :]
    ang_c = col[:, None] * inv_freq[None, :]
    return jnp.cos(ang_r), jnp.sin(ang_r), jnp.cos(ang_c), jnp.sin(ang_c)


def _rope_1d(x, cos, sin):
    cos = cos[:, None, None, :].astype(x.dtype)
    sin = sin[:, None, None, :].astype(x.dtype)
    x1, x2 = jnp.split(x, 2, axis=-1)
    return jnp.concatenate([x1 * cos - x2 * sin, x1 * sin + x2 * cos], axis=-1)


def apply_axial_rope(x, tables):
    cos_r, sin_r, cos_c, sin_c = tables
    x_row, x_col = jnp.split(x, 2, axis=-1)
    return jnp.concatenate([_rope_1d(x_row, cos_r, sin_r), _rope_1d(x_col, cos_c, sin_c)], axis=-1)


def diff_attn_core(q, k, v, lam):
    s = jnp.einsum('bqhed,bkhed->bheqk', q * (DA_HEAD_DIM ** -0.5), k,
                   preferred_element_type=jnp.float32)
    p = jax.nn.softmax(s, axis=-1)
    a = p[:, :, 0] - lam * p[:, :, 1]
    return jnp.einsum('bhqk,bkhv->bqhv', a.astype(v.dtype), v)


def diff_attention(u, uc, w_qkv, w_o, lq1, lk1, lq2, lk2, subln_g, lambda_init, need_ctx):
    B, S, _ = u.shape
    f32 = jnp.float32
    lam = (jnp.exp(jnp.sum(lq1.astype(f32) * lk1.astype(f32)))
           - jnp.exp(jnp.sum(lq2.astype(f32) * lk2.astype(f32))) + lambda_init)

    def proj(t):
        n = t.shape[1]
        q, k, v = jnp.split(t @ w_qkv, 3, axis=-1)
        return (q.reshape(B, n, DA_HEADS, 2, DA_HEAD_DIM),
                k.reshape(B, n, DA_HEADS, 2, DA_HEAD_DIM),
                v.reshape(B, n, DA_HEADS, DA_V_DIM))

    def finish(o):
        o = rms_norm(o, subln_g) * (1.0 - lambda_init)
        return o.reshape(o.shape[0], o.shape[1], D_MODEL) @ w_o

    tables = axial_rope_tables(S)
    q, k, v = proj(u)
    q = apply_axial_rope(q, tables)
    k = apply_axial_rope(k, tables)
    qc, kc, vc = proj(uc)
    k_all = jnp.concatenate([kc, k], axis=1)
    v_all = jnp.concatenate([vc, v], axis=1)
    n_blk = S // Q_BLOCK
    q_blocks = jnp.moveaxis(q.reshape(B, n_blk, Q_BLOCK, DA_HEADS, 2, DA_HEAD_DIM), 1, 0)
    o = lax.map(lambda qb: diff_attn_core(qb, k_all, v_all, lam), q_blocks)
    o = jnp.moveaxis(o, 0, 1).reshape(B, S, DA_HEADS, DA_V_DIM)
    out = finish(o)
    out_c = finish(diff_attn_core(qc, kc, vc, lam)) if need_ctx else None
    return out, out_c


def depthwise_conv_centred(x, w, b):
    n = x.shape[1]
    xp = jnp.pad(x, ((0, 0), (CONV_PAD_LEFT, CONV_PAD_RIGHT), (0, 0)))
    y = xp[:, 0:n] * w[0]
    for j in range(1, CONV_W):
        y = y + xp[:, j:j + n] * w[j]
    return y + b


def block_diag_linear(x, w, b):
    xb = x.reshape(x.shape[0], x.shape[1], LRU_BLOCKS, LRU_BLOCK_W)
    y = jnp.einsum('bnhi,hij->bnhj', xb, w)
    return y.reshape(x.shape) + b


def rglru_coeffs(xr, w_a, b_a, w_i, b_i, a_param):
    f32 = jnp.float32
    r = jax.nn.sigmoid(block_diag_linear(xr, w_a, b_a)).astype(f32)
    i = jax.nn.sigmoid(block_diag_linear(xr, w_i, b_i)).astype(f32)
    log_a = -LRU_C * r * jax.nn.softplus(-a_param.astype(f32))
    a = jnp.exp(log_a)
    mult = jnp.sqrt(-jnp.expm1(2.0 * log_a))
    return a, mult * i * xr.astype(f32)


def _combine(left, right):
    a1, b1 = left
    a2, b2 = right
    return a1 * a2, a2 * b1 + b2


def linear_scan(a, b, h0, reverse):
    a_cum, b_cum = lax.associative_scan(_combine, (a, b), reverse=reverse, axis=1)
    if h0 is None:
        return b_cum
    return b_cum + a_cum * h0[:, None, :]


def rglru_mixer(u, uc, w_in, conv_w, conv_b, w_a, b_a, w_i, b_i, a_param, w_out, need_ctx):
    def branches(t):
        g, xr = jnp.split(t @ w_in, 2, axis=-1)
        return jax.nn.gelu(g), depthwise_conv_centred(xr, conv_w, conv_b)

    g, xr = branches(u)
    gc, xrc = branches(uc)
    h_lat = []
    h_ctx = []
    for d, rev in enumerate((False, True)):
        ac, bc = rglru_coeffs(xrc, w_a[d], b_a[d], w_i[d], b_i[d], a_param[d])
        hc = linear_scan(ac, bc, None, rev)
        h0 = hc[:, 0] if rev else hc[:, -1]
        a, b = rglru_coeffs(xr, w_a[d], b_a[d], w_i[d], b_i[d], a_param[d])
        h_lat.append(linear_scan(a, b, h0, rev))
        h_ctx.append(hc)
    out = ((h_lat[0] + h_lat[1]).astype(u.dtype) * g) @ w_out
    out_c = (((h_ctx[0] + h_ctx[1]).astype(uc.dtype) * gc) @ w_out) if need_ctx else None
    return out, out_c


def _normal(key, shape, scale):
    return jax.random.normal(key, shape, jnp.float32) * scale


def setup_inputs(seed: int = 0) -> dict:
    key = jax.random.key(seed)
    ks = jax.random.split(key, 28)
    D = D_MODEL
    a0 = jax.random.uniform(ks[25], (N_LRU_LAYERS, 2, D_RNN), jnp.float32, A_MIN, A_MAX)
    return {
        "x": _normal(ks[0], (BATCH, SEQ, D), 1.0),
        "c": _normal(ks[1], (BATCH, D), 1.0),
        "ctx": _normal(ks[2], (BATCH, CTX_LEN, D), 1.0),
        "c_ctx": _normal(ks[3], (D,), 1.0),
        "w_ada": _normal(ks[4], (DEPTH, D, N_SUBLAYERS * 3 * D), 0.5 * D ** -0.5),
        "b_ada": _normal(ks[5], (DEPTH, N_SUBLAYERS * 3 * D), 0.02),
        "ln_g": 1.0 + _normal(ks[6], (DEPTH, N_SUBLAYERS, D), 0.02),
        "ln_b": _normal(ks[7], (DEPTH, N_SUBLAYERS, D), 0.02),
        "ffn_w_gate": _normal(ks[8], (DEPTH, 2, D, D_FF), D ** -0.5),
        "ffn_w_up": _normal(ks[9], (DEPTH, 2, D, D_FF), D ** -0.5),
        "ffn_w_down": _normal(ks[10], (DEPTH, 2, D_FF, D), BETA * D_FF ** -0.5),
        "attn_w_qkv": _normal(ks[11], (N_ATTN_LAYERS, D, 3 * D), D ** -0.5),
        "attn_w_o": _normal(ks[12], (N_ATTN_LAYERS, D, D), BETA * D ** -0.5),
        "attn_lambda_q1": _normal(ks[13], (N_ATTN_LAYERS, DA_HEAD_DIM), 0.1),
        "attn_lambda_k1": _normal(ks[14], (N_ATTN_LAYERS, DA_HEAD_DIM), 0.1),
        "attn_lambda_q2": _normal(ks[15], (N_ATTN_LAYERS, DA_HEAD_DIM), 0.1),
        "attn_lambda_k2": _normal(ks[16], (N_ATTN_LAYERS, DA_HEAD_DIM), 0.1),
        "attn_subln_g": 1.0 + _normal(ks[17], (N_ATTN_LAYERS, DA_V_DIM), 0.02),
        "lru_w_in": _normal(ks[18], (N_LRU_LAYERS, D, 2 * D_RNN), D ** -0.5),
        "lru_conv_w": _normal(ks[19], (N_LRU_LAYERS, CONV_W, D_RNN), CONV_W ** -0.5),
        "lru_conv_b": _normal(ks[20], (N_LRU_LAYERS, D_RNN), 0.02),
        "lru_w_a": _normal(ks[21], (N_LRU_LAYERS, 2, LRU_BLOCKS, LRU_BLOCK_W, LRU_BLOCK_W), LRU_BLOCK_W ** -0.5),
        "lru_b_a": _normal(ks[22], (N_LRU_LAYERS, 2, D_RNN), 0.02),
        "lru_w_i": _normal(ks[23], (N_LRU_LAYERS, 2, LRU_BLOCKS, LRU_BLOCK_W, LRU_BLOCK_W), LRU_BLOCK_W ** -0.5),
        "lru_b_i": _normal(ks[24], (N_LRU_LAYERS, 2, D_RNN), 0.02),
        "lru_a_param": jnp.log(a0) - jnp.log1p(-a0),
        "lru_w_out": _normal(ks[26], (N_LRU_LAYERS, D_RNN, D), BETA * D_RNN ** -0.5),
    }


def reference(x, c, ctx, c_ctx, w_ada, b_ada, ln_g, ln_b, ffn_w_gate, ffn_w_up, ffn_w_down,
              attn_w_qkv, attn_w_o, attn_lambda_q1, attn_lambda_k1, attn_lambda_q2, attn_lambda_k2,
              attn_subln_g, lru_w_in, lru_conv_w, lru_conv_b, lru_w_a, lru_b_a, lru_w_i, lru_b_i,
              lru_a_param, lru_w_out):
    B, S, D = x.shape
    xc = ctx
    for i in range(DEPTH):
        need_ctx = i < DEPTH - 1
        m = (jax.nn.silu(c) @ w_ada[i] + b_ada[i]).reshape(B, N_SUBLAYERS, 3, 1, D)
        mc = (jax.nn.silu(c_ctx) @ w_ada[i] + b_ada[i]).reshape(N_SUBLAYERS, 3, 1, D)

        x = ffn_sublayer(x, m[:, 0], ffn_w_gate[i, 0], ffn_w_up[i, 0], ffn_w_down[i, 0], ln_g[i, 0], ln_b[i, 0])
        xc = ffn_sublayer(xc, mc[0], ffn_w_gate[i, 0], ffn_w_up[i, 0], ffn_w_down[i, 0], ln_g[i, 0], ln_b[i, 0])

        sh, sc, gt = split_mod(m[:, 1])
        shc, scc, gtc = split_mod(mc[1])
        u = modulate(x, sh, sc)
        uc = modulate(xc, shc, scc)
        j = i // N_MIXERS
        if i % N_MIXERS == 0:
            lambda_init = 0.8 - 0.6 * math.exp(-0.3 * i)
            o, oc = diff_attention(u, uc, attn_w_qkv[j], attn_w_o[j], attn_lambda_q1[j], attn_lambda_k1[j],
                                   attn_lambda_q2[j], attn_lambda_k2[j], attn_subln_g[j], lambda_init, need_ctx)
        else:
            o, oc = rglru_mixer(u, uc, lru_w_in[j], lru_conv_w[j], lru_conv_b[j], lru_w_a[j], lru_b_a[j],
                                lru_w_i[j], lru_b_i[j], lru_a_param[j], lru_w_out[j], need_ctx)
        x = layer_norm(ALPHA * x + gt * o, ln_g[i, 1], ln_b[i, 1])
        if need_ctx:
            xc = layer_norm(ALPHA * xc + gtc * oc, ln_g[i, 1], ln_b[i, 1])

        x = ffn_sublayer(x, m[:, 2], ffn_w_gate[i, 1], ffn_w_up[i, 1], ffn_w_down[i, 1], ln_g[i, 2], ln_b[i, 2])
        if need_ctx:
            xc = ffn_sublayer(xc, mc[2], ffn_w_gate[i, 1], ffn_w_up[i, 1], ffn_w_down[i, 1], ln_g[i, 2], ln_b[i, 2])
    return x
```

```python
import functools
import math

import jax
import jax.numpy as jnp
from jax import lax
from jax.experimental import pallas as pl
from jax.experimental.pallas import tpu as pltpu

F32 = jnp.float32
BF16 = jnp.bfloat16

SUBLANES = 8
LANES = 128
VMEM_LIMIT_BYTES = 56 * 1024 * 1024

GRID_W = 64
DA_HEAD_DIM = 64
DA_V_DIM = 2 * DA_HEAD_DIM
ROPE_BASE = 10000.0
AXIS_ROT_DIM = DA_HEAD_DIM // 2
ROT_HALF = AXIS_ROT_DIM // 2
CONV_PAD_LEFT = 2
LRU_C = 8.0
FFN_RES_WEIGHT = 0.5
LN_EPS = 1e-6
GELU_C0 = math.sqrt(2.0 / math.pi)
GELU_C1 = 0.044715


def _params(semantics):
    return pltpu.CompilerParams(dimension_semantics=semantics, vmem_limit_bytes=VMEM_LIMIT_BYTES)


def _pick(n, prefs):
    for p in prefs:
        if n % p == 0:
            return p
    return n


def _layer_norm(y, g, b):
    mu = jnp.mean(y, axis=-1, keepdims=True)
    yc = y - mu
    var = jnp.mean(yc * yc, axis=-1, keepdims=True)
    return yc * lax.rsqrt(var + LN_EPS) * g + b


def _row_chunks(n_rows, chunk, fn):
    def body(r, carry):
        fn(pl.ds(pl.multiple_of(r * chunk, chunk), chunk))
        return carry
    lax.fori_loop(0, n_rows // chunk, body, 0)


def _modulate_rows(x_ref, mod_ref, u_ref, chunk):
    tm, d = x_ref.shape
    shift = mod_ref[0]
    scale1 = 1.0 + mod_ref[1]

    def fn(rows):
        xx = x_ref[rows, :].reshape(chunk // SUBLANES, SUBLANES, d)
        u_ref[rows, :] = (xx * scale1 + shift).reshape(chunk, d).astype(u_ref.dtype)
    _row_chunks(tm, chunk, fn)


def _residual_ln_rows(x_ref, o_ref, gate, lng, lnb, alpha, chunk):
    tm, d = x_ref.shape

    def fn(rows):
        xx = x_ref[rows, :].reshape(chunk // SUBLANES, SUBLANES, d)
        hh = o_ref[rows, :].reshape(chunk // SUBLANES, SUBLANES, d)
        y = alpha * xx + gate * hh
        o_ref[rows, :] = _layer_norm(y, lng, lnb).reshape(chunk, d)
    _row_chunks(tm, chunk, fn)


def _ada_kernel(c_ref, w_ref, b_ref, o_ref):
    c = c_ref[...]
    s = c * jax.nn.sigmoid(c)
    o_ref[...] = jnp.dot(s.astype(BF16), w_ref[...].astype(BF16),
                         preferred_element_type=F32) + b_ref[...]


def _ada(cc, w_ada, b_ada):
    depth, d, n = w_ada.shape
    rows = cc.shape[0]
    tn = _pick(n, (1024, 512, 256, 128))
    return pl.pallas_call(
        _ada_kernel,
        grid=(depth, n // tn),
        in_specs=[
            pl.BlockSpec((rows, d), lambda l, j: (0, 0)),
            pl.BlockSpec((None, d, tn), lambda l, j: (l, 0, j)),
            pl.BlockSpec((None, 1, tn), lambda l, j: (l, 0, j)),
        ],
        out_specs=pl.BlockSpec((None, rows, tn), lambda l, j: (l, 0, j)),
        out_shape=jax.ShapeDtypeStruct((depth, rows, n), F32),
        compiler_params=_params(("arbitrary", "arbitrary")),
        name="ada",
    )(cc, w_ada, b_ada.reshape(depth, 1, n))


def _ffn_kernel(tmod_ref, x_ref, mod_ref, wg_ref, wu_ref, wd_ref, lng_ref, lnb_ref, o_ref, u_ref,
                *, alpha, chunk):
    f = pl.program_id(1)
    tm, d = x_ref.shape

    @pl.when(f == 0)
    def _init():
        _modulate_rows(x_ref, mod_ref, u_ref, chunk)
        o_ref[...] = jnp.zeros((tm, d), F32)

    u = u_ref[...]
    ga = jnp.dot(u, wg_ref[...], preferred_element_type=F32)
    up = jnp.dot(u, wu_ref[...], preferred_element_type=F32)
    h = (ga * jax.nn.sigmoid(ga) * up).astype(BF16)
    o_ref[...] += jnp.dot(h, wd_ref[...], preferred_element_type=F32)

    @pl.when(f == pl.num_programs(1) - 1)
    def _finish():
        _residual_ln_rows(x_ref, o_ref, FFN_RES_WEIGHT * mod_ref[2], lng_ref[...], lnb_ref[...],
                          alpha, chunk)


def _ffn(x, tile_mod, mod, wg, wu, wd, lng, lnb, alpha, tm):
    rows, d = x.shape
    ff = wg.shape[1]
    tf = _pick(ff, (512, 256, 128))
    chunk = _pick(tm, (128, 64, 32, 16))
    kern = functools.partial(_ffn_kernel, alpha=alpha, chunk=chunk)
    grid_spec = pltpu.PrefetchScalarGridSpec(
        num_scalar_prefetch=1,
        grid=(rows // tm, ff // tf),
        in_specs=[
            pl.BlockSpec((tm, d), lambda i, f, t: (i, 0)),
            pl.BlockSpec((None, 3, SUBLANES, d), lambda i, f, t: (t[i], 0, 0, 0)),
            pl.BlockSpec((d, tf), lambda i, f, t: (0, f)),
            pl.BlockSpec((d, tf), lambda i, f, t: (0, f)),
            pl.BlockSpec((tf, d), lambda i, f, t: (f, 0)),
            pl.BlockSpec((1, d), lambda i, f, t: (0, 0)),
            pl.BlockSpec((1, d), lambda i, f, t: (0, 0)),
        ],
        out_specs=pl.BlockSpec((tm, d), lambda i, f, t: (i, 0)),
        scratch_shapes=[pltpu.VMEM((tm, d), BF16)],
    )
    return pl.pallas_call(
        kern,
        grid_spec=grid_spec,
        out_shape=jax.ShapeDtypeStruct((rows, d), F32),
        compiler_params=_params(("arbitrary", "arbitrary")),
        name="ffn",
    )(tile_mod, x, mod, wg, wu, wd, lng.reshape(1, d), lnb.reshape(1, d))


def _qkv_kernel(tmod_ref, x_ref, mod_ref, w_ref, cs_ref, sa_ref, sb_ref, o_ref, u_ref, *, chunk):
    j = pl.program_id(2)

    @pl.when(j == 0)
    def _init():
        _modulate_rows(x_ref, mod_ref, u_ref, chunk)

    acc = jnp.dot(u_ref[...], w_ref[...], preferred_element_type=F32)
    cs, sa, sb = cs_ref[...], sa_ref[...], sb_ref[...]
    for hh in range(o_ref.shape[0]):
        blk = acc[:, hh * LANES:(hh + 1) * LANES]
        rot = (blk * cs + pltpu.roll(blk, LANES - ROT_HALF, 1) * sa
               + pltpu.roll(blk, ROT_HALF, 1) * sb)
        o_ref[hh] = rot.astype(o_ref.dtype)


def _qkv(x, batch, tile_mod, mod, w, cs, sa, sb):
    rows, d = x.shape
    n = rows // batch
    heads = d // DA_V_DIM
    tm = _pick(n, (1024, 512, 256, 128))
    tn = _pick(d, (512, 256, 128))
    hpc = tn // LANES
    cpw = d // tn
    chunk = _pick(tm, (128, 64, 32, 16))
    nt = n // tm
    grid_spec = pltpu.PrefetchScalarGridSpec(
        num_scalar_prefetch=1,
        grid=(batch, nt, 3 * cpw),
        in_specs=[
            pl.BlockSpec((tm, d), lambda b, t, j, tmr: (b * nt + t, 0)),
            pl.BlockSpec((None, 3, SUBLANES, d), lambda b, t, j, tmr: (tmr[b * nt + t], 0, 0, 0)),
            pl.BlockSpec((d, tn), lambda b, t, j, tmr: (0, j)),
            pl.BlockSpec((None, tm, LANES), lambda b, t, j, tmr: (j // cpw, t, 0)),
            pl.BlockSpec((None, tm, LANES), lambda b, t, j, tmr: (j // cpw, t, 0)),
            pl.BlockSpec((None, tm, LANES), lambda b, t, j, tmr: (j // cpw, t, 0)),
        ],
        out_specs=pl.BlockSpec((None, None, hpc, tm, LANES),
                               lambda b, t, j, tmr: (j // cpw, b, j % cpw, t, 0)),
        scratch_shapes=[pltpu.VMEM((tm, d), BF16)],
    )
    return pl.pallas_call(
        functools.partial(_qkv_kernel, chunk=chunk),
        grid_spec=grid_spec,
        out_shape=jax.ShapeDtypeStruct((3, batch, heads, n, LANES), BF16),
        compiler_params=_params(("arbitrary", "arbitrary", "arbitrary")),
        name="qkv",
    )(tile_mod, x, mod, w, cs, sa, sb)


def _rope_tables(n, rotate):
    scale_q = DA_HEAD_DIM ** -0.5
    ones = jnp.ones((n, LANES), F32)
    zeros = jnp.zeros((n, LANES), F32)
    if not rotate:
        return (jnp.stack([scale_q * ones, ones, ones]), jnp.stack([zeros] * 3),
                jnp.stack([zeros] * 3))
    rows = n // GRID_W
    row = jnp.repeat(jnp.arange(rows, dtype=F32), GRID_W)
    col = jnp.tile(jnp.arange(GRID_W, dtype=F32), rows)
    inv_freq = ROPE_BASE ** (-jnp.arange(0, AXIS_ROT_DIM, 2, dtype=F32) / AXIS_ROT_DIM)
    ang_r = row[:, None] * inv_freq[None, :]
    ang_c = col[:, None] * inv_freq[None, :]
    z = jnp.zeros_like(ang_r)
    cos64 = jnp.concatenate([jnp.cos(ang_r), jnp.cos(ang_r), jnp.cos(ang_c), jnp.cos(ang_c)], -1)
    sa64 = jnp.concatenate([-jnp.sin(ang_r), z, -jnp.sin(ang_c), z], -1)
    sb64 = jnp.concatenate([z, jnp.sin(ang_r), z, jnp.sin(ang_c)], -1)
    cs = jnp.tile(cos64, (1, 2))
    sa = jnp.tile(sa64, (1, 2))
    sb = jnp.tile(sb64, (1, 2))
    return (jnp.stack([scale_q * cs, cs, ones]), jnp.stack([scale_q * sa, sa, zeros]),
            jnp.stack([scale_q * sb, sb, zeros]))


def _attn_kernel(*refs, n_parts, lambda_init):
    q_ref = refs[0]
    kv_refs = refs[1:1 + 2 * n_parts]
    lq1_ref, lk1_ref, lq2_ref, lk2_ref, g_ref, o_ref = refs[1 + 2 * n_parts:]
    lam = (jnp.exp(jnp.sum(lq1_ref[...] * lk1_ref[...], axis=-1, keepdims=True))
           - jnp.exp(jnp.sum(lq2_ref[...] * lk2_ref[...], axis=-1, keepdims=True)) + lambda_init)
    q = q_ref[...]
    lane = lax.broadcasted_iota(jnp.int32, q.shape, 1)
    zero = jnp.zeros_like(q)
    qs = (jnp.where(lane < DA_HEAD_DIM, q, zero), jnp.where(lane >= DA_HEAD_DIM, q, zero))
    nt_dims = (((1,), (1,)), ((), ()))
    scores = [[lax.dot_general(qe, kv_refs[2 * p][...], nt_dims, preferred_element_type=F32)
               for p in range(n_parts)] for qe in qs]
    weights = []
    for e in range(2):
        m = functools.reduce(jnp.maximum,
                             [jnp.max(s, axis=-1, keepdims=True) for s in scores[e]])
        ps = [jnp.exp(s - m) for s in scores[e]]
        l = functools.reduce(jnp.add, [jnp.sum(p, axis=-1, keepdims=True) for p in ps])
        weights.append((ps, 1.0 / l))
    o = None
    for p in range(n_parts):
        a = weights[0][0][p] * weights[0][1] - weights[1][0][p] * (lam * weights[1][1])
        part = jnp.dot(a.astype(BF16), kv_refs[2 * p + 1][...], preferred_element_type=F32)
        o = part if o is None else o + part
    y = o * lax.rsqrt(jnp.mean(o * o, axis=-1, keepdims=True) + LN_EPS) * g_ref[...]
    o_ref[...] = (y * (1.0 - lambda_init)).astype(o_ref.dtype)


def _attention(qkv, kv_parts, lam_params, subln_g, lambda_init):
    _, batch, heads, n, _ = qkv.shape
    tq = _pick(n, (256, 128))
    in_specs = [pl.BlockSpec((None, None, None, tq, LANES), lambda b, h, t: (0, b, h, t, 0))]
    args = [qkv]
    for part in kv_parts:
        m = part.shape[3]
        in_specs.append(pl.BlockSpec((None, None, None, m, LANES), lambda b, h, t: (1, b, h, 0, 0)))
        in_specs.append(pl.BlockSpec((None, None, None, m, LANES), lambda b, h, t: (2, b, h, 0, 0)))
        args += [part, part]
    for p in lam_params:
        in_specs.append(pl.BlockSpec((1, DA_HEAD_DIM), lambda b, h, t: (0, 0)))
        args.append(p.reshape(1, DA_HEAD_DIM))
    in_specs.append(pl.BlockSpec((1, DA_V_DIM), lambda b, h, t: (0, 0)))
    args.append(subln_g.reshape(1, DA_V_DIM))
    return pl.pallas_call(
        functools.partial(_attn_kernel, n_parts=len(kv_parts), lambda_init=lambda_init),
        grid=(batch, heads, n // tq),
        in_specs=in_specs,
        out_specs=pl.BlockSpec((None, tq, LANES), lambda b, h, t: (b, t, h)),
        out_shape=jax.ShapeDtypeStruct((batch, n, heads * LANES), BF16),
        compiler_params=_params(("arbitrary", "arbitrary", "arbitrary")),
        name="attn",
    )(*args)


def _proj_ln_kernel(tmod_ref, h_ref, x_ref, mod_ref, w_ref, lng_ref, lnb_ref, o_ref, *, alpha, chunk):
    o_ref[...] = jnp.dot(h_ref[...], w_ref[...], preferred_element_type=F32)
    _residual_ln_rows(x_ref, o_ref, mod_ref[2], lng_ref[...], lnb_ref[...], alpha, chunk)


def _proj_ln(h, x, tile_mod, mod, w, lng, lnb, alpha, tm):
    rows, d = x.shape
    k = h.shape[1]
    chunk = _pick(tm, (128, 64, 32, 16))
    grid_spec = pltpu.PrefetchScalarGridSpec(
        num_scalar_prefetch=1,
        grid=(rows // tm,),
        in_specs=[
            pl.BlockSpec((tm, k), lambda i, t: (i, 0)),
            pl.BlockSpec((tm, d), lambda i, t: (i, 0)),
            pl.BlockSpec((None, 3, SUBLANES, d), lambda i, t: (t[i], 0, 0, 0)),
            pl.BlockSpec((k, d), lambda i, t: (0, 0)),
            pl.BlockSpec((1, d), lambda i, t: (0, 0)),
            pl.BlockSpec((1, d), lambda i, t: (0, 0)),
        ],
        out_specs=pl.BlockSpec((tm, d), lambda i, t: (i, 0)),
    )
    return pl.pallas_call(
        functools.partial(_proj_ln_kernel, alpha=alpha, chunk=chunk),
        grid_spec=grid_spec,
        out_shape=jax.ShapeDtypeStruct((rows, d), F32),
        compiler_params=_params(("arbitrary",)),
        name="proj_ln",
    )(tile_mod, h, x, mod, w, lng.reshape(1, d), lnb.reshape(1, d))


def _lru_in_kernel(tmod_ref, x_ref, mod_ref, w_ref, o_ref, u_ref, *, n_gelu, chunk):
    j = pl.program_id(1)

    @pl.when(j == 0)
    def _init():
        _modulate_rows(x_ref, mod_ref, u_ref, chunk)

    acc = jnp.dot(u_ref[...], w_ref[...], preferred_element_type=F32)

    @pl.when(j < n_gelu)
    def _gate_branch():
        cdf = 0.5 * (1.0 + jnp.tanh(GELU_C0 * (acc + GELU_C1 * (acc * acc * acc))))
        o_ref[...] = acc * cdf

    @pl.when(j >= n_gelu)
    def _recurrent_branch():
        o_ref[...] = acc


def _lru_in(x, tile_mod, mod, w, tm):
    rows, d = x.shape
    n = w.shape[1]
    tn = _pick(n // 2, (512, 256, 128))
    chunk = _pick(tm, (128, 64, 32, 16))
    grid_spec = pltpu.PrefetchScalarGridSpec(
        num_scalar_prefetch=1,
        grid=(rows // tm, n // tn),
        in_specs=[
            pl.BlockSpec((tm, d), lambda i, j, t: (i, 0)),
            pl.BlockSpec((None, 3, SUBLANES, d), lambda i, j, t: (t[i], 0, 0, 0)),
            pl.BlockSpec((d, tn), lambda i, j, t: (0, j)),
        ],
        out_specs=pl.BlockSpec((tm, tn), lambda i, j, t: (i, j)),
        scratch_shapes=[pltpu.VMEM((tm, d), BF16)],
    )
    return pl.pallas_call(
        functools.partial(_lru_in_kernel, n_gelu=(n // 2) // tn, chunk=chunk),
        grid_spec=grid_spec,
        out_shape=jax.ShapeDtypeStruct((rows, n), F32),
        compiler_params=_params(("arbitrary", "arbitrary")),
        name="lru_in",
    )(tile_mod, x, mod, w)


def _lru_scan_kernel(*refs, reverse, combine, n_chunks, n_taps):
    if combine:
        (xr_ref, prev_ref, next_ref, cw_ref, cb_ref, wa_ref, ba_ref, wi_ref, bi_ref, ap_ref, h0_ref,
         hf_ref, g_ref, o_ref, hl_ref, a_s, b_s, carry) = refs
    else:
        (xr_ref, prev_ref, next_ref, cw_ref, cb_ref, wa_ref, ba_ref, wi_ref, bi_ref, ap_ref, h0_ref,
         o_ref, hl_ref, a_s, b_s, carry) = refs
    c = pl.program_id(1)
    c_eff = n_chunks - 1 - c if reverse else c
    rows, width = xr_ref.shape
    steps = rows // SUBLANES

    @pl.when(c == 0)
    def _init():
        carry[...] = h0_ref[...]

    halo_l = jnp.where(c_eff > 0, prev_ref[...], 0.0)
    halo_r = jnp.where(c_eff < n_chunks - 1, next_ref[...], 0.0)
    ext = jnp.concatenate([halo_l, xr_ref[...], halo_r], axis=0)
    xc = cb_ref[...]
    for j in range(n_taps):
        xc = xc + ext[j * SUBLANES:j * SUBLANES + rows] * cw_ref[j:j + 1, :]
    xb = xc.astype(BF16)
    r = jax.nn.sigmoid(jnp.dot(xb, wa_ref[...], preferred_element_type=F32) + ba_ref[...])
    i = jax.nn.sigmoid(jnp.dot(xb, wi_ref[...], preferred_element_type=F32) + bi_ref[...])
    z = -ap_ref[...]
    softplus = jnp.maximum(z, 0.0) + jnp.log(1.0 + jnp.exp(-jnp.abs(z)))
    log_a = (-LRU_C) * r * softplus
    a = jnp.exp(log_a)
    a_s[...] = a
    b_s[...] = jnp.sqrt(1.0 - a * a) * i * xc

    def step(t, h):
        tt = steps - 1 - t if reverse else t
        sl = pl.ds(pl.multiple_of(tt * SUBLANES, SUBLANES), SUBLANES)
        h = a_s[sl, :] * h + b_s[sl, :]
        b_s[sl, :] = h
        return h
    h_last = lax.fori_loop(0, steps, step, carry[...], unroll=8)
    carry[...] = h_last
    hl_ref[...] = h_last
    if combine:
        o_ref[...] = ((hf_ref[...] + b_s[...]) * g_ref[...]).astype(o_ref.dtype)
    else:
        o_ref[...] = b_s[...]


def _lru_scan(y, conv_w, conv_b, w_a, b_a, w_i, b_i, a_param, h0, direction, reverse, hf=None):
    rows = y.shape[0]
    d_rnn = y.shape[1] // 2
    n_taps = conv_w.shape[0]
    width = w_a.shape[-1]
    n_blk = d_rnn // width
    tokens = rows // SUBLANES
    tc = _pick(tokens, (256, 128, 64, 32, 16, 8))
    rc = tc * SUBLANES
    n_chunks = tokens // tc
    halo_l = CONV_PAD_LEFT * SUBLANES
    halo_r = (n_taps - 1 - CONV_PAD_LEFT) * SUBLANES
    combine = hf is not None
    last_l = rows // halo_l - 1
    last_r = rows // halo_r - 1

    def ce(c):
        return n_chunks - 1 - c if reverse else c

    vec = lambda: pl.BlockSpec((None, 1, width), lambda h, c: (direction, 0, h))
    in_specs = [
        pl.BlockSpec((rc, width), lambda h, c: (ce(c), n_blk + h)),
        pl.BlockSpec((halo_l, width),
                     lambda h, c: (jnp.maximum(ce(c) * (rc // halo_l) - 1, 0), n_blk + h)),
        pl.BlockSpec((halo_r, width),
                     lambda h, c: (jnp.minimum((ce(c) + 1) * (rc // halo_r), last_r), n_blk + h)),
        pl.BlockSpec((n_taps, width), lambda h, c: (0, h)),
        pl.BlockSpec((1, width), lambda h, c: (0, h)),
        pl.BlockSpec((None, None, width, width), lambda h, c: (direction, h, 0, 0)),
        vec(),
        pl.BlockSpec((None, None, width, width), lambda h, c: (direction, h, 0, 0)),
        vec(),
        vec(),
        pl.BlockSpec((SUBLANES, width), lambda h, c: (0, h)),
    ]
    args = [y, y, y, conv_w, conv_b.reshape(1, d_rnn), w_a, b_a.reshape(2, 1, d_rnn), w_i,
            b_i.reshape(2, 1, d_rnn), a_param.reshape(2, 1, d_rnn), h0]
    if combine:
        in_specs += [pl.BlockSpec((rc, width), lambda h, c: (ce(c), h)),
                     pl.BlockSpec((rc, width), lambda h, c: (ce(c), h))]
        args += [hf, y]
    del last_l
    out_dtype = BF16 if combine else F32
    return pl.pallas_call(
        functools.partial(_lru_scan_kernel, reverse=reverse, combine=combine, n_chunks=n_chunks,
                          n_taps=n_taps),
        grid=(n_blk, n_chunks),
        in_specs=in_specs,
        out_specs=[pl.BlockSpec((rc, width), lambda h, c: (ce(c), h)),
                   pl.BlockSpec((SUBLANES, width), lambda h, c: (0, h))],
        out_shape=[jax.ShapeDtypeStruct((rows, d_rnn), out_dtype),
                   jax.ShapeDtypeStruct((SUBLANES, d_rnn), F32)],
        scratch_shapes=[pltpu.VMEM((rc, width), F32), pltpu.VMEM((rc, width), F32),
                        pltpu.VMEM((SUBLANES, width), F32)],
        compiler_params=_params(("arbitrary", "arbitrary")),
        name="lru_scan",
    )(*args)


def _bcast_mod(m):
    return jnp.broadcast_to(m[..., None, :], m.shape[:-1] + (SUBLANES, m.shape[-1]))


def kernel(x, c, ctx, c_ctx, w_ada, b_ada, ln_g, ln_b, ffn_w_gate, ffn_w_up, ffn_w_down, attn_w_qkv, attn_w_o, attn_lambda_q1, attn_lambda_k1, attn_lambda_q2, attn_lambda_k2, attn_subln_g, lru_w_in, lru_conv_w, lru_conv_b, lru_w_a, lru_b_a, lru_w_i, lru_b_i, lru_a_param, lru_w_out):
    batch, seq, d = x.shape
    n_ctx = ctx.shape[1]
    depth = w_ada.shape[0]
    assert depth == 2 and batch == SUBLANES
    alpha = (2.0 * depth) ** 0.25

    ada_rows = 2 * SUBLANES
    cc = jnp.zeros((ada_rows, d), F32).at[:batch].set(c).at[batch].set(c_ctx)
    ada = _ada(cc, w_ada, b_ada).reshape(depth, ada_rows, 3, 3, d)

    tm_lat = _pick(seq, (512, 256, 128))
    tm_ctx = _pick(batch * n_ctx, (512, 256, 128))
    lat_tiles = batch * seq // tm_lat
    ctx_tiles = batch * n_ctx // tm_ctx
    zeros_lat = jnp.zeros((lat_tiles,), jnp.int32)
    zeros_ctx = jnp.zeros((ctx_tiles,), jnp.int32)
    bf = lambda w: w.astype(BF16)

    mod_b = _bcast_mod(ada[0, :batch])
    mod_c = _bcast_mod(ada[0, batch:batch + 1])
    tile_b = jnp.arange(lat_tiles, dtype=jnp.int32) // (seq // tm_lat)
    xl = x.reshape(batch * seq, d)
    xc = ctx.reshape(batch * n_ctx, d)

    wg, wu, wd = bf(ffn_w_gate[0, 0]), bf(ffn_w_up[0, 0]), bf(ffn_w_down[0, 0])
    xl = _ffn(xl, tile_b, mod_b[:, 0], wg, wu, wd, ln_g[0, 0], ln_b[0, 0], alpha, tm_lat)
    xc = _ffn(xc, zeros_ctx, mod_c[:, 0], wg, wu, wd, ln_g[0, 0], ln_b[0, 0], alpha, tm_ctx)

    lambda_init = 0.8 - 0.6 * math.exp(-0.3 * 0)
    w_qkv = bf(attn_w_qkv[0])
    tq_lat = _pick(seq, (1024, 512, 256, 128))
    tile_q = jnp.arange(batch * seq // tq_lat, dtype=jnp.int32) // (seq // tq_lat)
    qkv_l = _qkv(xl, batch, tile_q, mod_b[:, 1], w_qkv, *_rope_tables(seq, True))
    tq_ctx = _pick(n_ctx, (1024, 512, 256, 128))
    qkv_c = _qkv(xc, batch, jnp.zeros((batch * n_ctx // tq_ctx,), jnp.int32), mod_c[:, 1], w_qkv,
                 *_rope_tables(n_ctx, False))
    lam_params = (attn_lambda_q1[0], attn_lambda_k1[0], attn_lambda_q2[0], attn_lambda_k2[0])
    o_l = _attention(qkv_l, (qkv_c, qkv_l), lam_params, attn_subln_g[0], lambda_init)
    o_c = _attention(qkv_c, (qkv_c,), lam_params, attn_subln_g[0], lambda_init)
    w_o = bf(attn_w_o[0])
    xl = _proj_ln(o_l.reshape(batch * seq, d), xl, tile_b, mod_b[:, 1], w_o, ln_g[0, 1], ln_b[0, 1],
                  alpha, tm_lat)
    xc = _proj_ln(o_c.reshape(batch * n_ctx, d), xc, zeros_ctx, mod_c[:, 1], w_o, ln_g[0, 1],
                  ln_b[0, 1], alpha, tm_ctx)

    wg, wu, wd = bf(ffn_w_gate[0, 1]), bf(ffn_w_up[0, 1]), bf(ffn_w_down[0, 1])
    xl = _ffn(xl, tile_b, mod_b[:, 2], wg, wu, wd, ln_g[0, 2], ln_b[0, 2], alpha, tm_lat)
    xc = _ffn(xc, zeros_ctx, mod_c[:, 2], wg, wu, wd, ln_g[0, 2], ln_b[0, 2], alpha, tm_ctx)

    xl = xl.reshape(batch, seq, d).transpose(1, 0, 2).reshape(seq * batch, d)
    xc = xc.reshape(batch, n_ctx, d).transpose(1, 0, 2).reshape(n_ctx * batch, d)
    mod_p = ada[1, :batch].transpose(1, 2, 0, 3)[None]
    mod_c = _bcast_mod(ada[1, batch:batch + 1])

    wg, wu, wd = bf(ffn_w_gate[1, 0]), bf(ffn_w_up[1, 0]), bf(ffn_w_down[1, 0])
    xl = _ffn(xl, zeros_lat, mod_p[:, 0], wg, wu, wd, ln_g[1, 0], ln_b[1, 0], alpha, tm_lat)
    xc = _ffn(xc, zeros_ctx, mod_c[:, 0], wg, wu, wd, ln_g[1, 0], ln_b[1, 0], alpha, tm_ctx)

    w_in = bf(lru_w_in[0])
    y_l = _lru_in(xl, zeros_lat, mod_p[:, 1], w_in, tm_lat)
    y_c = _lru_in(xc, zeros_ctx, mod_c[:, 1], w_in, tm_ctx)
    d_rnn = w_in.shape[1] // 2
    w_a, w_i = bf(lru_w_a[0]), bf(lru_w_i[0])
    scan = functools.partial(_lru_scan, conv_w=lru_conv_w[0], conv_b=lru_conv_b[0], w_a=w_a,
                             b_a=lru_b_a[0], w_i=w_i, b_i=lru_b_i[0], a_param=lru_a_param[0])
    h_zero = jnp.zeros((SUBLANES, d_rnn), F32)
    _, h0_f = scan(y_c, h0=h_zero, direction=0, reverse=False)
    _, h0_b = scan(y_c, h0=h_zero, direction=1, reverse=True)
    hf, _ = scan(y_l, h0=h0_f, direction=0, reverse=False)
    hg, _ = scan(y_l, h0=h0_b, direction=1, reverse=True, hf=hf)
    xl = _proj_ln(hg, xl, zeros_lat, mod_p[:, 1], bf(lru_w_out[0]), ln_g[1, 1], ln_b[1, 1], alpha,
                  tm_lat)

    wg, wu, wd = bf(ffn_w_gate[1, 1]), bf(ffn_w_up[1, 1]), bf(ffn_w_down[1, 1])
    xl = _ffn(xl, zeros_lat, mod_p[:, 2], wg, wu, wd, ln_g[1, 2], ln_b[1, 2], alpha, tm_lat)
    return xl.reshape(seq, batch, d).transpose(1, 0, 2)
```

```python
import functools
import math

import jax
import jax.numpy as jnp
from jax import lax
from jax.experimental import pallas as pl
from jax.experimental.pallas import tpu as pltpu

F32 = jnp.float32
BF16 = jnp.bfloat16

SUBLANES = 8
LANES = 128
VMEM_LIMIT_BYTES = 56 * 1024 * 1024
ATTN_SUB_ROWS = 256
ATTN_SUB_TILES = 4

GRID_W = 64
DA_HEAD_DIM = 64
DA_V_DIM = 2 * DA_HEAD_DIM
ROPE_BASE = 10000.0
AXIS_ROT_DIM = DA_HEAD_DIM // 2
ROT_HALF = AXIS_ROT_DIM // 2
CONV_PAD_LEFT = 2
LRU_C = 8.0
FFN_RES_WEIGHT = 0.5
LN_EPS = 1e-6
GELU_C0 = math.sqrt(2.0 / math.pi)
GELU_C1 = 0.044715


def _params(semantics):
    return pltpu.CompilerParams(dimension_semantics=semantics, vmem_limit_bytes=VMEM_LIMIT_BYTES)


def _pick(n, prefs):
    for p in prefs:
        if n % p == 0:
            return p
    return n


def _layer_norm(y, g, b):
    mu = jnp.mean(y, axis=-1, keepdims=True)
    yc = y - mu
    var = jnp.mean(yc * yc, axis=-1, keepdims=True)
    return yc * lax.rsqrt(var + LN_EPS) * g + b


def _row_chunks(n_rows, chunk, fn):
    def body(r, carry):
        fn(pl.ds(pl.multiple_of(r * chunk, chunk), chunk))
        return carry
    lax.fori_loop(0, n_rows // chunk, body, 0)


def _modulate_rows(x_ref, mod_ref, u_ref, chunk):
    tm, d = x_ref.shape
    shift = mod_ref[0]
    scale1 = 1.0 + mod_ref[1]

    def fn(rows):
        xx = x_ref[rows, :].reshape(chunk // SUBLANES, SUBLANES, d)
        u_ref[rows, :] = (xx * scale1 + shift).reshape(chunk, d).astype(u_ref.dtype)
    _row_chunks(tm, chunk, fn)


def _residual_ln_rows(x_ref, o_ref, gate, lng, lnb, alpha, chunk):
    tm, d = x_ref.shape

    def fn(rows):
        xx = x_ref[rows, :].reshape(chunk // SUBLANES, SUBLANES, d)
        hh = o_ref[rows, :].reshape(chunk // SUBLANES, SUBLANES, d)
        y = alpha * xx + gate * hh
        o_ref[rows, :] = _layer_norm(y, lng, lnb).reshape(chunk, d)
    _row_chunks(tm, chunk, fn)


def _ada_kernel(c_ref, w_ref, b_ref, o_ref):
    c = c_ref[...]
    s = c * jax.nn.sigmoid(c)
    o_ref[...] = jnp.dot(s.astype(BF16), w_ref[...].astype(BF16),
                         preferred_element_type=F32) + b_ref[...]


def _ada(cc, w_ada, b_ada):
    depth, d, n = w_ada.shape
    rows = cc.shape[0]
    tn = _pick(n, (1024, 512, 256, 128))
    return pl.pallas_call(
        _ada_kernel,
        grid=(depth, n // tn),
        in_specs=[
            pl.BlockSpec((rows, d), lambda l, j: (0, 0)),
            pl.BlockSpec((None, d, tn), lambda l, j: (l, 0, j)),
            pl.BlockSpec((None, 1, tn), lambda l, j: (l, 0, j)),
        ],
        out_specs=pl.BlockSpec((None, rows, tn), lambda l, j: (l, 0, j)),
        out_shape=jax.ShapeDtypeStruct((depth, rows, n), F32),
        compiler_params=_params(("arbitrary", "arbitrary")),
        name="ada",
    )(cc, w_ada, b_ada.reshape(depth, 1, n))


def _ffn_kernel(tmod_ref, x_ref, mod_ref, wg_ref, wu_ref, wd_ref, lng_ref, lnb_ref, o_ref, u_ref,
                *, alpha, chunk):
    f = pl.program_id(1)
    tm, d = x_ref.shape

    @pl.when(f == 0)
    def _init():
        _modulate_rows(x_ref, mod_ref, u_ref, chunk)
        o_ref[...] = jnp.zeros((tm, d), F32)

    u = u_ref[...]
    ga = jnp.dot(u, wg_ref[...], preferred_element_type=F32)
    up = jnp.dot(u, wu_ref[...], preferred_element_type=F32)
    h = (ga * jax.nn.sigmoid(ga) * up).astype(BF16)
    o_ref[...] += jnp.dot(h, wd_ref[...], preferred_element_type=F32)

    @pl.when(f == pl.num_programs(1) - 1)
    def _finish():
        _residual_ln_rows(x_ref, o_ref, FFN_RES_WEIGHT * mod_ref[2], lng_ref[...], lnb_ref[...],
                          alpha, chunk)


def _ffn(x, tile_mod, mod, wg, wu, wd, lng, lnb, alpha, tm, in_tb=None, out_tb=None):
    d, ff = wg.shape
    rows = x.size // d
    tf = _pick(ff, (512, 256, 128))
    chunk = _pick(tm, (128, 64, 32, 16))
    kern = functools.partial(_ffn_kernel, alpha=alpha, chunk=chunk)

    def row_map(tb):
        if tb is None:
            return lambda i, f, t: (i, 0)
        nt = tb[1] // tm
        return lambda i, f, t: (i % nt, i // nt)

    out_rows_cols = (rows, d) if out_tb is None else (out_tb[1], out_tb[0] * d)
    grid_spec = pltpu.PrefetchScalarGridSpec(
        num_scalar_prefetch=1,
        grid=(rows // tm, ff // tf),
        in_specs=[
            pl.BlockSpec((tm, d), row_map(in_tb)),
            pl.BlockSpec((None, 3, SUBLANES, d), lambda i, f, t: (t[i], 0, 0, 0)),
            pl.BlockSpec((d, tf), lambda i, f, t: (0, f)),
            pl.BlockSpec((d, tf), lambda i, f, t: (0, f)),
            pl.BlockSpec((tf, d), lambda i, f, t: (f, 0)),
            pl.BlockSpec((1, d), lambda i, f, t: (0, 0)),
            pl.BlockSpec((1, d), lambda i, f, t: (0, 0)),
        ],
        out_specs=pl.BlockSpec((tm, d), row_map(out_tb)),
        scratch_shapes=[pltpu.VMEM((tm, d), BF16)],
    )
    return pl.pallas_call(
        kern,
        grid_spec=grid_spec,
        out_shape=jax.ShapeDtypeStruct(out_rows_cols, F32),
        compiler_params=_params(("arbitrary", "arbitrary")),
        name="ffn",
    )(tile_mod, x, mod, wg, wu, wd, lng.reshape(1, d), lnb.reshape(1, d))


def _qkv_kernel(tmod_ref, x_ref, mod_ref, w_ref, cs_ref, sa_ref, sb_ref, o_ref, u_ref, *, chunk):
    j = pl.program_id(2)

    @pl.when(j == 0)
    def _init():
        _modulate_rows(x_ref, mod_ref, u_ref, chunk)

    acc = jnp.dot(u_ref[...], w_ref[...], preferred_element_type=F32)
    cs, sa, sb = cs_ref[...], sa_ref[...], sb_ref[...]
    for hh in range(o_ref.shape[0]):
        blk = acc[:, hh * LANES:(hh + 1) * LANES]
        rot = (blk * cs + pltpu.roll(blk, LANES - ROT_HALF, 1) * sa
               + pltpu.roll(blk, ROT_HALF, 1) * sb)
        o_ref[hh] = rot.astype(o_ref.dtype)


def _qkv(x, batch, tile_mod, mod, w, cs, sa, sb):
    rows, d = x.shape
    n = rows // batch
    heads = d // DA_V_DIM
    tm = _pick(n, (1024, 512, 256, 128))
    tn = _pick(d, (512, 256, 128))
    hpc = tn // LANES
    cpw = d // tn
    chunk = _pick(tm, (128, 64, 32, 16))
    nt = n // tm
    grid_spec = pltpu.PrefetchScalarGridSpec(
        num_scalar_prefetch=1,
        grid=(batch, nt, 3 * cpw),
        in_specs=[
            pl.BlockSpec((tm, d), lambda b, t, j, tmr: (b * nt + t, 0)),
            pl.BlockSpec((None, 3, SUBLANES, d), lambda b, t, j, tmr: (tmr[b * nt + t], 0, 0, 0)),
            pl.BlockSpec((d, tn), lambda b, t, j, tmr: (0, j)),
            pl.BlockSpec((None, tm, LANES), lambda b, t, j, tmr: (j // cpw, t, 0)),
            pl.BlockSpec((None, tm, LANES), lambda b, t, j, tmr: (j // cpw, t, 0)),
            pl.BlockSpec((None, tm, LANES), lambda b, t, j, tmr: (j // cpw, t, 0)),
        ],
        out_specs=pl.BlockSpec((None, None, hpc, tm, LANES),
                               lambda b, t, j, tmr: (j // cpw, b, j % cpw, t, 0)),
        scratch_shapes=[pltpu.VMEM((tm, d), BF16)],
    )
    return pl.pallas_call(
        functools.partial(_qkv_kernel, chunk=chunk),
        grid_spec=grid_spec,
        out_shape=jax.ShapeDtypeStruct((3, batch, heads, n, LANES), BF16),
        compiler_params=_params(("arbitrary", "arbitrary", "arbitrary")),
        name="qkv",
    )(tile_mod, x, mod, w, cs, sa, sb)


def _rope_tables(n, rotate):
    scale_q = DA_HEAD_DIM ** -0.5 * math.log2(math.e)
    ones = jnp.ones((n, LANES), F32)
    zeros = jnp.zeros((n, LANES), F32)
    if not rotate:
        return (jnp.stack([scale_q * ones, ones, ones]), jnp.stack([zeros] * 3),
                jnp.stack([zeros] * 3))
    rows = n // GRID_W
    row = jnp.repeat(jnp.arange(rows, dtype=F32), GRID_W)
    col = jnp.tile(jnp.arange(GRID_W, dtype=F32), rows)
    inv_freq = ROPE_BASE ** (-jnp.arange(0, AXIS_ROT_DIM, 2, dtype=F32) / AXIS_ROT_DIM)
    ang_r = row[:, None] * inv_freq[None, :]
    ang_c = col[:, None] * inv_freq[None, :]
    z = jnp.zeros_like(ang_r)
    cos64 = jnp.concatenate([jnp.cos(ang_r), jnp.cos(ang_r), jnp.cos(ang_c), jnp.cos(ang_c)], -1)
    sa64 = jnp.concatenate([-jnp.sin(ang_r), z, -jnp.sin(ang_c), z], -1)
    sb64 = jnp.concatenate([z, jnp.sin(ang_r), z, jnp.sin(ang_c)], -1)
    cs = jnp.tile(cos64, (1, 2))
    sa = jnp.tile(sa64, (1, 2))
    sb = jnp.tile(sb64, (1, 2))
    return (jnp.stack([scale_q * cs, cs, ones]), jnp.stack([scale_q * sa, sa, zeros]),
            jnp.stack([scale_q * sb, sb, zeros]))


def _attn_kernel(*refs, n_parts, lambda_init, n_sub):
    q_ref = refs[0]
    k_refs = refs[1:1 + n_parts]
    v_refs = refs[1 + n_parts:1 + 2 * n_parts]
    lq1_ref, lk1_ref, lq2_ref, lk2_ref, g_ref, o_ref = refs[1 + 2 * n_parts:]
    lam = (jnp.exp(jnp.sum(lq1_ref[...] * lk1_ref[...], axis=-1, keepdims=True))
           - jnp.exp(jnp.sum(lq2_ref[...] * lk2_ref[...], axis=-1, keepdims=True)) + lambda_init)
    tq = q_ref.shape[0] // n_sub
    lane = lax.broadcasted_iota(jnp.int32, (tq, LANES), 1)
    nt_dims = (((1,), (1,)), ((), ()))
    order = None
    for t in range(n_sub):
        rows = slice(t * tq, (t + 1) * tq)
        q = q_ref[rows, :]
        zero = jnp.zeros_like(q)
        lane_t = lane if order is None else lane + order
        qs = (jnp.where(lane_t < DA_HEAD_DIM, q, zero), jnp.where(lane_t >= DA_HEAD_DIM, q, zero))
        scores = [[lax.dot_general(qe, k_ref[...], nt_dims, preferred_element_type=F32)
                   for k_ref in k_refs] for qe in qs]
        last = scores[1][-1]
        bits = pltpu.bitcast(last[tq - SUBLANES:, last.shape[1] - LANES:], jnp.uint32)
        order = ((bits >> 16) >> 16).astype(jnp.int32)[0:1, :]
        weights = []
        for e in range(2):
            m = functools.reduce(jnp.maximum,
                                 [jnp.max(s, axis=-1, keepdims=True) for s in scores[e]])
            ps = [jnp.exp2(s - m) for s in scores[e]]
            l = functools.reduce(jnp.add, [jnp.sum(p, axis=-1, keepdims=True) for p in ps])
            weights.append((ps, 1.0 / l))
        o = None
        for p, v_ref in enumerate(v_refs):
            a = weights[0][0][p] * weights[0][1] - weights[1][0][p] * (lam * weights[1][1])
            part = jnp.dot(a.astype(BF16), v_ref[...], preferred_element_type=F32)
            o = part if o is None else o + part
        y = o * lax.rsqrt(jnp.mean(o * o, axis=-1, keepdims=True) + LN_EPS) * g_ref[...]
        o_ref[rows, :] = (y * (1.0 - lambda_init)).astype(o_ref.dtype)


def _attention(qkv, kv_parts, lam_params, subln_g, lambda_init):
    _, batch, heads, n, _ = qkv.shape
    tq = _pick(n, (ATTN_SUB_TILES * ATTN_SUB_ROWS, ATTN_SUB_ROWS, 128))
    n_sub = max(tq // ATTN_SUB_ROWS, 1)
    in_specs = [pl.BlockSpec((None, None, None, tq, LANES), lambda b, h, t: (0, b, h, t, 0))]
    in_specs += [pl.BlockSpec((None, None, None, part.shape[3], LANES),
                              lambda b, h, t: (1, b, h, 0, 0)) for part in kv_parts]
    in_specs += [pl.BlockSpec((None, None, None, part.shape[3], LANES),
                              lambda b, h, t: (2, b, h, 0, 0)) for part in kv_parts]
    args = [qkv] + list(kv_parts) + list(kv_parts)
    for p in lam_params:
        in_specs.append(pl.BlockSpec((1, DA_HEAD_DIM), lambda b, h, t: (0, 0)))
        args.append(p.reshape(1, DA_HEAD_DIM))
    in_specs.append(pl.BlockSpec((1, DA_V_DIM), lambda b, h, t: (0, 0)))
    args.append(subln_g.reshape(1, DA_V_DIM))
    return pl.pallas_call(
        functools.partial(_attn_kernel, n_parts=len(kv_parts), lambda_init=lambda_init,
                          n_sub=n_sub),
        grid=(batch, heads, n // tq),
        in_specs=in_specs,
        out_specs=pl.BlockSpec((None, tq, LANES), lambda b, h, t: (b, t, h)),
        out_shape=jax.ShapeDtypeStruct((batch, n, heads * LANES), BF16),
        compiler_params=_params(("arbitrary", "arbitrary", "arbitrary")),
        name="attn",
    )(*args)


def _proj_ln_kernel(tmod_ref, h_ref, x_ref, mod_ref, w_ref, lng_ref, lnb_ref, o_ref, *, alpha, chunk):
    o_ref[...] = jnp.dot(h_ref[...], w_ref[...], preferred_element_type=F32)
    _residual_ln_rows(x_ref, o_ref, mod_ref[2], lng_ref[...], lnb_ref[...], alpha, chunk)


def _proj_ln(h, x, tile_mod, mod, w, lng, lnb, alpha, tm):
    rows, d = x.shape
    k = h.shape[1]
    chunk = _pick(tm, (128, 64, 32, 16))
    grid_spec = pltpu.PrefetchScalarGridSpec(
        num_scalar_prefetch=1,
        grid=(rows // tm,),
        in_specs=[
            pl.BlockSpec((tm, k), lambda i, t: (i, 0)),
            pl.BlockSpec((tm, d), lambda i, t: (i, 0)),
            pl.BlockSpec((None, 3, SUBLANES, d), lambda i, t: (t[i], 0, 0, 0)),
            pl.BlockSpec((k, d), lambda i, t: (0, 0)),
            pl.BlockSpec((1, d), lambda i, t: (0, 0)),
            pl.BlockSpec((1, d), lambda i, t: (0, 0)),
        ],
        out_specs=pl.BlockSpec((tm, d), lambda i, t: (i, 0)),
    )
    return pl.pallas_call(
        functools.partial(_proj_ln_kernel, alpha=alpha, chunk=chunk),
        grid_spec=grid_spec,
        out_shape=jax.ShapeDtypeStruct((rows, d), F32),
        compiler_params=_params(("arbitrary",)),
        name="proj_ln",
    )(tile_mod, h, x, mod, w, lng.reshape(1, d), lnb.reshape(1, d))


def _lru_in_kernel(tmod_ref, x_ref, mod_ref, w_ref, o_ref, u_ref, *, n_gelu, chunk):
    j = pl.program_id(1)

    @pl.when(j == 0)
    def _init():
        _modulate_rows(x_ref, mod_ref, u_ref, chunk)

    acc = jnp.dot(u_ref[...], w_ref[...], preferred_element_type=F32)

    @pl.when(j < n_gelu)
    def _gate_branch():
        cdf = 0.5 * (1.0 + jnp.tanh(GELU_C0 * (acc + GELU_C1 * (acc * acc * acc))))
        o_ref[...] = acc * cdf

    @pl.when(j >= n_gelu)
    def _recurrent_branch():
        o_ref[...] = acc


def _lru_in(x, tile_mod, mod, w, tm):
    rows, d = x.shape
    n = w.shape[1]
    tn = _pick(n // 2, (512, 256, 128))
    chunk = _pick(tm, (128, 64, 32, 16))
    grid_spec = pltpu.PrefetchScalarGridSpec(
        num_scalar_prefetch=1,
        grid=(rows // tm, n // tn),
        in_specs=[
            pl.BlockSpec((tm, d), lambda i, j, t: (i, 0)),
            pl.BlockSpec((None, 3, SUBLANES, d), lambda i, j, t: (t[i], 0, 0, 0)),
            pl.BlockSpec((d, tn), lambda i, j, t: (0, j)),
        ],
        out_specs=pl.BlockSpec((tm, tn), lambda i, j, t: (i, j)),
        scratch_shapes=[pltpu.VMEM((tm, d), BF16)],
    )
    return pl.pallas_call(
        functools.partial(_lru_in_kernel, n_gelu=(n // 2) // tn, chunk=chunk),
        grid_spec=grid_spec,
        out_shape=jax.ShapeDtypeStruct((rows, n), F32),
        compiler_params=_params(("arbitrary", "arbitrary")),
        name="lru_in",
    )(tile_mod, x, mod, w)


def _lru_scan_kernel(*refs, reverse, combine, n_chunks, n_taps):
    if combine:
        (xr_ref, prev_ref, next_ref, cw_ref, cb_ref, wa_ref, ba_ref, wi_ref, bi_ref, ap_ref, h0_ref,
         hf_ref, g_ref, o_ref, hl_ref, a_s, b_s, carry) = refs
    else:
        (xr_ref, prev_ref, next_ref, cw_ref, cb_ref, wa_ref, ba_ref, wi_ref, bi_ref, ap_ref, h0_ref,
         o_ref, hl_ref, a_s, b_s, carry) = refs
    c = pl.program_id(1)
    c_eff = n_chunks - 1 - c if reverse else c
    rows, width = xr_ref.shape
    steps = rows // SUBLANES

    @pl.when(c == 0)
    def _init():
        carry[...] = h0_ref[...]

    halo_l = jnp.where(c_eff > 0, prev_ref[...], 0.0)
    halo_r = jnp.where(c_eff < n_chunks - 1, next_ref[...], 0.0)
    ext = jnp.concatenate([halo_l, xr_ref[...], halo_r], axis=0)
    xc = cb_ref[...]
    for j in range(n_taps):
        xc = xc + ext[j * SUBLANES:j * SUBLANES + rows] * cw_ref[j:j + 1, :]
    xb = xc.astype(BF16)
    r = jax.nn.sigmoid(jnp.dot(xb, wa_ref[...], preferred_element_type=F32) + ba_ref[...])
    i = jax.nn.sigmoid(jnp.dot(xb, wi_ref[...], preferred_element_type=F32) + bi_ref[...])
    z = -ap_ref[...]
    softplus = jnp.maximum(z, 0.0) + jnp.log(1.0 + jnp.exp(-jnp.abs(z)))
    log_a = (-LRU_C) * r * softplus
    a = jnp.exp(log_a)
    a_s[...] = a
    b_s[...] = jnp.sqrt(1.0 - a * a) * i * xc

    def step(t, h):
        tt = steps - 1 - t if reverse else t
        sl = pl.ds(pl.multiple_of(tt * SUBLANES, SUBLANES), SUBLANES)
        h = a_s[sl, :] * h + b_s[sl, :]
        b_s[sl, :] = h
        return h
    h_last = lax.fori_loop(0, steps, step, carry[...], unroll=8)
    carry[...] = h_last
    hl_ref[...] = h_last
    if combine:
        o_ref[...] = ((hf_ref[...] + b_s[...]) * g_ref[...]).astype(o_ref.dtype)
    else:
        o_ref[...] = b_s[...]


def _lru_scan(y, conv_w, conv_b, w_a, b_a, w_i, b_i, a_param, h0, direction, reverse, hf=None):
    rows = y.shape[0]
    d_rnn = y.shape[1] // 2
    n_taps = conv_w.shape[0]
    width = w_a.shape[-1]
    n_blk = d_rnn // width
    tokens = rows // SUBLANES
    tc = _pick(tokens, (256, 128, 64, 32, 16, 8))
    rc = tc * SUBLANES
    n_chunks = tokens // tc
    halo_l = CONV_PAD_LEFT * SUBLANES
    halo_r = (n_taps - 1 - CONV_PAD_LEFT) * SUBLANES
    combine = hf is not None
    last_l = rows // halo_l - 1
    last_r = rows // halo_r - 1

    def ce(c):
        return n_chunks - 1 - c if reverse else c

    vec = lambda: pl.BlockSpec((None, 1, width), lambda h, c: (direction, 0, h))
    in_specs = [
        pl.BlockSpec((rc, width), lambda h, c: (ce(c), n_blk + h)),
        pl.BlockSpec((halo_l, width),
                     lambda h, c: (jnp.maximum(ce(c) * (rc // halo_l) - 1, 0), n_blk + h)),
        pl.BlockSpec((halo_r, width),
                     lambda h, c: (jnp.minimum((ce(c) + 1) * (rc // halo_r), last_r), n_blk + h)),
        pl.BlockSpec((n_taps, width), lambda h, c: (0, h)),
        pl.BlockSpec((1, width), lambda h, c: (0, h)),
        pl.BlockSpec((None, None, width, width), lambda h, c: (direction, h, 0, 0)),
        vec(),
        pl.BlockSpec((None, None, width, width), lambda h, c: (direction, h, 0, 0)),
        vec(),
        vec(),
        pl.BlockSpec((SUBLANES, width), lambda h, c: (0, h)),
    ]
    args = [y, y, y, conv_w, conv_b.reshape(1, d_rnn), w_a, b_a.reshape(2, 1, d_rnn), w_i,
            b_i.reshape(2, 1, d_rnn), a_param.reshape(2, 1, d_rnn), h0]
    if combine:
        in_specs += [pl.BlockSpec((rc, width), lambda h, c: (ce(c), h)),
                     pl.BlockSpec((rc, width), lambda h, c: (ce(c), h))]
        args += [hf, y]
    del last_l
    out_dtype = BF16 if combine else F32
    return pl.pallas_call(
        functools.partial(_lru_scan_kernel, reverse=reverse, combine=combine, n_chunks=n_chunks,
                          n_taps=n_taps),
        grid=(n_blk, n_chunks),
        in_specs=in_specs,
        out_specs=[pl.BlockSpec((rc, width), lambda h, c: (ce(c), h)),
                   pl.BlockSpec((SUBLANES, width), lambda h, c: (0, h))],
        out_shape=[jax.ShapeDtypeStruct((rows, d_rnn), out_dtype),
                   jax.ShapeDtypeStruct((SUBLANES, d_rnn), F32)],
        scratch_shapes=[pltpu.VMEM((rc, width), F32), pltpu.VMEM((rc, width), F32),
                        pltpu.VMEM((SUBLANES, width), F32)],
        compiler_params=_params(("arbitrary", "arbitrary")),
        name="lru_scan",
    )(*args)


def _bcast_mod(m):
    return jnp.broadcast_to(m[..., None, :], m.shape[:-1] + (SUBLANES, m.shape[-1]))


def kernel(x, c, ctx, c_ctx, w_ada, b_ada, ln_g, ln_b, ffn_w_gate, ffn_w_up, ffn_w_down, attn_w_qkv, attn_w_o, attn_lambda_q1, attn_lambda_k1, attn_lambda_q2, attn_lambda_k2, attn_subln_g, lru_w_in, lru_conv_w, lru_conv_b, lru_w_a, lru_b_a, lru_w_i, lru_b_i, lru_a_param, lru_w_out):
    batch, seq, d = x.shape
    n_ctx = ctx.shape[1]
    depth = w_ada.shape[0]
    assert depth == 2 and batch == SUBLANES
    alpha = (2.0 * depth) ** 0.25

    ada_rows = 2 * SUBLANES
    cc = jnp.zeros((ada_rows, d), F32).at[:batch].set(c).at[batch].set(c_ctx)
    ada = _ada(cc, w_ada, b_ada).reshape(depth, ada_rows, 3, 3, d)

    tm_lat = _pick(seq, (512, 256, 128))
    tm_ctx = _pick(batch * n_ctx, (512, 256, 128))
    tf_lat = _pick(seq, (1024, 512, 256, 128))
    tf_ctx = _pick(batch * n_ctx, (1024, 512, 256, 128))
    zeros_lat = jnp.zeros((batch * seq // tm_lat,), jnp.int32)
    zeros_ctx = jnp.zeros((batch * n_ctx // tm_ctx,), jnp.int32)
    zeros_lat_f = jnp.zeros((batch * seq // tf_lat,), jnp.int32)
    zeros_ctx_f = jnp.zeros((batch * n_ctx // tf_ctx,), jnp.int32)
    bf = lambda w: w.astype(BF16)

    def batch_of_tile(tile):
        return jnp.arange(batch * seq // tile, dtype=jnp.int32) // (seq // tile)

    mod_b = _bcast_mod(ada[0, :batch])
    mod_c = _bcast_mod(ada[0, batch:batch + 1])
    tile_b = batch_of_tile(tm_lat)
    tile_bf = batch_of_tile(tf_lat)
    xl = x.reshape(batch * seq, d)
    xc = ctx.reshape(batch * n_ctx, d)

    wg, wu, wd = bf(ffn_w_gate[0, 0]), bf(ffn_w_up[0, 0]), bf(ffn_w_down[0, 0])
    xl = _ffn(xl, tile_bf, mod_b[:, 0], wg, wu, wd, ln_g[0, 0], ln_b[0, 0], alpha, tf_lat)
    xc = _ffn(xc, zeros_ctx_f, mod_c[:, 0], wg, wu, wd, ln_g[0, 0], ln_b[0, 0], alpha, tf_ctx)

    lambda_init = 0.8 - 0.6 * math.exp(-0.3 * 0)
    w_qkv = bf(attn_w_qkv[0])
    tq_lat = _pick(seq, (1024, 512, 256, 128))
    tile_q = jnp.arange(batch * seq // tq_lat, dtype=jnp.int32) // (seq // tq_lat)
    qkv_l = _qkv(xl, batch, tile_q, mod_b[:, 1], w_qkv, *_rope_tables(seq, True))
    tq_ctx = _pick(n_ctx, (1024, 512, 256, 128))
    qkv_c = _qkv(xc, batch, jnp.zeros((batch * n_ctx // tq_ctx,), jnp.int32), mod_c[:, 1], w_qkv,
                 *_rope_tables(n_ctx, False))
    lam_params = (attn_lambda_q1[0], attn_lambda_k1[0], attn_lambda_q2[0], attn_lambda_k2[0])
    o_l = _attention(qkv_l, (qkv_c, qkv_l), lam_params, attn_subln_g[0], lambda_init)
    o_c = _attention(qkv_c, (qkv_c,), lam_params, attn_subln_g[0], lambda_init)
    w_o = bf(attn_w_o[0])
    xl = _proj_ln(o_l.reshape(batch * seq, d), xl, tile_b, mod_b[:, 1], w_o, ln_g[0, 1], ln_b[0, 1],
                  alpha, tm_lat)
    xc = _proj_ln(o_c.reshape(batch * n_ctx, d), xc, zeros_ctx, mod_c[:, 1], w_o, ln_g[0, 1],
                  ln_b[0, 1], alpha, tm_ctx)

    wg, wu, wd = bf(ffn_w_gate[0, 1]), bf(ffn_w_up[0, 1]), bf(ffn_w_down[0, 1])
    xl = _ffn(xl, tile_bf, mod_b[:, 2], wg, wu, wd, ln_g[0, 2], ln_b[0, 2], alpha, tf_lat,
              out_tb=(batch, seq))
    xc = _ffn(xc, zeros_ctx_f, mod_c[:, 2], wg, wu, wd, ln_g[0, 2], ln_b[0, 2], alpha, tf_ctx)

    xl = xl.reshape(seq * batch, d)
    xc = xc.reshape(batch, n_ctx, d).transpose(1, 0, 2).reshape(n_ctx * batch, d)
    mod_p = ada[1, :batch].transpose(1, 2, 0, 3)[None]
    mod_c = _bcast_mod(ada[1, batch:batch + 1])

    wg, wu, wd = bf(ffn_w_gate[1, 0]), bf(ffn_w_up[1, 0]), bf(ffn_w_down[1, 0])
    xl = _ffn(xl, zeros_lat_f, mod_p[:, 0], wg, wu, wd, ln_g[1, 0], ln_b[1, 0], alpha, tf_lat)
    xc = _ffn(xc, zeros_ctx_f, mod_c[:, 0], wg, wu, wd, ln_g[1, 0], ln_b[1, 0], alpha, tf_ctx)

    w_in = bf(lru_w_in[0])
    y_l = _lru_in(xl, zeros_lat, mod_p[:, 1], w_in, tm_lat)
    y_c = _lru_in(xc, zeros_ctx, mod_c[:, 1], w_in, tm_ctx)
    d_rnn = w_in.shape[1] // 2
    w_a, w_i = bf(lru_w_a[0]), bf(lru_w_i[0])
    scan = functools.partial(_lru_scan, conv_w=lru_conv_w[0], conv_b=lru_conv_b[0], w_a=w_a,
                             b_a=lru_b_a[0], w_i=w_i, b_i=lru_b_i[0], a_param=lru_a_param[0])
    h_zero = jnp.zeros((SUBLANES, d_rnn), F32)
    _, h0_f = scan(y_c, h0=h_zero, direction=0, reverse=False)
    _, h0_b = scan(y_c, h0=h_zero, direction=1, reverse=True)
    hf, _ = scan(y_l, h0=h0_f, direction=0, reverse=False)
    hg, _ = scan(y_l, h0=h0_b, direction=1, reverse=True, hf=hf)
    xl = _proj_ln(hg, xl, zeros_lat, mod_p[:, 1], bf(lru_w_out[0]), ln_g[1, 1], ln_b[1, 1], alpha,
                  tm_lat)

    wg, wu, wd = bf(ffn_w_gate[1, 1]), bf(ffn_w_up[1, 1]), bf(ffn_w_down[1, 1])
    mod_b1 = _bcast_mod(ada[1, :batch])
    xl = _ffn(xl.reshape(seq, batch * d), tile_bf, mod_b1[:, 2], wg, wu, wd, ln_g[1, 2], ln_b[1, 2],
              alpha, tf_lat, in_tb=(batch, seq))
    return xl.reshape(batch, seq, d)
```

```python
import functools
import math

import jax
import jax.numpy as jnp
from jax import lax
from jax.experimental import pallas as pl
from jax.experimental.pallas import tpu as pltpu

F32 = jnp.float32
BF16 = jnp.bfloat16

SUBLANES = 8
LANES = 128
VMEM_LIMIT_BYTES = 56 * 1024 * 1024
ATTN_SUB_ROWS = 256
ATTN_SUB_TILES = 4
ATTN_BOUND_LIMIT = 80.0
ATTN_BOUND_SLACK = 1.001

GRID_W = 64
DA_HEAD_DIM = 64
DA_V_DIM = 2 * DA_HEAD_DIM
ROPE_BASE = 10000.0
AXIS_ROT_DIM = DA_HEAD_DIM // 2
ROT_HALF = AXIS_ROT_DIM // 2
CONV_PAD_LEFT = 2
LRU_C = 8.0
FFN_RES_WEIGHT = 0.5
LN_EPS = 1e-6
GELU_C0 = math.sqrt(2.0 / math.pi)
GELU_C1 = 0.044715


def _params(semantics):
    return pltpu.CompilerParams(dimension_semantics=semantics, vmem_limit_bytes=VMEM_LIMIT_BYTES)


def _pick(n, prefs):
    for p in prefs:
        if n % p == 0:
            return p
    return n


def _layer_norm(y, g, b):
    mu = jnp.mean(y, axis=-1, keepdims=True)
    yc = y - mu
    var = jnp.mean(yc * yc, axis=-1, keepdims=True)
    return yc * lax.rsqrt(var + LN_EPS) * g + b


def _row_chunks(n_rows, chunk, fn):
    def body(r, carry):
        fn(pl.ds(pl.multiple_of(r * chunk, chunk), chunk))
        return carry
    lax.fori_loop(0, n_rows // chunk, body, 0)


def _modulate_rows(x_ref, mod_ref, u_ref, chunk):
    tm, d = x_ref.shape
    shift = mod_ref[0]
    scale1 = 1.0 + mod_ref[1]

    def fn(rows):
        xx = x_ref[rows, :].reshape(chunk // SUBLANES, SUBLANES, d)
        u_ref[rows, :] = (xx * scale1 + shift).reshape(chunk, d).astype(u_ref.dtype)
    _row_chunks(tm, chunk, fn)


def _residual_ln_rows(x_ref, o_ref, gate, lng, lnb, alpha, chunk):
    tm, d = x_ref.shape

    def fn(rows):
        xx = x_ref[rows, :].reshape(chunk // SUBLANES, SUBLANES, d)
        hh = o_ref[rows, :].reshape(chunk // SUBLANES, SUBLANES, d)
        y = alpha * xx + gate * hh
        o_ref[rows, :] = _layer_norm(y, lng, lnb).reshape(chunk, d)
    _row_chunks(tm, chunk, fn)


def _ada_kernel(c_ref, w_ref, b_ref, o_ref):
    c = c_ref[...]
    s = c * jax.nn.sigmoid(c)
    o_ref[...] = jnp.dot(s.astype(BF16), w_ref[...].astype(BF16),
                         preferred_element_type=F32) + b_ref[...]


def _ada(cc, w_ada, b_ada):
    depth, d, n = w_ada.shape
    rows = cc.shape[0]
    tn = _pick(n, (1024, 512, 256, 128))
    return pl.pallas_call(
        _ada_kernel,
        grid=(depth, n // tn),
        in_specs=[
            pl.BlockSpec((rows, d), lambda l, j: (0, 0)),
            pl.BlockSpec((None, d, tn), lambda l, j: (l, 0, j)),
            pl.BlockSpec((None, 1, tn), lambda l, j: (l, 0, j)),
        ],
        out_specs=pl.BlockSpec((None, rows, tn), lambda l, j: (l, 0, j)),
        out_shape=jax.ShapeDtypeStruct((depth, rows, n), F32),
        compiler_params=_params(("arbitrary", "arbitrary")),
        name="ada",
    )(cc, w_ada, b_ada.reshape(depth, 1, n))


def _ffn_kernel(tmod_ref, x_ref, mod_ref, wg_ref, wu_ref, wd_ref, lng_ref, lnb_ref, o_ref, u_ref,
                *, alpha, chunk):
    f = pl.program_id(1)
    tm, d = x_ref.shape

    @pl.when(f == 0)
    def _init():
        _modulate_rows(x_ref, mod_ref, u_ref, chunk)
        o_ref[...] = jnp.zeros((tm, d), F32)

    u = u_ref[...]
    ga = jnp.dot(u, wg_ref[...], preferred_element_type=F32)
    up = jnp.dot(u, wu_ref[...], preferred_element_type=F32)
    h = (ga * jax.nn.sigmoid(ga) * up).astype(BF16)
    o_ref[...] += jnp.dot(h, wd_ref[...], preferred_element_type=F32)

    @pl.when(f == pl.num_programs(1) - 1)
    def _finish():
        _residual_ln_rows(x_ref, o_ref, FFN_RES_WEIGHT * mod_ref[2], lng_ref[...], lnb_ref[...],
                          alpha, chunk)


def _ffn(x, tile_mod, mod, wg, wu, wd, lng, lnb, alpha, tm):
    rows, d = x.shape
    ff = wg.shape[1]
    tf = _pick(ff, (512, 256, 128))
    chunk = _pick(tm, (128, 64, 32, 16))
    kern = functools.partial(_ffn_kernel, alpha=alpha, chunk=chunk)
    grid_spec = pltpu.PrefetchScalarGridSpec(
        num_scalar_prefetch=1,
        grid=(rows // tm, ff // tf),
        in_specs=[
            pl.BlockSpec((tm, d), lambda i, f, t: (i, 0)),
            pl.BlockSpec((None, 3, SUBLANES, d), lambda i, f, t: (t[i], 0, 0, 0)),
            pl.BlockSpec((d, tf), lambda i, f, t: (0, f)),
            pl.BlockSpec((d, tf), lambda i, f, t: (0, f)),
            pl.BlockSpec((tf, d), lambda i, f, t: (f, 0)),
            pl.BlockSpec((1, d), lambda i, f, t: (0, 0)),
            pl.BlockSpec((1, d), lambda i, f, t: (0, 0)),
        ],
        out_specs=pl.BlockSpec((tm, d), lambda i, f, t: (i, 0)),
        scratch_shapes=[pltpu.VMEM((tm, d), BF16)],
    )
    return pl.pallas_call(
        kern,
        grid_spec=grid_spec,
        out_shape=jax.ShapeDtypeStruct((rows, d), F32),
        compiler_params=_params(("arbitrary", "arbitrary")),
        name="ffn",
    )(tile_mod, x, mod, wg, wu, wd, lng.reshape(1, d), lnb.reshape(1, d))


def _qkv_kernel(tmod_ref, x_ref, mod_ref, w_ref, cs_ref, sa_ref, sb_ref, o_ref, u_ref, *, chunk):
    j = pl.program_id(2)

    @pl.when(j == 0)
    def _init():
        _modulate_rows(x_ref, mod_ref, u_ref, chunk)

    acc = jnp.dot(u_ref[...], w_ref[...], preferred_element_type=F32)
    cs, sa, sb = cs_ref[...], sa_ref[...], sb_ref[...]
    for hh in range(o_ref.shape[0]):
        blk = acc[:, hh * LANES:(hh + 1) * LANES]
        rot = (blk * cs + pltpu.roll(blk, LANES - ROT_HALF, 1) * sa
               + pltpu.roll(blk, ROT_HALF, 1) * sb)
        o_ref[hh] = rot.astype(o_ref.dtype)


def _qkv(x, batch, tile_mod, mod, w, cs, sa, sb):
    rows, d = x.shape
    n = rows // batch
    heads = d // DA_V_DIM
    tm = _pick(n, (1024, 512, 256, 128))
    tn = _pick(d, (512, 256, 128))
    hpc = tn // LANES
    cpw = d // tn
    chunk = _pick(tm, (128, 64, 32, 16))
    nt = n // tm
    grid_spec = pltpu.PrefetchScalarGridSpec(
        num_scalar_prefetch=1,
        grid=(batch, nt, 3 * cpw),
        in_specs=[
            pl.BlockSpec((tm, d), lambda b, t, j, tmr: (b * nt + t, 0)),
            pl.BlockSpec((None, 3, SUBLANES, d), lambda b, t, j, tmr: (tmr[b * nt + t], 0, 0, 0)),
            pl.BlockSpec((d, tn), lambda b, t, j, tmr: (0, j)),
            pl.BlockSpec((None, tm, LANES), lambda b, t, j, tmr: (j // cpw, t, 0)),
            pl.BlockSpec((None, tm, LANES), lambda b, t, j, tmr: (j // cpw, t, 0)),
            pl.BlockSpec((None, tm, LANES), lambda b, t, j, tmr: (j // cpw, t, 0)),
        ],
        out_specs=pl.BlockSpec((None, None, hpc, tm, LANES),
                               lambda b, t, j, tmr: (j // cpw, b, j % cpw, t, 0)),
        scratch_shapes=[pltpu.VMEM((tm, d), BF16)],
    )
    return pl.pallas_call(
        functools.partial(_qkv_kernel, chunk=chunk),
        grid_spec=grid_spec,
        out_shape=jax.ShapeDtypeStruct((3, batch, heads, n, LANES), BF16),
        compiler_params=_params(("arbitrary", "arbitrary", "arbitrary")),
        name="qkv",
    )(tile_mod, x, mod, w, cs, sa, sb)


def _rope_tables(n, rotate):
    scale_q = DA_HEAD_DIM ** -0.5 * math.log2(math.e)
    ones = jnp.ones((n, LANES), F32)
    zeros = jnp.zeros((n, LANES), F32)
    if not rotate:
        return (jnp.stack([scale_q * ones, ones, ones]), jnp.stack([zeros] * 3),
                jnp.stack([zeros] * 3))
    rows = n // GRID_W
    row = jnp.repeat(jnp.arange(rows, dtype=F32), GRID_W)
    col = jnp.tile(jnp.arange(GRID_W, dtype=F32), rows)
    inv_freq = ROPE_BASE ** (-jnp.arange(0, AXIS_ROT_DIM, 2, dtype=F32) / AXIS_ROT_DIM)
    ang_r = row[:, None] * inv_freq[None, :]
    ang_c = col[:, None] * inv_freq[None, :]
    z = jnp.zeros_like(ang_r)
    cos64 = jnp.concatenate([jnp.cos(ang_r), jnp.cos(ang_r), jnp.cos(ang_c), jnp.cos(ang_c)], -1)
    sa64 = jnp.concatenate([-jnp.sin(ang_r), z, -jnp.sin(ang_c), z], -1)
    sb64 = jnp.concatenate([z, jnp.sin(ang_r), z, jnp.sin(ang_c)], -1)
    cs = jnp.tile(cos64, (1, 2))
    sa = jnp.tile(sa64, (1, 2))
    sb = jnp.tile(sb64, (1, 2))
    return (jnp.stack([scale_q * cs, cs, ones]), jnp.stack([scale_q * sa, sa, zeros]),
            jnp.stack([scale_q * sb, sb, zeros]))


def _attn_kernel(*refs, n_parts, lambda_init, n_sub):
    q_ref = refs[0]
    k_refs = refs[1:1 + n_parts]
    v_refs = refs[1 + n_parts:1 + 2 * n_parts]
    lq1_ref, lk1_ref, lq2_ref, lk2_ref, g_ref, o_ref = refs[1 + 2 * n_parts:]
    lam = (jnp.exp(jnp.sum(lq1_ref[...] * lk1_ref[...], axis=-1, keepdims=True))
           - jnp.exp(jnp.sum(lq2_ref[...] * lk2_ref[...], axis=-1, keepdims=True)) + lambda_init)
    tq = q_ref.shape[0] // n_sub
    lane = lax.broadcasted_iota(jnp.int32, (tq, LANES), 1)
    sub_head = (lane < DA_HEAD_DIM, lane >= DA_HEAD_DIM)
    nt_dims = (((1,), (1,)), ((), ()))

    def masked_q(t):
        q = q_ref[t * tq:(t + 1) * tq, :]
        zero = jnp.zeros_like(q)
        return [jnp.where(mask, q, zero) for mask in sub_head]

    def finish(t, o):
        y = o * lax.rsqrt(jnp.mean(o * o, axis=-1, keepdims=True) + LN_EPS) * g_ref[...]
        o_ref[t * tq:(t + 1) * tq, :] = (y * (1.0 - lambda_init)).astype(o_ref.dtype)

    k_abs_max = functools.reduce(jnp.maximum, [
        jnp.max(jnp.abs(k_ref[...]), axis=0, keepdims=True) for k_ref in k_refs]).astype(F32)
    k_first = k_refs[0][0:1, :].astype(F32)
    bounds, gaps = [], []
    for t in range(n_sub):
        qf = q_ref[t * tq:(t + 1) * tq, :].astype(F32)
        reach = jnp.abs(qf) * k_abs_max
        first = qf * k_first
        bounds.append([jnp.sum(jnp.where(mask, reach, 0.0), axis=-1, keepdims=True)
                       * ATTN_BOUND_SLACK for mask in sub_head])
        gaps += [jnp.max(b - jnp.sum(jnp.where(mask, first, 0.0), axis=-1, keepdims=True))
                 for b, mask in zip(bounds[-1], sub_head)]
    worst = functools.reduce(jnp.maximum, gaps)

    def streaming():
        v_aug = [jnp.concatenate([v_ref[...], jnp.ones(v_ref.shape, BF16)], axis=1)
                 for v_ref in v_refs]
        for t in range(n_sub):
            qs = masked_q(t)
            acc = []
            for e in range(2):
                tot = None
                for k_ref, va in zip(k_refs, v_aug):
                    s = lax.dot_general(qs[e], k_ref[...], nt_dims, preferred_element_type=F32)
                    p = jnp.exp2(s - bounds[t][e]).astype(BF16)
                    part = jnp.dot(p, va, preferred_element_type=F32)
                    tot = part if tot is None else tot + part
                acc.append(tot)
            finish(t, acc[0][:, :LANES] * (1.0 / acc[0][:, LANES:])
                   - acc[1][:, :LANES] * (lam / acc[1][:, LANES:]))

    def exact_max():
        for t in range(n_sub):
            qs = masked_q(t)
            scores = [[lax.dot_general(qe, k_ref[...], nt_dims, preferred_element_type=F32)
                       for k_ref in k_refs] for qe in qs]
            weights = []
            for e in range(2):
                m = functools.reduce(jnp.maximum,
                                     [jnp.max(s, axis=-1, keepdims=True) for s in scores[e]])
                ps = [jnp.exp2(s - m) for s in scores[e]]
                l = functools.reduce(jnp.add, [jnp.sum(p, axis=-1, keepdims=True) for p in ps])
                weights.append((ps, 1.0 / l))
            o = None
            for p, v_ref in enumerate(v_refs):
                a = weights[0][0][p] * weights[0][1] - weights[1][0][p] * (lam * weights[1][1])
                part = jnp.dot(a.astype(BF16), v_ref[...], preferred_element_type=F32)
                o = part if o is None else o + part
            finish(t, o)

    bound_is_tight = worst <= ATTN_BOUND_LIMIT
    pl.when(bound_is_tight)(streaming)
    pl.when(jnp.logical_not(bound_is_tight))(exact_max)


def _attention(qkv, kv_parts, lam_params, subln_g, lambda_init):
    _, batch, heads, n, _ = qkv.shape
    tq = _pick(n, (ATTN_SUB_TILES * ATTN_SUB_ROWS, ATTN_SUB_ROWS, 128))
    n_sub = max(tq // ATTN_SUB_ROWS, 1)
    in_specs = [pl.BlockSpec((None, None, None, tq, LANES), lambda b, h, t: (0, b, h, t, 0))]
    in_specs += [pl.BlockSpec((None, None, None, part.shape[3], LANES),
                              lambda b, h, t: (1, b, h, 0, 0)) for part in kv_parts]
    in_specs += [pl.BlockSpec((None, None, None, part.shape[3], LANES),
                              lambda b, h, t: (2, b, h, 0, 0)) for part in kv_parts]
    args = [qkv] + list(kv_parts) + list(kv_parts)
    for p in lam_params:
        in_specs.append(pl.BlockSpec((1, DA_HEAD_DIM), lambda b, h, t: (0, 0)))
        args.append(p.reshape(1, DA_HEAD_DIM))
    in_specs.append(pl.BlockSpec((1, DA_V_DIM), lambda b, h, t: (0, 0)))
    args.append(subln_g.reshape(1, DA_V_DIM))
    return pl.pallas_call(
        functools.partial(_attn_kernel, n_parts=len(kv_parts), lambda_init=lambda_init,
                          n_sub=n_sub),
        grid=(batch, heads, n // tq),
        in_specs=in_specs,
        out_specs=pl.BlockSpec((None, tq, LANES), lambda b, h, t: (b, t, h)),
        out_shape=jax.ShapeDtypeStruct((batch, n, heads * LANES), BF16),
        compiler_params=_params(("arbitrary", "arbitrary", "arbitrary")),
        name="attn",
    )(*args)


def _proj_ln_kernel(tmod_ref, h_ref, x_ref, mod_ref, w_ref, lng_ref, lnb_ref, o_ref, *, alpha, chunk):
    o_ref[...] = jnp.dot(h_ref[...], w_ref[...], preferred_element_type=F32)
    _residual_ln_rows(x_ref, o_ref, mod_ref[2], lng_ref[...], lnb_ref[...], alpha, chunk)


def _proj_ln(h, x, tile_mod, mod, w, lng, lnb, alpha, tm):
    rows, d = x.shape
    k = h.shape[1]
    chunk = _pick(tm, (128, 64, 32, 16))
    grid_spec = pltpu.PrefetchScalarGridSpec(
        num_scalar_prefetch=1,
        grid=(rows // tm,),
        in_specs=[
            pl.BlockSpec((tm, k), lambda i, t: (i, 0)),
            pl.BlockSpec((tm, d), lambda i, t: (i, 0)),
            pl.BlockSpec((None, 3, SUBLANES, d), lambda i, t: (t[i], 0, 0, 0)),
            pl.BlockSpec((k, d), lambda i, t: (0, 0)),
            pl.BlockSpec((1, d), lambda i, t: (0, 0)),
            pl.BlockSpec((1, d), lambda i, t: (0, 0)),
        ],
        out_specs=pl.BlockSpec((tm, d), lambda i, t: (i, 0)),
    )
    return pl.pallas_call(
        functools.partial(_proj_ln_kernel, alpha=alpha, chunk=chunk),
        grid_spec=grid_spec,
        out_shape=jax.ShapeDtypeStruct((rows, d), F32),
        compiler_params=_params(("arbitrary",)),
        name="proj_ln",
    )(tile_mod, h, x, mod, w, lng.reshape(1, d), lnb.reshape(1, d))


def _lru_in_kernel(tmod_ref, x_ref, mod_ref, w_ref, o_ref, u_ref, *, n_gelu, chunk):
    j = pl.program_id(1)

    @pl.when(j == 0)
    def _init():
        _modulate_rows(x_ref, mod_ref, u_ref, chunk)

    acc = jnp.dot(u_ref[...], w_ref[...], preferred_element_type=F32)

    @pl.when(j < n_gelu)
    def _gate_branch():
        cdf = 0.5 * (1.0 + jnp.tanh(GELU_C0 * (acc + GELU_C1 * (acc * acc * acc))))
        o_ref[...] = acc * cdf

    @pl.when(j >= n_gelu)
    def _recurrent_branch():
        o_ref[...] = acc


def _lru_in(x, tile_mod, mod, w, tm):
    rows, d = x.shape
    n = w.shape[1]
    tn = _pick(n // 2, (512, 256, 128))
    chunk = _pick(tm, (128, 64, 32, 16))
    grid_spec = pltpu.PrefetchScalarGridSpec(
        num_scalar_prefetch=1,
        grid=(rows // tm, n // tn),
        in_specs=[
            pl.BlockSpec((tm, d), lambda i, j, t: (i, 0)),
            pl.BlockSpec((None, 3, SUBLANES, d), lambda i, j, t: (t[i], 0, 0, 0)),
            pl.BlockSpec((d, tn), lambda i, j, t: (0, j)),
        ],
        out_specs=pl.BlockSpec((tm, tn), lambda i, j, t: (i, j)),
        scratch_shapes=[pltpu.VMEM((tm, d), BF16)],
    )
    return pl.pallas_call(
        functools.partial(_lru_in_kernel, n_gelu=(n // 2) // tn, chunk=chunk),
        grid_spec=grid_spec,
        out_shape=jax.ShapeDtypeStruct((rows, n), F32),
        compiler_params=_params(("arbitrary", "arbitrary")),
        name="lru_in",
    )(tile_mod, x, mod, w)


def _lru_scan_kernel(*refs, reverse, combine, n_chunks, n_taps):
    if combine:
        (xr_ref, prev_ref, next_ref, cw_ref, cb_ref, wa_ref, ba_ref, wi_ref, bi_ref, ap_ref, h0_ref,
         hf_ref, g_ref, o_ref, hl_ref, a_s, b_s, carry) = refs
    else:
        (xr_ref, prev_ref, next_ref, cw_ref, cb_ref, wa_ref, ba_ref, wi_ref, bi_ref, ap_ref, h0_ref,
         o_ref, hl_ref, a_s, b_s, carry) = refs
    c = pl.program_id(1)
    c_eff = n_chunks - 1 - c if reverse else c
    rows, width = xr_ref.shape
    steps = rows // SUBLANES

    @pl.when(c == 0)
    def _init():
        carry[...] = h0_ref[...]

    halo_l = jnp.where(c_eff > 0, prev_ref[...], 0.0)
    halo_r = jnp.where(c_eff < n_chunks - 1, next_ref[...], 0.0)
    ext = jnp.concatenate([halo_l, xr_ref[...], halo_r], axis=0)
    xc = cb_ref[...]
    for j in range(n_taps):
        xc = xc + ext[j * SUBLANES:j * SUBLANES + rows] * cw_ref[j:j + 1, :]
    xb = xc.astype(BF16)
    r = jax.nn.sigmoid(jnp.dot(xb, wa_ref[...], preferred_element_type=F32) + ba_ref[...])
    i = jax.nn.sigmoid(jnp.dot(xb, wi_ref[...], preferred_element_type=F32) + bi_ref[...])
    z = -ap_ref[...]
    softplus = jnp.maximum(z, 0.0) + jnp.log(1.0 + jnp.exp(-jnp.abs(z)))
    log_a = (-LRU_C) * r * softplus
    a = jnp.exp(log_a)
    a_s[...] = a
    b_s[...] = jnp.sqrt(1.0 - a * a) * i * xc

    def step(t, h):
        tt = steps - 1 - t if reverse else t
        sl = pl.ds(pl.multiple_of(tt * SUBLANES, SUBLANES), SUBLANES)
        h = a_s[sl, :] * h + b_s[sl, :]
        b_s[sl, :] = h
        return h
    h_last = lax.fori_loop(0, steps, step, carry[...], unroll=8)
    carry[...] = h_last
    hl_ref[...] = h_last
    if combine:
        o_ref[...] = ((hf_ref[...] + b_s[...]) * g_ref[...]).astype(o_ref.dtype)
    else:
        o_ref[...] = b_s[...]


def _lru_scan(y, conv_w, conv_b, w_a, b_a, w_i, b_i, a_param, h0, direction, reverse, hf=None):
    rows = y.shape[0]
    d_rnn = y.shape[1] // 2
    n_taps = conv_w.shape[0]
    width = w_a.shape[-1]
    n_blk = d_rnn // width
    tokens = rows // SUBLANES
    tc = _pick(tokens, (256, 128, 64, 32, 16, 8))
    rc = tc * SUBLANES
    n_chunks = tokens // tc
    halo_l = CONV_PAD_LEFT * SUBLANES
    halo_r = (n_taps - 1 - CONV_PAD_LEFT) * SUBLANES
    combine = hf is not None
    last_l = rows // halo_l - 1
    last_r = rows // halo_r - 1

    def ce(c):
        return n_chunks - 1 - c if reverse else c

    vec = lambda: pl.BlockSpec((None, 1, width), lambda h, c: (direction, 0, h))
    in_specs = [
        pl.BlockSpec((rc, width), lambda h, c: (ce(c), n_blk + h)),
        pl.BlockSpec((halo_l, width),
                     lambda h, c: (jnp.maximum(ce(c) * (rc // halo_l) - 1, 0), n_blk + h)),
        pl.BlockSpec((halo_r, width),
                     lambda h, c: (jnp.minimum((ce(c) + 1) * (rc // halo_r), last_r), n_blk + h)),
        pl.BlockSpec((n_taps, width), lambda h, c: (0, h)),
        pl.BlockSpec((1, width), lambda h, c: (0, h)),
        pl.BlockSpec((None, None, width, width), lambda h, c: (direction, h, 0, 0)),
        vec(),
        pl.BlockSpec((None, None, width, width), lambda h, c: (direction, h, 0, 0)),
        vec(),
        vec(),
        pl.BlockSpec((SUBLANES, width), lambda h, c: (0, h)),
    ]
    args = [y, y, y, conv_w, conv_b.reshape(1, d_rnn), w_a, b_a.reshape(2, 1, d_rnn), w_i,
            b_i.reshape(2, 1, d_rnn), a_param.reshape(2, 1, d_rnn), h0]
    if combine:
        in_specs += [pl.BlockSpec((rc, width), lambda h, c: (ce(c), h)),
                     pl.BlockSpec((rc, width), lambda h, c: (ce(c), h))]
        args += [hf, y]
    del last_l
    out_dtype = BF16 if combine else F32
    return pl.pallas_call(
        functools.partial(_lru_scan_kernel, reverse=reverse, combine=combine, n_chunks=n_chunks,
                          n_taps=n_taps),
        grid=(n_blk, n_chunks),
        in_specs=in_specs,
        out_specs=[pl.BlockSpec((rc, width), lambda h, c: (ce(c), h)),
                   pl.BlockSpec((SUBLANES, width), lambda h, c: (0, h))],
        out_shape=[jax.ShapeDtypeStruct((rows, d_rnn), out_dtype),
                   jax.ShapeDtypeStruct((SUBLANES, d_rnn), F32)],
        scratch_shapes=[pltpu.VMEM((rc, width), F32), pltpu.VMEM((rc, width), F32),
                        pltpu.VMEM((SUBLANES, width), F32)],
        compiler_params=_params(("arbitrary", "arbitrary")),
        name="lru_scan",
    )(*args)


def _bcast_mod(m):
    return jnp.broadcast_to(m[..., None, :], m.shape[:-1] + (SUBLANES, m.shape[-1]))


def kernel(x, c, ctx, c_ctx, w_ada, b_ada, ln_g, ln_b, ffn_w_gate, ffn_w_up, ffn_w_down, attn_w_qkv, attn_w_o, attn_lambda_q1, attn_lambda_k1, attn_lambda_q2, attn_lambda_k2, attn_subln_g, lru_w_in, lru_conv_w, lru_conv_b, lru_w_a, lru_b_a, lru_w_i, lru_b_i, lru_a_param, lru_w_out):
    batch, seq, d = x.shape
    n_ctx = ctx.shape[1]
    depth = w_ada.shape[0]
    assert depth == 2 and batch == SUBLANES
    alpha = (2.0 * depth) ** 0.25

    ada_rows = 2 * SUBLANES
    cc = jnp.zeros((ada_rows, d), F32).at[:batch].set(c).at[batch].set(c_ctx)
    ada = _ada(cc, w_ada, b_ada).reshape(depth, ada_rows, 3, 3, d)

    tm_lat = _pick(seq, (512, 256, 128))
    tm_ctx = _pick(batch * n_ctx, (512, 256, 128))
    tf_lat = _pick(seq, (1024, 512, 256, 128))
    tf_ctx = _pick(batch * n_ctx, (1024, 512, 256, 128))
    zeros_lat = jnp.zeros((batch * seq // tm_lat,), jnp.int32)
    zeros_ctx = jnp.zeros((batch * n_ctx // tm_ctx,), jnp.int32)
    zeros_lat_f = jnp.zeros((batch * seq // tf_lat,), jnp.int32)
    zeros_ctx_f = jnp.zeros((batch * n_ctx // tf_ctx,), jnp.int32)
    bf = lambda w: w.astype(BF16)

    def batch_of_tile(tile):
        return jnp.arange(batch * seq // tile, dtype=jnp.int32) // (seq // tile)

    mod_b = _bcast_mod(ada[0, :batch])
    mod_c = _bcast_mod(ada[0, batch:batch + 1])
    tile_b = batch_of_tile(tm_lat)
    tile_bf = batch_of_tile(tf_lat)
    xl = x.reshape(batch * seq, d)
    xc = ctx.reshape(batch * n_ctx, d)

    wg, wu, wd = bf(ffn_w_gate[0, 0]), bf(ffn_w_up[0, 0]), bf(ffn_w_down[0, 0])
    xl = _ffn(xl, tile_bf, mod_b[:, 0], wg, wu, wd, ln_g[0, 0], ln_b[0, 0], alpha, tf_lat)
    xc = _ffn(xc, zeros_ctx_f, mod_c[:, 0], wg, wu, wd, ln_g[0, 0], ln_b[0, 0], alpha, tf_ctx)

    lambda_init = 0.8 - 0.6 * math.exp(-0.3 * 0)
    w_qkv = bf(attn_w_qkv[0])
    tq_lat = _pick(seq, (1024, 512, 256, 128))
    tile_q = jnp.arange(batch * seq // tq_lat, dtype=jnp.int32) // (seq // tq_lat)
    qkv_l = _qkv(xl, batch, tile_q, mod_b[:, 1], w_qkv, *_rope_tables(seq, True))
    tq_ctx = _pick(n_ctx, (1024, 512, 256, 128))
    qkv_c = _qkv(xc, batch, jnp.zeros((batch * n_ctx // tq_ctx,), jnp.int32), mod_c[:, 1], w_qkv,
                 *_rope_tables(n_ctx, False))
    lam_params = (attn_lambda_q1[0], attn_lambda_k1[0], attn_lambda_q2[0], attn_lambda_k2[0])
    o_l = _attention(qkv_l, (qkv_c, qkv_l), lam_params, attn_subln_g[0], lambda_init)
    o_c = _attention(qkv_c, (qkv_c,), lam_params, attn_subln_g[0], lambda_init)
    w_o = bf(attn_w_o[0])
    xl = _proj_ln(o_l.reshape(batch * seq, d), xl, tile_b, mod_b[:, 1], w_o, ln_g[0, 1], ln_b[0, 1],
                  alpha, tm_lat)
    xc = _proj_ln(o_c.reshape(batch * n_ctx, d), xc, zeros_ctx, mod_c[:, 1], w_o, ln_g[0, 1],
                  ln_b[0, 1], alpha, tm_ctx)

    wg, wu, wd = bf(ffn_w_gate[0, 1]), bf(ffn_w_up[0, 1]), bf(ffn_w_down[0, 1])
    xl = _ffn(xl, tile_bf, mod_b[:, 2], wg, wu, wd, ln_g[0, 2], ln_b[0, 2], alpha, tf_lat)
    xc = _ffn(xc, zeros_ctx_f, mod_c[:, 2], wg, wu, wd, ln_g[0, 2], ln_b[0, 2], alpha, tf_ctx)

    xl = xl.reshape(batch, seq, d).transpose(1, 0, 2).reshape(seq * batch, d)
    xc = xc.reshape(batch, n_ctx, d).transpose(1, 0, 2).reshape(n_ctx * batch, d)
    mod_p = ada[1, :batch].transpose(1, 2, 0, 3)[None]
    mod_c = _bcast_mod(ada[1, batch:batch + 1])

    wg, wu, wd = bf(ffn_w_gate[1, 0]), bf(ffn_w_up[1, 0]), bf(ffn_w_down[1, 0])
    xl = _ffn(xl, zeros_lat_f, mod_p[:, 0], wg, wu, wd, ln_g[1, 0], ln_b[1, 0], alpha, tf_lat)
    xc = _ffn(xc, zeros_ctx_f, mod_c[:, 0], wg, wu, wd, ln_g[1, 0], ln_b[1, 0], alpha, tf_ctx)

    w_in = bf(lru_w_in[0])
    y_l = _lru_in(xl, zeros_lat_f, mod_p[:, 1], w_in, tf_lat)
    y_c = _lru_in(xc, zeros_ctx_f, mod_c[:, 1], w_in, tf_ctx)
    d_rnn = w_in.shape[1] // 2
    w_a, w_i = bf(lru_w_a[0]), bf(lru_w_i[0])
    scan = functools.partial(_lru_scan, conv_w=lru_conv_w[0], conv_b=lru_conv_b[0], w_a=w_a,
                             b_a=lru_b_a[0], w_i=w_i, b_i=lru_b_i[0], a_param=lru_a_param[0])
    h_zero = jnp.zeros((SUBLANES, d_rnn), F32)
    _, h0_f = scan(y_c, h0=h_zero, direction=0, reverse=False)
    _, h0_b = scan(y_c, h0=h_zero, direction=1, reverse=True)
    hf, _ = scan(y_l, h0=h0_f, direction=0, reverse=False)
    hg, _ = scan(y_l, h0=h0_b, direction=1, reverse=True, hf=hf)
    xl = _proj_ln(hg, xl, zeros_lat, mod_p[:, 1], bf(lru_w_out[0]), ln_g[1, 1], ln_b[1, 1], alpha,
                  tm_lat)

    wg, wu, wd = bf(ffn_w_gate[1, 1]), bf(ffn_w_up[1, 1]), bf(ffn_w_down[1, 1])
    xl = _ffn(xl, zeros_lat_f, mod_p[:, 2], wg, wu, wd, ln_g[1, 2], ln_b[1, 2], alpha, tf_lat)
    return xl.reshape(seq, batch, d).transpose(1, 0, 2)
```

```python
import functools
import math

import jax
import jax.numpy as jnp
from jax import lax
from jax.experimental import pallas as pl
from jax.experimental.pallas import tpu as pltpu

F32 = jnp.float32
BF16 = jnp.bfloat16

SUBLANES = 8
LANES = 128
VMEM_LIMIT_BYTES = 56 * 1024 * 1024
ATTN_SUB_ROWS = 512
ATTN_SUB_TILES = 2
ATTN_MIN_ROW_SUM = 2.0 ** -80
ATTN_BOUND_SLACK = 1.001
ATTN_ONES_ROWS = 16

GRID_W = 64
DA_HEAD_DIM = 64
DA_V_DIM = 2 * DA_HEAD_DIM
ROPE_BASE = 10000.0
AXIS_ROT_DIM = DA_HEAD_DIM // 2
ROT_HALF = AXIS_ROT_DIM // 2
CONV_PAD_LEFT = 2
LRU_C = 8.0
FFN_RES_WEIGHT = 0.5
LN_EPS = 1e-6
GELU_C0 = math.sqrt(2.0 / math.pi)
GELU_C1 = 0.044715


def _params(semantics):
    return pltpu.CompilerParams(dimension_semantics=semantics, vmem_limit_bytes=VMEM_LIMIT_BYTES)


def _pick(n, prefs):
    for p in prefs:
        if n % p == 0:
            return p
    return n


def _layer_norm(y, g, b):
    mu = jnp.mean(y, axis=-1, keepdims=True)
    yc = y - mu
    var = jnp.mean(yc * yc, axis=-1, keepdims=True)
    return yc * lax.rsqrt(var + LN_EPS) * g + b


def _row_chunks(n_rows, chunk, fn):
    def body(r, carry):
        fn(pl.ds(pl.multiple_of(r * chunk, chunk), chunk))
        return carry
    lax.fori_loop(0, n_rows // chunk, body, 0)


def _modulate_rows(x_ref, mod_ref, u_ref, chunk):
    tm, d = x_ref.shape
    shift = mod_ref[0]
    scale1 = 1.0 + mod_ref[1]

    def fn(rows):
        xx = x_ref[rows, :].reshape(chunk // SUBLANES, SUBLANES, d)
        u_ref[rows, :] = (xx * scale1 + shift).reshape(chunk, d).astype(u_ref.dtype)
    _row_chunks(tm, chunk, fn)


def _residual_ln_rows(x_ref, o_ref, gate, lng, lnb, alpha, chunk):
    tm, d = x_ref.shape

    def fn(rows):
        xx = x_ref[rows, :].reshape(chunk // SUBLANES, SUBLANES, d)
        hh = o_ref[rows, :].reshape(chunk // SUBLANES, SUBLANES, d)
        y = alpha * xx + gate * hh
        o_ref[rows, :] = _layer_norm(y, lng, lnb).reshape(chunk, d)
    _row_chunks(tm, chunk, fn)


def _ada_kernel(c_ref, w_ref, b_ref, o_ref):
    c = c_ref[...]
    s = c * jax.nn.sigmoid(c)
    o_ref[...] = jnp.dot(s.astype(BF16), w_ref[...].astype(BF16),
                         preferred_element_type=F32) + b_ref[...]


def _ada(cc, w_ada, b_ada):
    depth, d, n = w_ada.shape
    rows = cc.shape[0]
    tn = _pick(n, (1024, 512, 256, 128))
    return pl.pallas_call(
        _ada_kernel,
        grid=(depth, n // tn),
        in_specs=[
            pl.BlockSpec((rows, d), lambda l, j: (0, 0)),
            pl.BlockSpec((None, d, tn), lambda l, j: (l, 0, j)),
            pl.BlockSpec((None, 1, tn), lambda l, j: (l, 0, j)),
        ],
        out_specs=pl.BlockSpec((None, rows, tn), lambda l, j: (l, 0, j)),
        out_shape=jax.ShapeDtypeStruct((depth, rows, n), F32),
        compiler_params=_params(("arbitrary", "arbitrary")),
        name="ada",
    )(cc, w_ada, b_ada.reshape(depth, 1, n))


def _ffn_kernel(tmod_ref, x_ref, mod_ref, wg_ref, wu_ref, wd_ref, lng_ref, lnb_ref, o_ref, u_ref,
                *, alpha, chunk):
    f = pl.program_id(1)
    tm, d = x_ref.shape

    @pl.when(f == 0)
    def _init():
        _modulate_rows(x_ref, mod_ref, u_ref, chunk)
        o_ref[...] = jnp.zeros((tm, d), F32)

    u = u_ref[...]
    ga = jnp.dot(u, wg_ref[...], preferred_element_type=F32)
    up = jnp.dot(u, wu_ref[...], preferred_element_type=F32)
    h = (ga * jax.nn.sigmoid(ga) * up).astype(BF16)
    o_ref[...] += jnp.dot(h, wd_ref[...], preferred_element_type=F32)

    @pl.when(f == pl.num_programs(1) - 1)
    def _finish():
        _residual_ln_rows(x_ref, o_ref, FFN_RES_WEIGHT * mod_ref[2], lng_ref[...], lnb_ref[...],
                          alpha, chunk)


def _ffn(x, tile_mod, mod, weights, which, lng, lnb, alpha, tm):
    rows, d = x.shape
    wg, wu, wd = weights
    li, ki = which
    ff = wg.shape[-1]
    tf = _pick(ff, (512, 256, 128))
    chunk = _pick(tm, (128, 64, 32, 16))
    kern = functools.partial(_ffn_kernel, alpha=alpha, chunk=chunk)
    grid_spec = pltpu.PrefetchScalarGridSpec(
        num_scalar_prefetch=1,
        grid=(rows // tm, ff // tf),
        in_specs=[
            pl.BlockSpec((tm, d), lambda i, f, t: (i, 0)),
            pl.BlockSpec((None, 3, SUBLANES, d), lambda i, f, t: (t[i], 0, 0, 0)),
            pl.BlockSpec((None, None, d, tf), lambda i, f, t: (li, ki, 0, f)),
            pl.BlockSpec((None, None, d, tf), lambda i, f, t: (li, ki, 0, f)),
            pl.BlockSpec((None, None, tf, d), lambda i, f, t: (li, ki, f, 0)),
            pl.BlockSpec((1, d), lambda i, f, t: (0, 0)),
            pl.BlockSpec((1, d), lambda i, f, t: (0, 0)),
        ],
        out_specs=pl.BlockSpec((tm, d), lambda i, f, t: (i, 0)),
        scratch_shapes=[pltpu.VMEM((tm, d), BF16)],
    )
    return pl.pallas_call(
        kern,
        grid_spec=grid_spec,
        out_shape=jax.ShapeDtypeStruct((rows, d), F32),
        compiler_params=_params(("arbitrary", "arbitrary")),
        name="ffn",
    )(tile_mod, x, mod, wg, wu, wd, lng.reshape(1, d), lnb.reshape(1, d))


def _qkv_kernel(tmod_ref, x_ref, mod_ref, w_ref, cs_ref, sa_ref, sb_ref, o_ref, u_ref, *, chunk,
                n_rot):
    j = pl.program_id(2)

    @pl.when(j == 0)
    def _init():
        _modulate_rows(x_ref, mod_ref, u_ref, chunk)

    acc = jnp.dot(u_ref[...], w_ref[...], preferred_element_type=F32)
    heads_per_chunk = o_ref.shape[0]

    @pl.when(j < n_rot)
    def _rotate():
        cs, sa, sb = cs_ref[...], sa_ref[...], sb_ref[...]
        for hh in range(heads_per_chunk):
            blk = acc[:, hh * LANES:(hh + 1) * LANES]
            rot = (blk * cs + pltpu.roll(blk, LANES - ROT_HALF, 1) * sa
                   + pltpu.roll(blk, ROT_HALF, 1) * sb)
            o_ref[hh] = rot.astype(o_ref.dtype)

    @pl.when(j >= n_rot)
    def _plain():
        for hh in range(heads_per_chunk):
            o_ref[hh] = acc[:, hh * LANES:(hh + 1) * LANES].astype(o_ref.dtype)


def _qkv(x, batch, tile_mod, mod, w, cs, sa, sb):
    rows, d = x.shape
    n = rows // batch
    heads = d // DA_V_DIM
    tm = _pick(n, (1024, 512, 256, 128))
    tn = _pick(d, (512, 256, 128))
    hpc = tn // LANES
    cpw = d // tn
    chunk = _pick(tm, (128, 64, 32, 16))
    nt = n // tm
    grid_spec = pltpu.PrefetchScalarGridSpec(
        num_scalar_prefetch=1,
        grid=(batch, nt, 3 * cpw),
        in_specs=[
            pl.BlockSpec((tm, d), lambda b, t, j, tmr: (b * nt + t, 0)),
            pl.BlockSpec((None, 3, SUBLANES, d), lambda b, t, j, tmr: (tmr[b * nt + t], 0, 0, 0)),
            pl.BlockSpec((d, tn), lambda b, t, j, tmr: (0, j)),
            pl.BlockSpec((None, tm, LANES), lambda b, t, j, tmr: (jnp.minimum(j // cpw, 1), t, 0)),
            pl.BlockSpec((None, tm, LANES), lambda b, t, j, tmr: (jnp.minimum(j // cpw, 1), t, 0)),
            pl.BlockSpec((None, tm, LANES), lambda b, t, j, tmr: (jnp.minimum(j // cpw, 1), t, 0)),
        ],
        out_specs=pl.BlockSpec((None, None, hpc, tm, LANES),
                               lambda b, t, j, tmr: (j // cpw, b, j % cpw, t, 0)),
        scratch_shapes=[pltpu.VMEM((tm, d), BF16)],
    )
    return pl.pallas_call(
        functools.partial(_qkv_kernel, chunk=chunk, n_rot=2 * cpw),
        grid_spec=grid_spec,
        out_shape=jax.ShapeDtypeStruct((3, batch, heads, n, LANES), BF16),
        compiler_params=_params(("arbitrary", "arbitrary", "arbitrary")),
        name="qkv",
    )(tile_mod, x, mod, w, cs, sa, sb)


def _rope_tables(n, rotate):
    scale_q = DA_HEAD_DIM ** -0.5 * math.log2(math.e)
    ones = jnp.ones((n, LANES), F32)
    zeros = jnp.zeros((n, LANES), F32)
    if not rotate:
        return (jnp.stack([scale_q * ones, ones]), jnp.stack([zeros] * 2), jnp.stack([zeros] * 2))
    rows = n // GRID_W
    row = jnp.repeat(jnp.arange(rows, dtype=F32), GRID_W)
    col = jnp.tile(jnp.arange(GRID_W, dtype=F32), rows)
    inv_freq = ROPE_BASE ** (-jnp.arange(0, AXIS_ROT_DIM, 2, dtype=F32) / AXIS_ROT_DIM)
    ang_r = row[:, None] * inv_freq[None, :]
    ang_c = col[:, None] * inv_freq[None, :]
    z = jnp.zeros_like(ang_r)
    cos64 = jnp.concatenate([jnp.cos(ang_r), jnp.cos(ang_r), jnp.cos(ang_c), jnp.cos(ang_c)], -1)
    sa64 = jnp.concatenate([-jnp.sin(ang_r), z, -jnp.sin(ang_c), z], -1)
    sb64 = jnp.concatenate([z, jnp.sin(ang_r), z, jnp.sin(ang_c)], -1)
    cs = jnp.tile(cos64, (1, 2))
    sa = jnp.tile(sa64, (1, 2))
    sb = jnp.tile(sb64, (1, 2))
    return (jnp.stack([scale_q * cs, cs]), jnp.stack([scale_q * sa, sa]),
            jnp.stack([scale_q * sb, sb]))


def _attn_kernel(*refs, n_parts, lambda_init, n_sub):
    q_ref = refs[0]
    k_refs = refs[1:1 + n_parts]
    v_refs = refs[1 + n_parts:1 + 2 * n_parts]
    lq1_ref, lk1_ref, lq2_ref, lk2_ref, g_ref, o_ref = refs[1 + 2 * n_parts:7 + 2 * n_parts]
    vt_refs = refs[7 + 2 * n_parts:7 + 3 * n_parts]
    kmax_ref = refs[7 + 3 * n_parts]
    lam = (jnp.exp(jnp.sum(lq1_ref[...] * lk1_ref[...], axis=-1, keepdims=True))
           - jnp.exp(jnp.sum(lq2_ref[...] * lk2_ref[...], axis=-1, keepdims=True)) + lambda_init)
    tq = q_ref.shape[0] // n_sub
    lane = lax.broadcasted_iota(jnp.int32, (tq, LANES), 1)
    sub_head = (lane < DA_HEAD_DIM, lane >= DA_HEAD_DIM)
    nt_dims = (((1,), (1,)), ((), ()))

    def masked_q(t):
        q = q_ref[t * tq:(t + 1) * tq, :]
        zero = jnp.zeros_like(q)
        return [jnp.where(mask, q, zero) for mask in sub_head]

    def finish(t, o):
        y = o * lax.rsqrt(jnp.mean(o * o, axis=-1, keepdims=True) + LN_EPS) * g_ref[...]
        o_ref[t * tq:(t + 1) * tq, :] = (y * (1.0 - lambda_init)).astype(o_ref.dtype)

    @pl.when(pl.program_id(2) == 0)
    def _per_head_setup():
        for v_ref, vt_ref in zip(v_refs, vt_refs):
            vt_ref[0:LANES, :] = jnp.transpose(v_ref[...].astype(F32)).astype(BF16)
            vt_ref[LANES:, :] = jnp.ones((vt_ref.shape[0] - LANES, vt_ref.shape[1]), BF16)
        k_abs_max = functools.reduce(jnp.maximum, [
            jnp.max(jnp.abs(k_ref[...].astype(F32)), axis=0, keepdims=True) for k_ref in k_refs])
        kmax_ref[...] = jnp.broadcast_to(k_abs_max, kmax_ref.shape)

    k_abs_max = kmax_ref[0:1, :]
    smallest_sum = None
    for t in range(n_sub):
        qs = masked_q(t)
        reach = jnp.abs(q_ref[t * tq:(t + 1) * tq, :].astype(F32)) * k_abs_max
        norm = []
        for e in range(2):
            bound = jnp.sum(jnp.where(sub_head[e], reach, 0.0), axis=-1,
                            keepdims=True) * ATTN_BOUND_SLACK
            shift = jnp.transpose(jnp.broadcast_to(bound, (tq, LANES)))[0:1, :]
            tot = None
            for k_ref, vt_ref in zip(k_refs, vt_refs):
                s_t = lax.dot_general(k_ref[...], qs[e], nt_dims, preferred_element_type=F32)
                p_t = jnp.exp2(s_t - shift).astype(BF16)
                part = jnp.dot(vt_ref[...], p_t, preferred_element_type=F32)
                tot = part if tot is None else tot + part
            norm.append(tot)
            low = jnp.min(tot[LANES:LANES + 1])
            smallest_sum = low if smallest_sum is None else jnp.minimum(smallest_sum, low)
        o_t = (norm[0][:LANES] * (1.0 / norm[0][LANES:LANES + 1])
               - norm[1][:LANES] * (lam / norm[1][LANES:LANES + 1]))
        finish(t, jnp.transpose(o_t))

    def exact_max():
        for t in range(n_sub):
            qs = masked_q(t)
            scores = [[lax.dot_general(qe, k_ref[...], nt_dims, preferred_element_type=F32)
                       for k_ref in k_refs] for qe in qs]
            weights = []
            for e in range(2):
                m = functools.reduce(jnp.maximum,
                                     [jnp.max(s, axis=-1, keepdims=True) for s in scores[e]])
                ps = [jnp.exp2(s - m) for s in scores[e]]
                l = functools.reduce(jnp.add, [jnp.sum(p, axis=-1, keepdims=True) for p in ps])
                weights.append((ps, 1.0 / l))
            o = None
            for p, v_ref in enumerate(v_refs):
                a = weights[0][0][p] * weights[0][1] - weights[1][0][p] * (lam * weights[1][1])
                part = jnp.dot(a.astype(BF16), v_ref[...], preferred_element_type=F32)
                o = part if o is None else o + part
            finish(t, o)

    pl.when(jnp.logical_not(smallest_sum >= ATTN_MIN_ROW_SUM))(exact_max)


def _attention(qkv, kv_parts, lam_params, subln_g, lambda_init):
    _, batch, heads, n, _ = qkv.shape
    tq = _pick(n, (ATTN_SUB_TILES * ATTN_SUB_ROWS, ATTN_SUB_ROWS, 128))
    n_sub = max(tq // ATTN_SUB_ROWS, 1)
    in_specs = [pl.BlockSpec((None, None, None, tq, LANES), lambda b, h, t: (0, b, h, t, 0))]
    in_specs += [pl.BlockSpec((None, None, None, part.shape[3], LANES),
                              lambda b, h, t: (1, b, h, 0, 0)) for part in kv_parts]
    in_specs += [pl.BlockSpec((None, None, None, part.shape[3], LANES),
                              lambda b, h, t: (2, b, h, 0, 0)) for part in kv_parts]
    args = [qkv] + list(kv_parts) + list(kv_parts)
    for p in lam_params:
        in_specs.append(pl.BlockSpec((1, DA_HEAD_DIM), lambda b, h, t: (0, 0)))
        args.append(p.reshape(1, DA_HEAD_DIM))
    in_specs.append(pl.BlockSpec((1, DA_V_DIM), lambda b, h, t: (0, 0)))
    args.append(subln_g.reshape(1, DA_V_DIM))
    return pl.pallas_call(
        functools.partial(_attn_kernel, n_parts=len(kv_parts), lambda_init=lambda_init,
                          n_sub=n_sub),
        grid=(batch, heads, n // tq),
        in_specs=in_specs,
        out_specs=pl.BlockSpec((None, tq, LANES), lambda b, h, t: (b, t, h)),
        out_shape=jax.ShapeDtypeStruct((batch, n, heads * LANES), BF16),
        scratch_shapes=[pltpu.VMEM((LANES + ATTN_ONES_ROWS, part.shape[3]), BF16)
                        for part in kv_parts] + [pltpu.VMEM((SUBLANES, LANES), F32)],
        compiler_params=_params(("arbitrary", "arbitrary", "arbitrary")),
        name="attn",
    )(*args)


def _proj_ln_kernel(tmod_ref, h_ref, x_ref, mod_ref, w_ref, lng_ref, lnb_ref, o_ref, *, alpha, chunk):
    o_ref[...] = jnp.dot(h_ref[...], w_ref[...], preferred_element_type=F32)
    _residual_ln_rows(x_ref, o_ref, mod_ref[2], lng_ref[...], lnb_ref[...], alpha, chunk)


def _proj_ln(h, x, tile_mod, mod, w, lng, lnb, alpha, tm):
    rows, d = x.shape
    k = h.shape[1]
    chunk = _pick(tm, (128, 64, 32, 16))
    grid_spec = pltpu.PrefetchScalarGridSpec(
        num_scalar_prefetch=1,
        grid=(rows // tm,),
        in_specs=[
            pl.BlockSpec((tm, k), lambda i, t: (i, 0)),
            pl.BlockSpec((tm, d), lambda i, t: (i, 0)),
            pl.BlockSpec((None, 3, SUBLANES, d), lambda i, t: (t[i], 0, 0, 0)),
            pl.BlockSpec((k, d), lambda i, t: (0, 0)),
            pl.BlockSpec((1, d), lambda i, t: (0, 0)),
            pl.BlockSpec((1, d), lambda i, t: (0, 0)),
        ],
        out_specs=pl.BlockSpec((tm, d), lambda i, t: (i, 0)),
    )
    return pl.pallas_call(
        functools.partial(_proj_ln_kernel, alpha=alpha, chunk=chunk),
        grid_spec=grid_spec,
        out_shape=jax.ShapeDtypeStruct((rows, d), F32),
        compiler_params=_params(("arbitrary",)),
        name="proj_ln",
    )(tile_mod, h, x, mod, w, lng.reshape(1, d), lnb.reshape(1, d))


def _lru_in_kernel(tmod_ref, x_ref, mod_ref, w_ref, o_ref, u_ref, *, n_gelu, chunk):
    j = pl.program_id(1)

    @pl.when(j == 0)
    def _init():
        _modulate_rows(x_ref, mod_ref, u_ref, chunk)

    acc = jnp.dot(u_ref[...], w_ref[...], preferred_element_type=F32)

    @pl.when(j < n_gelu)
    def _gate_branch():
        cdf = 0.5 * (1.0 + jnp.tanh(GELU_C0 * (acc + GELU_C1 * (acc * acc * acc))))
        o_ref[...] = acc * cdf

    @pl.when(j >= n_gelu)
    def _recurrent_branch():
        o_ref[...] = acc


def _lru_in(x, tile_mod, mod, w, tm):
    rows, d = x.shape
    n = w.shape[1]
    tn = _pick(n // 2, (512, 256, 128))
    chunk = _pick(tm, (128, 64, 32, 16))
    grid_spec = pltpu.PrefetchScalarGridSpec(
        num_scalar_prefetch=1,
        grid=(rows // tm, n // tn),
        in_specs=[
            pl.BlockSpec((tm, d), lambda i, j, t: (i, 0)),
            pl.BlockSpec((None, 3, SUBLANES, d), lambda i, j, t: (t[i], 0, 0, 0)),
            pl.BlockSpec((d, tn), lambda i, j, t: (0, j)),
        ],
        out_specs=pl.BlockSpec((tm, tn), lambda i, j, t: (i, j)),
        scratch_shapes=[pltpu.VMEM((tm, d), BF16)],
    )
    return pl.pallas_call(
        functools.partial(_lru_in_kernel, n_gelu=(n // 2) // tn, chunk=chunk),
        grid_spec=grid_spec,
        out_shape=jax.ShapeDtypeStruct((rows, n), F32),
        compiler_params=_params(("arbitrary", "arbitrary")),
        name="lru_in",
    )(tile_mod, x, mod, w)


def _lru_scan_kernel(*refs, reverse, combine, n_chunks, n_taps):
    if combine:
        (xr_ref, prev_ref, next_ref, cw_ref, cb_ref, wa_ref, ba_ref, wi_ref, bi_ref, ap_ref, h0_ref,
         hf_ref, g_ref, o_ref, hl_ref, a_s, b_s, carry) = refs
    else:
        (xr_ref, prev_ref, next_ref, cw_ref, cb_ref, wa_ref, ba_ref, wi_ref, bi_ref, ap_ref, h0_ref,
         o_ref, hl_ref, a_s, b_s, carry) = refs
    c = pl.program_id(1)
    c_eff = n_chunks - 1 - c if reverse else c
    rows, width = xr_ref.shape
    steps = rows // SUBLANES

    @pl.when(c == 0)
    def _init():
        carry[...] = h0_ref[...]

    halo_l = jnp.where(c_eff > 0, prev_ref[...], 0.0)
    halo_r = jnp.where(c_eff < n_chunks - 1, next_ref[...], 0.0)
    ext = jnp.concatenate([halo_l, xr_ref[...], halo_r], axis=0)
    xc = cb_ref[...]
    for j in range(n_taps):
        xc = xc + ext[j * SUBLANES:j * SUBLANES + rows] * cw_ref[j:j + 1, :]
    xb = xc.astype(BF16)
    r = jax.nn.sigmoid(jnp.dot(xb, wa_ref[...], preferred_element_type=F32) + ba_ref[...])
    i = jax.nn.sigmoid(jnp.dot(xb, wi_ref[...], preferred_element_type=F32) + bi_ref[...])
    z = -ap_ref[...]
    softplus = jnp.maximum(z, 0.0) + jnp.log(1.0 + jnp.exp(-jnp.abs(z)))
    log_a = (-LRU_C) * r * softplus
    a = jnp.exp(log_a)
    a_s[...] = a
    b_s[...] = jnp.sqrt(1.0 - a * a) * i * xc

    def step(t, h):
        tt = steps - 1 - t if reverse else t
        sl = pl.ds(pl.multiple_of(tt * SUBLANES, SUBLANES), SUBLANES)
        h = a_s[sl, :] * h + b_s[sl, :]
        b_s[sl, :] = h
        return h
    h_last = lax.fori_loop(0, steps, step, carry[...], unroll=8)
    carry[...] = h_last
    hl_ref[...] = h_last
    if combine:
        o_ref[...] = ((hf_ref[...] + b_s[...]) * g_ref[...]).astype(o_ref.dtype)
    else:
        o_ref[...] = b_s[...]


def _lru_scan(y, conv_w, conv_b, w_a, b_a, w_i, b_i, a_param, h0, direction, reverse, hf=None):
    rows = y.shape[0]
    d_rnn = y.shape[1] // 2
    n_taps = conv_w.shape[0]
    width = w_a.shape[-1]
    n_blk = d_rnn // width
    tokens = rows // SUBLANES
    tc = _pick(tokens, (256, 128, 64, 32, 16, 8))
    rc = tc * SUBLANES
    n_chunks = tokens // tc
    halo_l = CONV_PAD_LEFT * SUBLANES
    halo_r = (n_taps - 1 - CONV_PAD_LEFT) * SUBLANES
    combine = hf is not None
    last_l = rows // halo_l - 1
    last_r = rows // halo_r - 1

    def ce(c):
        return n_chunks - 1 - c if reverse else c

    vec = lambda: pl.BlockSpec((None, 1, width), lambda h, c: (direction, 0, h))
    in_specs = [
        pl.BlockSpec((rc, width), lambda h, c: (ce(c), n_blk + h)),
        pl.BlockSpec((halo_l, width),
                     lambda h, c: (jnp.maximum(ce(c) * (rc // halo_l) - 1, 0), n_blk + h)),
        pl.BlockSpec((halo_r, width),
                     lambda h, c: (jnp.minimum((ce(c) + 1) * (rc // halo_r), last_r), n_blk + h)),
        pl.BlockSpec((n_taps, width), lambda h, c: (0, h)),
        pl.BlockSpec((1, width), lambda h, c: (0, h)),
        pl.BlockSpec((None, None, width, width), lambda h, c: (direction, h, 0, 0)),
        vec(),
        pl.BlockSpec((None, None, width, width), lambda h, c: (direction, h, 0, 0)),
        vec(),
        vec(),
        pl.BlockSpec((SUBLANES, width), lambda h, c: (0, h)),
    ]
    args = [y, y, y, conv_w, conv_b.reshape(1, d_rnn), w_a, b_a.reshape(2, 1, d_rnn), w_i,
            b_i.reshape(2, 1, d_rnn), a_param.reshape(2, 1, d_rnn), h0]
    if combine:
        in_specs += [pl.BlockSpec((rc, width), lambda h, c: (ce(c), h)),
                     pl.BlockSpec((rc, width), lambda h, c: (ce(c), h))]
        args += [hf, y]
    del last_l
    out_dtype = BF16 if combine else F32
    return pl.pallas_call(
        functools.partial(_lru_scan_kernel, reverse=reverse, combine=combine, n_chunks=n_chunks,
                          n_taps=n_taps),
        grid=(n_blk, n_chunks),
        in_specs=in_specs,
        out_specs=[pl.BlockSpec((rc, width), lambda h, c: (ce(c), h)),
                   pl.BlockSpec((SUBLANES, width), lambda h, c: (0, h))],
        out_shape=[jax.ShapeDtypeStruct((rows, d_rnn), out_dtype),
                   jax.ShapeDtypeStruct((SUBLANES, d_rnn), F32)],
        scratch_shapes=[pltpu.VMEM((rc, width), F32), pltpu.VMEM((rc, width), F32),
                        pltpu.VMEM((SUBLANES, width), F32)],
        compiler_params=_params(("arbitrary", "arbitrary")),
        name="lru_scan",
    )(*args)


def _bcast_mod(m):
    return jnp.broadcast_to(m[..., None, :], m.shape[:-1] + (SUBLANES, m.shape[-1]))


def kernel(x, c, ctx, c_ctx, w_ada, b_ada, ln_g, ln_b, ffn_w_gate, ffn_w_up, ffn_w_down, attn_w_qkv, attn_w_o, attn_lambda_q1, attn_lambda_k1, attn_lambda_q2, attn_lambda_k2, attn_subln_g, lru_w_in, lru_conv_w, lru_conv_b, lru_w_a, lru_b_a, lru_w_i, lru_b_i, lru_a_param, lru_w_out):
    batch, seq, d = x.shape
    n_ctx = ctx.shape[1]
    depth = w_ada.shape[0]
    assert depth == 2 and batch == SUBLANES
    alpha = (2.0 * depth) ** 0.25

    ada_rows = 2 * SUBLANES
    cc = jnp.zeros((ada_rows, d), F32).at[:batch].set(c).at[batch].set(c_ctx)
    ada = _ada(cc, w_ada, b_ada).reshape(depth, ada_rows, 3, 3, d)

    tm_lat = _pick(seq, (512, 256, 128))
    tm_ctx = _pick(batch * n_ctx, (512, 256, 128))
    tf_lat = _pick(seq, (1024, 512, 256, 128))
    tf_ctx = _pick(batch * n_ctx, (1024, 512, 256, 128))
    zeros_lat = jnp.zeros((batch * seq // tm_lat,), jnp.int32)
    zeros_ctx = jnp.zeros((batch * n_ctx // tm_ctx,), jnp.int32)
    zeros_lat_f = jnp.zeros((batch * seq // tf_lat,), jnp.int32)
    zeros_ctx_f = jnp.zeros((batch * n_ctx // tf_ctx,), jnp.int32)
    bf = lambda w: w.astype(BF16)

    def batch_of_tile(tile):
        return jnp.arange(batch * seq // tile, dtype=jnp.int32) // (seq // tile)

    mod_b = _bcast_mod(ada[0, :batch])
    mod_c = _bcast_mod(ada[0, batch:batch + 1])
    tile_b = batch_of_tile(tm_lat)
    tile_bf = batch_of_tile(tf_lat)
    xl = x.reshape(batch * seq, d)
    xc = ctx.reshape(batch * n_ctx, d)

    ffn_w = (bf(ffn_w_gate), bf(ffn_w_up), bf(ffn_w_down))
    xl = _ffn(xl, tile_bf, mod_b[:, 0], ffn_w, (0, 0), ln_g[0, 0], ln_b[0, 0], alpha, tf_lat)
    xc = _ffn(xc, zeros_ctx_f, mod_c[:, 0], ffn_w, (0, 0), ln_g[0, 0], ln_b[0, 0], alpha, tf_ctx)

    lambda_init = 0.8 - 0.6 * math.exp(-0.3 * 0)
    w_qkv = bf(attn_w_qkv[0])
    tq_lat = _pick(seq, (1024, 512, 256, 128))
    tile_q = jnp.arange(batch * seq // tq_lat, dtype=jnp.int32) // (seq // tq_lat)
    qkv_l = _qkv(xl, batch, tile_q, mod_b[:, 1], w_qkv, *_rope_tables(seq, True))
    tq_ctx = _pick(n_ctx, (1024, 512, 256, 128))
    qkv_c = _qkv(xc, batch, jnp.zeros((batch * n_ctx // tq_ctx,), jnp.int32), mod_c[:, 1], w_qkv,
                 *_rope_tables(n_ctx, False))
    lam_params = (attn_lambda_q1[0], attn_lambda_k1[0], attn_lambda_q2[0], attn_lambda_k2[0])
    o_l = _attention(qkv_l, (qkv_c, qkv_l), lam_params, attn_subln_g[0], lambda_init)
    o_c = _attention(qkv_c, (qkv_c,), lam_params, attn_subln_g[0], lambda_init)
    w_o = bf(attn_w_o[0])
    xl = _proj_ln(o_l.reshape(batch * seq, d), xl, tile_b, mod_b[:, 1], w_o, ln_g[0, 1], ln_b[0, 1],
                  alpha, tm_lat)
    xc = _proj_ln(o_c.reshape(batch * n_ctx, d), xc, zeros_ctx, mod_c[:, 1], w_o, ln_g[0, 1],
                  ln_b[0, 1], alpha, tm_ctx)

    xl = _ffn(xl, tile_bf, mod_b[:, 2], ffn_w, (0, 1), ln_g[0, 2], ln_b[0, 2], alpha, tf_lat)
    xc = _ffn(xc, zeros_ctx_f, mod_c[:, 2], ffn_w, (0, 1), ln_g[0, 2], ln_b[0, 2], alpha, tf_ctx)

    xl = xl.reshape(batch, seq, d).transpose(1, 0, 2).reshape(seq * batch, d)
    xc = xc.reshape(batch, n_ctx, d).transpose(1, 0, 2).reshape(n_ctx * batch, d)
    mod_p = ada[1, :batch].transpose(1, 2, 0, 3)[None]
    mod_c = _bcast_mod(ada[1, batch:batch + 1])

    xl = _ffn(xl, zeros_lat_f, mod_p[:, 0], ffn_w, (1, 0), ln_g[1, 0], ln_b[1, 0], alpha, tf_lat)
    xc = _ffn(xc, zeros_ctx_f, mod_c[:, 0], ffn_w, (1, 0), ln_g[1, 0], ln_b[1, 0], alpha, tf_ctx)

    w_in = bf(lru_w_in[0])
    y_l = _lru_in(xl, zeros_lat_f, mod_p[:, 1], w_in, tf_lat)
    y_c = _lru_in(xc, zeros_ctx_f, mod_c[:, 1], w_in, tf_ctx)
    d_rnn = w_in.shape[1] // 2
    w_a, w_i = bf(lru_w_a[0]), bf(lru_w_i[0])
    scan = functools.partial(_lru_scan, conv_w=lru_conv_w[0], conv_b=lru_conv_b[0], w_a=w_a,
                             b_a=lru_b_a[0], w_i=w_i, b_i=lru_b_i[0], a_param=lru_a_param[0])
    h_zero = jnp.zeros((SUBLANES, d_rnn), F32)
    _, h0_f = scan(y_c, h0=h_zero, direction=0, reverse=False)
    _, h0_b = scan(y_c, h0=h_zero, direction=1, reverse=True)
    hf, _ = scan(y_l, h0=h0_f, direction=0, reverse=False)
    hg, _ = scan(y_l, h0=h0_b, direction=1, reverse=True, hf=hf)
    xl = _proj_ln(hg, xl, zeros_lat, mod_p[:, 1], bf(lru_w_out[0]), ln_g[1, 1], ln_b[1, 1], alpha,
                  tm_lat)

    xl = _ffn(xl, zeros_lat_f, mod_p[:, 2], ffn_w, (1, 1), ln_g[1, 2], ln_b[1, 2], alpha, tf_lat)
    return xl.reshape(seq, batch, d).transpose(1, 0, 2)
```

```python
import functools
import math

import jax
import jax.numpy as jnp
from jax import lax
from jax.experimental import pallas as pl
from jax.experimental.pallas import tpu as pltpu

F32 = jnp.float32
BF16 = jnp.bfloat16

SUBLANES = 8
LANES = 128
VMEM_LIMIT_BYTES = 56 * 1024 * 1024
EPILOGUE_ROWS = 256
ATTN_SUB_ROWS = 512
ATTN_SUB_TILES = 2
ATTN_MIN_ROW_SUM = 2.0 ** -80
ATTN_BOUND_SLACK = 1.001
ATTN_ONES_ROWS = 16

GRID_W = 64
DA_HEAD_DIM = 64
DA_V_DIM = 2 * DA_HEAD_DIM
ROPE_BASE = 10000.0
AXIS_ROT_DIM = DA_HEAD_DIM // 2
ROT_HALF = AXIS_ROT_DIM // 2
CONV_PAD_LEFT = 2
LRU_C = 8.0
FFN_RES_WEIGHT = 0.5
LN_EPS = 1e-6
GELU_C0 = math.sqrt(2.0 / math.pi)
GELU_C1 = 0.044715


def _params(semantics):
    return pltpu.CompilerParams(dimension_semantics=semantics, vmem_limit_bytes=VMEM_LIMIT_BYTES)


def _pick(n, prefs):
    for p in prefs:
        if n % p == 0:
            return p
    return n


def _layer_norm(y, g, b):
    mu = jnp.mean(y, axis=-1, keepdims=True)
    yc = y - mu
    var = jnp.mean(yc * yc, axis=-1, keepdims=True)
    return yc * lax.rsqrt(var + LN_EPS) * g + b


def _row_chunks(n_rows, chunk, fn):
    def body(r, carry):
        fn(pl.ds(pl.multiple_of(r * chunk, chunk), chunk))
        return carry
    lax.fori_loop(0, n_rows // chunk, body, 0)


def _modulate_rows(x_ref, mod_ref, u_ref, chunk):
    tm, d = x_ref.shape
    shift = mod_ref[0]
    scale1 = 1.0 + mod_ref[1]

    def fn(rows):
        xx = x_ref[rows, :].reshape(chunk // SUBLANES, SUBLANES, d)
        u_ref[rows, :] = (xx * scale1 + shift).reshape(chunk, d).astype(u_ref.dtype)
    _row_chunks(tm, chunk, fn)


def _residual_ln(x, h, gate, lng, lnb, alpha):
    rows, d = x.shape
    y = (alpha * x.reshape(rows // SUBLANES, SUBLANES, d)
         + gate * h.reshape(rows // SUBLANES, SUBLANES, d))
    return _layer_norm(y, lng, lnb).reshape(rows, d)


def _ada_kernel(c_ref, w_ref, b_ref, o_ref):
    c = c_ref[...]
    s = c * jax.nn.sigmoid(c)
    o_ref[...] = jnp.dot(s.astype(BF16), w_ref[...].astype(BF16),
                         preferred_element_type=F32) + b_ref[...]


def _ada(cc, w_ada, b_ada):
    depth, d, n = w_ada.shape
    rows = cc.shape[0]
    tn = _pick(n, (1024, 512, 256, 128))
    return pl.pallas_call(
        _ada_kernel,
        grid=(depth, n // tn),
        in_specs=[
            pl.BlockSpec((rows, d), lambda l, j: (0, 0)),
            pl.BlockSpec((None, d, tn), lambda l, j: (l, 0, j)),
            pl.BlockSpec((None, 1, tn), lambda l, j: (l, 0, j)),
        ],
        out_specs=pl.BlockSpec((None, rows, tn), lambda l, j: (l, 0, j)),
        out_shape=jax.ShapeDtypeStruct((depth, rows, n), F32),
        compiler_params=_params(("arbitrary", "arbitrary")),
        name="ada",
    )(cc, w_ada, b_ada.reshape(depth, 1, n))


def _ffn_kernel(tmod_ref, x_ref, mod_ref, wg_ref, wu_ref, wd_ref, lng_ref, lnb_ref, o_ref, u_ref,
                *, alpha, chunk):
    f = pl.program_id(1)
    tm, d = x_ref.shape

    @pl.when(f == 0)
    def _init():
        _modulate_rows(x_ref, mod_ref, u_ref, chunk)
        o_ref[...] = jnp.zeros((tm, d), F32)

    def swiglu(rows):
        u = u_ref[rows, :]
        ga = jnp.dot(u, wg_ref[...], preferred_element_type=F32)
        up = jnp.dot(u, wu_ref[...], preferred_element_type=F32)
        h = (ga * jax.nn.sigmoid(ga) * up).astype(BF16)
        return jnp.dot(h, wd_ref[...], preferred_element_type=F32)

    last = pl.num_programs(1) - 1

    @pl.when(f < last)
    def _accumulate():
        o_ref[...] += swiglu(slice(None))

    @pl.when(f == last)
    def _finish():
        gate, lng, lnb = FFN_RES_WEIGHT * mod_ref[2], lng_ref[...], lnb_ref[...]
        group = min(tm, EPILOGUE_ROWS)
        for r in range(0, tm, group):
            rows = slice(r, r + group)
            o_ref[rows, :] = _residual_ln(x_ref[rows, :], o_ref[rows, :] + swiglu(rows), gate,
                                          lng, lnb, alpha)


def _ffn(x, tile_mod, mod, weights, which, lng, lnb, alpha, tm):
    rows, d = x.shape
    wg, wu, wd = weights
    li, ki = which
    ff = wg.shape[-1]
    tf = _pick(ff, (512, 256, 128))
    chunk = _pick(tm, (128, 64, 32, 16))
    kern = functools.partial(_ffn_kernel, alpha=alpha, chunk=chunk)
    grid_spec = pltpu.PrefetchScalarGridSpec(
        num_scalar_prefetch=1,
        grid=(rows // tm, ff // tf),
        in_specs=[
            pl.BlockSpec((tm, d), lambda i, f, t: (i, 0)),
            pl.BlockSpec((None, 3, SUBLANES, d), lambda i, f, t: (t[i], 0, 0, 0)),
            pl.BlockSpec((None, None, d, tf), lambda i, f, t: (li, ki, 0, f)),
            pl.BlockSpec((None, None, d, tf), lambda i, f, t: (li, ki, 0, f)),
            pl.BlockSpec((None, None, tf, d), lambda i, f, t: (li, ki, f, 0)),
            pl.BlockSpec((1, d), lambda i, f, t: (0, 0)),
            pl.BlockSpec((1, d), lambda i, f, t: (0, 0)),
        ],
        out_specs=pl.BlockSpec((tm, d), lambda i, f, t: (i, 0)),
        scratch_shapes=[pltpu.VMEM((tm, d), BF16)],
    )
    return pl.pallas_call(
        kern,
        grid_spec=grid_spec,
        out_shape=jax.ShapeDtypeStruct((rows, d), F32),
        compiler_params=_params(("arbitrary", "arbitrary")),
        name="ffn",
    )(tile_mod, x, mod, wg, wu, wd, lng.reshape(1, d), lnb.reshape(1, d))


def _qkv_kernel(tmod_ref, x_ref, mod_ref, w_ref, cs_ref, sa_ref, sb_ref, o_ref, u_ref, *, chunk,
                n_rot):
    j = pl.program_id(2)

    @pl.when(j == 0)
    def _init():
        _modulate_rows(x_ref, mod_ref, u_ref, chunk)

    heads_per_chunk = o_ref.shape[0]
    tm = u_ref.shape[0]
    group = min(tm, EPILOGUE_ROWS)

    @pl.when(j < n_rot)
    def _rotate():
        for r in range(0, tm, group):
            acc = jnp.dot(u_ref[r:r + group, :], w_ref[...], preferred_element_type=F32)
            cs, sa, sb = cs_ref[r:r + group, :], sa_ref[r:r + group, :], sb_ref[r:r + group, :]
            for hh in range(heads_per_chunk):
                blk = acc[:, hh * LANES:(hh + 1) * LANES]
                rot = (blk * cs + pltpu.roll(blk, LANES - ROT_HALF, 1) * sa
                       + pltpu.roll(blk, ROT_HALF, 1) * sb)
                o_ref[hh, r:r + group, :] = rot.astype(o_ref.dtype)

    @pl.when(j >= n_rot)
    def _plain():
        for r in range(0, tm, group):
            acc = jnp.dot(u_ref[r:r + group, :], w_ref[...], preferred_element_type=F32)
            for hh in range(heads_per_chunk):
                o_ref[hh, r:r + group, :] = acc[:, hh * LANES:(hh + 1) * LANES].astype(o_ref.dtype)


def _qkv(x, batch, tile_mod, mod, w, cs, sa, sb):
    rows, d = x.shape
    n = rows // batch
    heads = d // DA_V_DIM
    tm = _pick(n, (1024, 512, 256, 128))
    tn = _pick(d, (512, 256, 128))
    hpc = tn // LANES
    cpw = d // tn
    chunk = _pick(tm, (128, 64, 32, 16))
    nt = n // tm
    grid_spec = pltpu.PrefetchScalarGridSpec(
        num_scalar_prefetch=1,
        grid=(batch, nt, 3 * cpw),
        in_specs=[
            pl.BlockSpec((tm, d), lambda b, t, j, tmr: (b * nt + t, 0)),
            pl.BlockSpec((None, 3, SUBLANES, d), lambda b, t, j, tmr: (tmr[b * nt + t], 0, 0, 0)),
            pl.BlockSpec((d, tn), lambda b, t, j, tmr: (0, j)),
            pl.BlockSpec((None, tm, LANES), lambda b, t, j, tmr: (jnp.minimum(j // cpw, 1), t, 0)),
            pl.BlockSpec((None, tm, LANES), lambda b, t, j, tmr: (jnp.minimum(j // cpw, 1), t, 0)),
            pl.BlockSpec((None, tm, LANES), lambda b, t, j, tmr: (jnp.minimum(j // cpw, 1), t, 0)),
        ],
        out_specs=pl.BlockSpec((None, None, hpc, tm, LANES),
                               lambda b, t, j, tmr: (j // cpw, b, j % cpw, t, 0)),
        scratch_shapes=[pltpu.VMEM((tm, d), BF16)],
    )
    return pl.pallas_call(
        functools.partial(_qkv_kernel, chunk=chunk, n_rot=2 * cpw),
        grid_spec=grid_spec,
        out_shape=jax.ShapeDtypeStruct((3, batch, heads, n, LANES), BF16),
        compiler_params=_params(("arbitrary", "arbitrary", "arbitrary")),
        name="qkv",
    )(tile_mod, x, mod, w, cs, sa, sb)


def _rope_tables(n, rotate):
    scale_q = DA_HEAD_DIM ** -0.5 * math.log2(math.e)
    ones = jnp.ones((n, LANES), F32)
    zeros = jnp.zeros((n, LANES), F32)
    if not rotate:
        return (jnp.stack([scale_q * ones, ones]), jnp.stack([zeros] * 2), jnp.stack([zeros] * 2))
    rows = n // GRID_W
    row = jnp.repeat(jnp.arange(rows, dtype=F32), GRID_W)
    col = jnp.tile(jnp.arange(GRID_W, dtype=F32), rows)
    inv_freq = ROPE_BASE ** (-jnp.arange(0, AXIS_ROT_DIM, 2, dtype=F32) / AXIS_ROT_DIM)
    ang_r = row[:, None] * inv_freq[None, :]
    ang_c = col[:, None] * inv_freq[None, :]
    z = jnp.zeros_like(ang_r)
    cos64 = jnp.concatenate([jnp.cos(ang_r), jnp.cos(ang_r), jnp.cos(ang_c), jnp.cos(ang_c)], -1)
    sa64 = jnp.concatenate([-jnp.sin(ang_r), z, -jnp.sin(ang_c), z], -1)
    sb64 = jnp.concatenate([z, jnp.sin(ang_r), z, jnp.sin(ang_c)], -1)
    cs = jnp.tile(cos64, (1, 2))
    sa = jnp.tile(sa64, (1, 2))
    sb = jnp.tile(sb64, (1, 2))
    return (jnp.stack([scale_q * cs, cs]), jnp.stack([scale_q * sa, sa]),
            jnp.stack([scale_q * sb, sb]))


def _attn_kernel(*refs, n_parts, lambda_init, n_sub):
    q_ref = refs[0]
    k_refs = refs[1:1 + n_parts]
    v_refs = refs[1 + n_parts:1 + 2 * n_parts]
    lq1_ref, lk1_ref, lq2_ref, lk2_ref, g_ref, o_ref = refs[1 + 2 * n_parts:7 + 2 * n_parts]
    vt_refs = refs[7 + 2 * n_parts:7 + 3 * n_parts]
    kmax_ref = refs[7 + 3 * n_parts]
    lam = (jnp.exp(jnp.sum(lq1_ref[...] * lk1_ref[...], axis=-1, keepdims=True))
           - jnp.exp(jnp.sum(lq2_ref[...] * lk2_ref[...], axis=-1, keepdims=True)) + lambda_init)
    tq = q_ref.shape[0] // n_sub
    lane = lax.broadcasted_iota(jnp.int32, (tq, LANES), 1)
    sub_head = (lane < DA_HEAD_DIM, lane >= DA_HEAD_DIM)
    nt_dims = (((1,), (1,)), ((), ()))

    def masked_q(t):
        q = q_ref[t * tq:(t + 1) * tq, :]
        zero = jnp.zeros_like(q)
        return [jnp.where(mask, q, zero) for mask in sub_head]

    def finish(t, o):
        y = o * lax.rsqrt(jnp.mean(o * o, axis=-1, keepdims=True) + LN_EPS) * g_ref[...]
        o_ref[t * tq:(t + 1) * tq, :] = (y * (1.0 - lambda_init)).astype(o_ref.dtype)

    @pl.when(pl.program_id(2) == 0)
    def _per_head_setup():
        for v_ref, vt_ref in zip(v_refs, vt_refs):
            vt_ref[0:LANES, :] = jnp.transpose(v_ref[...].astype(F32)).astype(BF16)
            vt_ref[LANES:, :] = jnp.ones((vt_ref.shape[0] - LANES, vt_ref.shape[1]), BF16)
        k_abs_max = functools.reduce(jnp.maximum, [
            jnp.max(jnp.abs(k_ref[...].astype(F32)), axis=0, keepdims=True) for k_ref in k_refs])
        kmax_ref[...] = jnp.broadcast_to(k_abs_max, kmax_ref.shape)

    k_abs_max = kmax_ref[0:1, :]
    smallest_sum = None
    for t in range(n_sub):
        qs = masked_q(t)
        reach = jnp.abs(q_ref[t * tq:(t + 1) * tq, :].astype(F32)) * k_abs_max
        norm = []
        for e in range(2):
            bound = jnp.sum(jnp.where(sub_head[e], reach, 0.0), axis=-1,
                            keepdims=True) * ATTN_BOUND_SLACK
            shift = jnp.transpose(jnp.broadcast_to(bound, (tq, LANES)))[0:1, :]
            tot = None
            for k_ref, vt_ref in zip(k_refs, vt_refs):
                s_t = lax.dot_general(k_ref[...], qs[e], nt_dims, preferred_element_type=F32)
                p_t = jnp.exp2(s_t - shift).astype(BF16)
                part = jnp.dot(vt_ref[...], p_t, preferred_element_type=F32)
                tot = part if tot is None else tot + part
            norm.append(tot)
            low = jnp.min(tot[LANES:LANES + 1])
            smallest_sum = low if smallest_sum is None else jnp.minimum(smallest_sum, low)
        o_t = (norm[0][:LANES] * (1.0 / norm[0][LANES:LANES + 1])
               - norm[1][:LANES] * (lam / norm[1][LANES:LANES + 1]))
        finish(t, jnp.transpose(o_t))

    def exact_max():
        for t in range(n_sub):
            qs = masked_q(t)
            scores = [[lax.dot_general(qe, k_ref[...], nt_dims, preferred_element_type=F32)
                       for k_ref in k_refs] for qe in qs]
            weights = []
            for e in range(2):
                m = functools.reduce(jnp.maximum,
                                     [jnp.max(s, axis=-1, keepdims=True) for s in scores[e]])
                ps = [jnp.exp2(s - m) for s in scores[e]]
                l = functools.reduce(jnp.add, [jnp.sum(p, axis=-1, keepdims=True) for p in ps])
                weights.append((ps, 1.0 / l))
            o = None
            for p, v_ref in enumerate(v_refs):
                a = weights[0][0][p] * weights[0][1] - weights[1][0][p] * (lam * weights[1][1])
                part = jnp.dot(a.astype(BF16), v_ref[...], preferred_element_type=F32)
                o = part if o is None else o + part
            finish(t, o)

    pl.when(jnp.logical_not(smallest_sum >= ATTN_MIN_ROW_SUM))(exact_max)


def _attention(qkv, kv_parts, lam_params, subln_g, lambda_init):
    _, batch, heads, n, _ = qkv.shape
    tq = _pick(n, (ATTN_SUB_TILES * ATTN_SUB_ROWS, ATTN_SUB_ROWS, 128))
    n_sub = max(tq // ATTN_SUB_ROWS, 1)
    in_specs = [pl.BlockSpec((None, None, None, tq, LANES), lambda b, h, t: (0, b, h, t, 0))]
    in_specs += [pl.BlockSpec((None, None, None, part.shape[3], LANES),
                              lambda b, h, t: (1, b, h, 0, 0)) for part in kv_parts]
    in_specs += [pl.BlockSpec((None, None, None, part.shape[3], LANES),
                              lambda b, h, t: (2, b, h, 0, 0)) for part in kv_parts]
    args = [qkv] + list(kv_parts) + list(kv_parts)
    for p in lam_params:
        in_specs.append(pl.BlockSpec((1, DA_HEAD_DIM), lambda b, h, t: (0, 0)))
        args.append(p.reshape(1, DA_HEAD_DIM))
    in_specs.append(pl.BlockSpec((1, DA_V_DIM), lambda b, h, t: (0, 0)))
    args.append(subln_g.reshape(1, DA_V_DIM))
    return pl.pallas_call(
        functools.partial(_attn_kernel, n_parts=len(kv_parts), lambda_init=lambda_init,
                          n_sub=n_sub),
        grid=(batch, heads, n // tq),
        in_specs=in_specs,
        out_specs=pl.BlockSpec((None, tq, LANES), lambda b, h, t: (b, t, h)),
        out_shape=jax.ShapeDtypeStruct((batch, n, heads * LANES), BF16),
        scratch_shapes=[pltpu.VMEM((LANES + ATTN_ONES_ROWS, part.shape[3]), BF16)
                        for part in kv_parts] + [pltpu.VMEM((SUBLANES, LANES), F32)],
        compiler_params=_params(("arbitrary", "arbitrary", "arbitrary")),
        name="attn",
    )(*args)


def _proj_ln_kernel(tmod_ref, h_ref, x_ref, mod_ref, w_ref, lng_ref, lnb_ref, o_ref, *, alpha, chunk):
    tm, d = x_ref.shape
    gate, lng, lnb = mod_ref[2], lng_ref[...], lnb_ref[...]
    for r in range(0, tm, chunk):
        hh = jnp.dot(h_ref[r:r + chunk, :], w_ref[...], preferred_element_type=F32)
        o_ref[r:r + chunk, :] = _residual_ln(x_ref[r:r + chunk, :], hh, gate, lng, lnb, alpha)


def _proj_ln(h, x, tile_mod, mod, w, lng, lnb, alpha, tm):
    rows, d = x.shape
    k = h.shape[1]
    chunk = _pick(tm, (128, 64, 32, 16))
    grid_spec = pltpu.PrefetchScalarGridSpec(
        num_scalar_prefetch=1,
        grid=(rows // tm,),
        in_specs=[
            pl.BlockSpec((tm, k), lambda i, t: (i, 0)),
            pl.BlockSpec((tm, d), lambda i, t: (i, 0)),
            pl.BlockSpec((None, 3, SUBLANES, d), lambda i, t: (t[i], 0, 0, 0)),
            pl.BlockSpec((k, d), lambda i, t: (0, 0)),
            pl.BlockSpec((1, d), lambda i, t: (0, 0)),
            pl.BlockSpec((1, d), lambda i, t: (0, 0)),
        ],
        out_specs=pl.BlockSpec((tm, d), lambda i, t: (i, 0)),
    )
    return pl.pallas_call(
        functools.partial(_proj_ln_kernel, alpha=alpha, chunk=chunk),
        grid_spec=grid_spec,
        out_shape=jax.ShapeDtypeStruct((rows, d), F32),
        compiler_params=_params(("arbitrary",)),
        name="proj_ln",
    )(tile_mod, h, x, mod, w, lng.reshape(1, d), lnb.reshape(1, d))


def _lru_in_kernel(tmod_ref, x_ref, mod_ref, w_ref, o_ref, u_ref, *, n_gelu, chunk):
    j = pl.program_id(1)

    @pl.when(j == 0)
    def _init():
        _modulate_rows(x_ref, mod_ref, u_ref, chunk)

    tm = u_ref.shape[0]
    group = min(tm, EPILOGUE_ROWS)

    @pl.when(j < n_gelu)
    def _gate_branch():
        for r in range(0, tm, group):
            acc = jnp.dot(u_ref[r:r + group, :], w_ref[...], preferred_element_type=F32)
            cdf = 0.5 * (1.0 + jnp.tanh(GELU_C0 * (acc + GELU_C1 * (acc * acc * acc))))
            o_ref[r:r + group, :] = acc * cdf

    @pl.when(j >= n_gelu)
    def _recurrent_branch():
        o_ref[...] = jnp.dot(u_ref[...], w_ref[...], preferred_element_type=F32)


def _lru_in(x, tile_mod, mod, w, tm):
    rows, d = x.shape
    n = w.shape[1]
    tn = _pick(n // 2, (512, 256, 128))
    chunk = _pick(tm, (128, 64, 32, 16))
    grid_spec = pltpu.PrefetchScalarGridSpec(
        num_scalar_prefetch=1,
        grid=(rows // tm, n // tn),
        in_specs=[
            pl.BlockSpec((tm, d), lambda i, j, t: (i, 0)),
            pl.BlockSpec((None, 3, SUBLANES, d), lambda i, j, t: (t[i], 0, 0, 0)),
            pl.BlockSpec((d, tn), lambda i, j, t: (0, j)),
        ],
        out_specs=pl.BlockSpec((tm, tn), lambda i, j, t: (i, j)),
        scratch_shapes=[pltpu.VMEM((tm, d), BF16)],
    )
    return pl.pallas_call(
        functools.partial(_lru_in_kernel, n_gelu=(n // 2) // tn, chunk=chunk),
        grid_spec=grid_spec,
        out_shape=jax.ShapeDtypeStruct((rows, n), F32),
        compiler_params=_params(("arbitrary", "arbitrary")),
        name="lru_in",
    )(tile_mod, x, mod, w)


def _lru_scan_kernel(*refs, reverse, combine, n_chunks, n_taps):
    if combine:
        (xr_ref, prev_ref, next_ref, cw_ref, cb_ref, wa_ref, ba_ref, wi_ref, bi_ref, ap_ref, h0_ref,
         hf_ref, g_ref, o_ref, hl_ref, a_s, b_s, carry) = refs
    else:
        (xr_ref, prev_ref, next_ref, cw_ref, cb_ref, wa_ref, ba_ref, wi_ref, bi_ref, ap_ref, h0_ref,
         o_ref, hl_ref, a_s, b_s, carry) = refs
    c = pl.program_id(1)
    c_eff = n_chunks - 1 - c if reverse else c
    rows, width = xr_ref.shape
    steps = rows // SUBLANES

    @pl.when(c == 0)
    def _init():
        carry[...] = h0_ref[...]

    halo_l = jnp.where(c_eff > 0, prev_ref[...], 0.0)
    halo_r = jnp.where(c_eff < n_chunks - 1, next_ref[...], 0.0)
    ext = jnp.concatenate([halo_l, xr_ref[...], halo_r], axis=0)
    xc = cb_ref[...]
    for j in range(n_taps):
        xc = xc + ext[j * SUBLANES:j * SUBLANES + rows] * cw_ref[j:j + 1, :]
    xb = xc.astype(BF16)
    r = jax.nn.sigmoid(jnp.dot(xb, wa_ref[...], preferred_element_type=F32) + ba_ref[...])
    i = jax.nn.sigmoid(jnp.dot(xb, wi_ref[...], preferred_element_type=F32) + bi_ref[...])
    z = -ap_ref[...]
    softplus = jnp.maximum(z, 0.0) + jnp.log(1.0 + jnp.exp(-jnp.abs(z)))
    log_a = (-LRU_C) * r * softplus
    a = jnp.exp(log_a)
    a_s[...] = a
    b_s[...] = jnp.sqrt(1.0 - a * a) * i * xc

    def step(t, h):
        tt = steps - 1 - t if reverse else t
        sl = pl.ds(pl.multiple_of(tt * SUBLANES, SUBLANES), SUBLANES)
        h = a_s[sl, :] * h + b_s[sl, :]
        b_s[sl, :] = h
        return h
    h_last = lax.fori_loop(0, steps, step, carry[...], unroll=8)
    carry[...] = h_last
    hl_ref[...] = h_last
    if combine:
        o_ref[...] = ((hf_ref[...] + b_s[...]) * g_ref[...]).astype(o_ref.dtype)
    else:
        o_ref[...] = b_s[...]


def _lru_scan(y, conv_w, conv_b, w_a, b_a, w_i, b_i, a_param, h0, direction, reverse, hf=None):
    rows = y.shape[0]
    d_rnn = y.shape[1] // 2
    n_taps = conv_w.shape[0]
    width = w_a.shape[-1]
    n_blk = d_rnn // width
    tokens = rows // SUBLANES
    tc = _pick(tokens, (256, 128, 64, 32, 16, 8))
    rc = tc * SUBLANES
    n_chunks = tokens // tc
    halo_l = CONV_PAD_LEFT * SUBLANES
    halo_r = (n_taps - 1 - CONV_PAD_LEFT) * SUBLANES
    combine = hf is not None
    last_l = rows // halo_l - 1
    last_r = rows // halo_r - 1

    def ce(c):
        return n_chunks - 1 - c if reverse else c

    vec = lambda: pl.BlockSpec((None, 1, width), lambda h, c: (direction, 0, h))
    in_specs = [
        pl.BlockSpec((rc, width), lambda h, c: (ce(c), n_blk + h)),
        pl.BlockSpec((halo_l, width),
                     lambda h, c: (jnp.maximum(ce(c) * (rc // halo_l) - 1, 0), n_blk + h)),
        pl.BlockSpec((halo_r, width),
                     lambda h, c: (jnp.minimum((ce(c) + 1) * (rc // halo_r), last_r), n_blk + h)),
        pl.BlockSpec((n_taps, width), lambda h, c: (0, h)),
        pl.BlockSpec((1, width), lambda h, c: (0, h)),
        pl.BlockSpec((None, None, width, width), lambda h, c: (direction, h, 0, 0)),
        vec(),
        pl.BlockSpec((None, None, width, width), lambda h, c: (direction, h, 0, 0)),
        vec(),
        vec(),
        pl.BlockSpec((SUBLANES, width), lambda h, c: (0, h)),
    ]
    args = [y, y, y, conv_w, conv_b.reshape(1, d_rnn), w_a, b_a.reshape(2, 1, d_rnn), w_i,
            b_i.reshape(2, 1, d_rnn), a_param.reshape(2, 1, d_rnn), h0]
    if combine:
        in_specs += [pl.BlockSpec((rc, width), lambda h, c: (ce(c), h)),
                     pl.BlockSpec((rc, width), lambda h, c: (ce(c), h))]
        args += [hf, y]
    del last_l
    out_dtype = BF16 if combine else F32
    return pl.pallas_call(
        functools.partial(_lru_scan_kernel, reverse=reverse, combine=combine, n_chunks=n_chunks,
                          n_taps=n_taps),
        grid=(n_blk, n_chunks),
        in_specs=in_specs,
        out_specs=[pl.BlockSpec((rc, width), lambda h, c: (ce(c), h)),
                   pl.BlockSpec((SUBLANES, width), lambda h, c: (0, h))],
        out_shape=[jax.ShapeDtypeStruct((rows, d_rnn), out_dtype),
                   jax.ShapeDtypeStruct((SUBLANES, d_rnn), F32)],
        scratch_shapes=[pltpu.VMEM((rc, width), F32), pltpu.VMEM((rc, width), F32),
                        pltpu.VMEM((SUBLANES, width), F32)],
        compiler_params=_params(("arbitrary", "arbitrary")),
        name="lru_scan",
    )(*args)


def _bcast_mod(m):
    return jnp.broadcast_to(m[..., None, :], m.shape[:-1] + (SUBLANES, m.shape[-1]))


def kernel(x, c, ctx, c_ctx, w_ada, b_ada, ln_g, ln_b, ffn_w_gate, ffn_w_up, ffn_w_down, attn_w_qkv, attn_w_o, attn_lambda_q1, attn_lambda_k1, attn_lambda_q2, attn_lambda_k2, attn_subln_g, lru_w_in, lru_conv_w, lru_conv_b, lru_w_a, lru_b_a, lru_w_i, lru_b_i, lru_a_param, lru_w_out):
    batch, seq, d = x.shape
    n_ctx = ctx.shape[1]
    depth = w_ada.shape[0]
    assert depth == 2 and batch == SUBLANES
    alpha = (2.0 * depth) ** 0.25

    ada_rows = 2 * SUBLANES
    cc = jnp.zeros((ada_rows, d), F32).at[:batch].set(c).at[batch].set(c_ctx)
    ada = _ada(cc, w_ada, b_ada).reshape(depth, ada_rows, 3, 3, d)

    tm_lat = _pick(seq, (512, 256, 128))
    tm_ctx = _pick(batch * n_ctx, (512, 256, 128))
    tf_lat = _pick(seq, (1024, 512, 256, 128))
    tf_ctx = _pick(batch * n_ctx, (1024, 512, 256, 128))
    zeros_lat = jnp.zeros((batch * seq // tm_lat,), jnp.int32)
    zeros_ctx = jnp.zeros((batch * n_ctx // tm_ctx,), jnp.int32)
    zeros_lat_f = jnp.zeros((batch * seq // tf_lat,), jnp.int32)
    zeros_ctx_f = jnp.zeros((batch * n_ctx // tf_ctx,), jnp.int32)
    bf = lambda w: w.astype(BF16)

    def batch_of_tile(tile):
        return jnp.arange(batch * seq // tile, dtype=jnp.int32) // (seq // tile)

    mod_b = _bcast_mod(ada[0, :batch])
    mod_c = _bcast_mod(ada[0, batch:batch + 1])
    tile_b = batch_of_tile(tm_lat)
    tile_bf = batch_of_tile(tf_lat)
    xl = x.reshape(batch * seq, d)
    xc = ctx.reshape(batch * n_ctx, d)

    ffn_w = (bf(ffn_w_gate), bf(ffn_w_up), bf(ffn_w_down))
    xl = _ffn(xl, tile_bf, mod_b[:, 0], ffn_w, (0, 0), ln_g[0, 0], ln_b[0, 0], alpha, tf_lat)
    xc = _ffn(xc, zeros_ctx_f, mod_c[:, 0], ffn_w, (0, 0), ln_g[0, 0], ln_b[0, 0], alpha, tf_ctx)

    lambda_init = 0.8 - 0.6 * math.exp(-0.3 * 0)
    w_qkv = bf(attn_w_qkv[0])
    tq_lat = _pick(seq, (1024, 512, 256, 128))
    tile_q = jnp.arange(batch * seq // tq_lat, dtype=jnp.int32) // (seq // tq_lat)
    qkv_l = _qkv(xl, batch, tile_q, mod_b[:, 1], w_qkv, *_rope_tables(seq, True))
    tq_ctx = _pick(n_ctx, (1024, 512, 256, 128))
    qkv_c = _qkv(xc, batch, jnp.zeros((batch * n_ctx // tq_ctx,), jnp.int32), mod_c[:, 1], w_qkv,
                 *_rope_tables(n_ctx, False))
    lam_params = (attn_lambda_q1[0], attn_lambda_k1[0], attn_lambda_q2[0], attn_lambda_k2[0])
    o_l = _attention(qkv_l, (qkv_c, qkv_l), lam_params, attn_subln_g[0], lambda_init)
    o_c = _attention(qkv_c, (qkv_c,), lam_params, attn_subln_g[0], lambda_init)
    w_o = bf(attn_w_o[0])
    xl = _proj_ln(o_l.reshape(batch * seq, d), xl, tile_b, mod_b[:, 1], w_o, ln_g[0, 1], ln_b[0, 1],
                  alpha, tm_lat)
    xc = _proj_ln(o_c.reshape(batch * n_ctx, d), xc, zeros_ctx, mod_c[:, 1], w_o, ln_g[0, 1],
                  ln_b[0, 1], alpha, tm_ctx)

    xl = _ffn(xl, tile_bf, mod_b[:, 2], ffn_w, (0, 1), ln_g[0, 2], ln_b[0, 2], alpha, tf_lat)
    xc = _ffn(xc, zeros_ctx_f, mod_c[:, 2], ffn_w, (0, 1), ln_g[0, 2], ln_b[0, 2], alpha, tf_ctx)

    xl = xl.reshape(batch, seq, d).transpose(1, 0, 2).reshape(seq * batch, d)
    xc = xc.reshape(batch, n_ctx, d).transpose(1, 0, 2).reshape(n_ctx * batch, d)
    mod_p = ada[1, :batch].transpose(1, 2, 0, 3)[None]
    mod_c = _bcast_mod(ada[1, batch:batch + 1])

    xl = _ffn(xl, zeros_lat_f, mod_p[:, 0], ffn_w, (1, 0), ln_g[1, 0], ln_b[1, 0], alpha, tf_lat)
    xc = _ffn(xc, zeros_ctx_f, mod_c[:, 0], ffn_w, (1, 0), ln_g[1, 0], ln_b[1, 0], alpha, tf_ctx)

    w_in = bf(lru_w_in[0])
    y_l = _lru_in(xl, zeros_lat_f, mod_p[:, 1], w_in, tf_lat)
    y_c = _lru_in(xc, zeros_ctx_f, mod_c[:, 1], w_in, tf_ctx)
    d_rnn = w_in.shape[1] // 2
    w_a, w_i = bf(lru_w_a[0]), bf(lru_w_i[0])
    scan = functools.partial(_lru_scan, conv_w=lru_conv_w[0], conv_b=lru_conv_b[0], w_a=w_a,
                             b_a=lru_b_a[0], w_i=w_i, b_i=lru_b_i[0], a_param=lru_a_param[0])
    h_zero = jnp.zeros((SUBLANES, d_rnn), F32)
    _, h0_f = scan(y_c, h0=h_zero, direction=0, reverse=False)
    _, h0_b = scan(y_c, h0=h_zero, direction=1, reverse=True)
    hf, _ = scan(y_l, h0=h0_f, direction=0, reverse=False)
    hg, _ = scan(y_l, h0=h0_b, direction=1, reverse=True, hf=hf)
    xl = _proj_ln(hg, xl, zeros_lat, mod_p[:, 1], bf(lru_w_out[0]), ln_g[1, 1], ln_b[1, 1], alpha,
                  tm_lat)

    xl = _ffn(xl, zeros_lat_f, mod_p[:, 2], ffn_w, (1, 1), ln_g[1, 2], ln_b[1, 2], alpha, tf_lat)
    return xl.reshape(seq, batch, d).transpose(1, 0, 2)
```

```python
import functools
import math

import jax
import jax.numpy as jnp
from jax import lax
from jax.experimental import pallas as pl
from jax.experimental.pallas import tpu as pltpu

F32 = jnp.float32
BF16 = jnp.bfloat16

SUBLANES = 8
LANES = 128
VMEM_LIMIT_BYTES = 56 * 1024 * 1024
LRU_BLOCKS_PER_STEP = 5
LRU_CHUNK_TOKENS = 64
EPILOGUE_ROWS = 256
ATTN_SUB_ROWS = 512
ATTN_SUB_TILES = 2
ATTN_MIN_ROW_SUM = 2.0 ** -80
ATTN_BOUND_SLACK = 1.001
ATTN_ONES_ROWS = 16

GRID_W = 64
DA_HEAD_DIM = 64
DA_V_DIM = 2 * DA_HEAD_DIM
ROPE_BASE = 10000.0
AXIS_ROT_DIM = DA_HEAD_DIM // 2
ROT_HALF = AXIS_ROT_DIM // 2
CONV_PAD_LEFT = 2
LRU_C = 8.0
FFN_RES_WEIGHT = 0.5
LN_EPS = 1e-6
GELU_C0 = math.sqrt(2.0 / math.pi)
GELU_C1 = 0.044715


def _params(semantics):
    return pltpu.CompilerParams(dimension_semantics=semantics, vmem_limit_bytes=VMEM_LIMIT_BYTES)


def _pick(n, prefs):
    for p in prefs:
        if n % p == 0:
            return p
    return n


def _layer_norm(y, g, b):
    mu = jnp.mean(y, axis=-1, keepdims=True)
    yc = y - mu
    var = jnp.mean(yc * yc, axis=-1, keepdims=True)
    return yc * lax.rsqrt(var + LN_EPS) * g + b


def _row_chunks(n_rows, chunk, fn):
    def body(r, carry):
        fn(pl.ds(pl.multiple_of(r * chunk, chunk), chunk))
        return carry
    lax.fori_loop(0, n_rows // chunk, body, 0)


def _modulate_rows(x_ref, mod_ref, u_ref, chunk):
    tm, d = x_ref.shape
    shift = mod_ref[0]
    scale1 = 1.0 + mod_ref[1]

    def fn(rows):
        xx = x_ref[rows, :].reshape(chunk // SUBLANES, SUBLANES, d)
        u_ref[rows, :] = (xx * scale1 + shift).reshape(chunk, d).astype(u_ref.dtype)
    _row_chunks(tm, chunk, fn)


def _residual_ln(x, h, gate, lng, lnb, alpha):
    rows, d = x.shape
    y = (alpha * x.reshape(rows // SUBLANES, SUBLANES, d)
         + gate * h.reshape(rows // SUBLANES, SUBLANES, d))
    return _layer_norm(y, lng, lnb).reshape(rows, d)


def _ada_kernel(c_ref, w_ref, b_ref, o_ref):
    c = c_ref[...]
    s = c * jax.nn.sigmoid(c)
    o_ref[...] = jnp.dot(s.astype(BF16), w_ref[...].astype(BF16),
                         preferred_element_type=F32) + b_ref[...]


def _ada(cc, w_ada, b_ada):
    depth, d, n = w_ada.shape
    rows = cc.shape[0]
    tn = _pick(n, (1024, 512, 256, 128))
    return pl.pallas_call(
        _ada_kernel,
        grid=(depth, n // tn),
        in_specs=[
            pl.BlockSpec((rows, d), lambda l, j: (0, 0)),
            pl.BlockSpec((None, d, tn), lambda l, j: (l, 0, j)),
            pl.BlockSpec((None, 1, tn), lambda l, j: (l, 0, j)),
        ],
        out_specs=pl.BlockSpec((None, rows, tn), lambda l, j: (l, 0, j)),
        out_shape=jax.ShapeDtypeStruct((depth, rows, n), F32),
        compiler_params=_params(("arbitrary", "arbitrary")),
        name="ada",
    )(cc, w_ada, b_ada.reshape(depth, 1, n))


def _ffn_kernel(tmod_ref, x_ref, mod_ref, wg_ref, wu_ref, wd_ref, lng_ref, lnb_ref, o_ref, u_ref,
                *, alpha, chunk):
    f = pl.program_id(1)
    tm, d = x_ref.shape

    @pl.when(f == 0)
    def _init():
        _modulate_rows(x_ref, mod_ref, u_ref, chunk)
        o_ref[...] = jnp.zeros((tm, d), F32)

    def swiglu(rows):
        u = u_ref[rows, :]
        ga = jnp.dot(u, wg_ref[...], preferred_element_type=F32)
        up = jnp.dot(u, wu_ref[...], preferred_element_type=F32)
        h = (ga * jax.nn.sigmoid(ga) * up).astype(BF16)
        return jnp.dot(h, wd_ref[...], preferred_element_type=F32)

    last = pl.num_programs(1) - 1

    @pl.when(f < last)
    def _accumulate():
        o_ref[...] += swiglu(slice(None))

    @pl.when(f == last)
    def _finish():
        gate, lng, lnb = FFN_RES_WEIGHT * mod_ref[2], lng_ref[...], lnb_ref[...]
        group = min(tm, EPILOGUE_ROWS)
        for r in range(0, tm, group):
            rows = slice(r, r + group)
            o_ref[rows, :] = _residual_ln(x_ref[rows, :], o_ref[rows, :] + swiglu(rows), gate,
                                          lng, lnb, alpha)


def _ffn(x, tile_mod, mod, weights, which, lng, lnb, alpha, tm):
    rows, d = x.shape
    wg, wu, wd = weights
    li, ki = which
    ff = wg.shape[-1]
    tf = _pick(ff, (512, 256, 128))
    chunk = _pick(tm, (128, 64, 32, 16))
    kern = functools.partial(_ffn_kernel, alpha=alpha, chunk=chunk)
    grid_spec = pltpu.PrefetchScalarGridSpec(
        num_scalar_prefetch=1,
        grid=(rows // tm, ff // tf),
        in_specs=[
            pl.BlockSpec((tm, d), lambda i, f, t: (i, 0)),
            pl.BlockSpec((None, 3, SUBLANES, d), lambda i, f, t: (t[i], 0, 0, 0)),
            pl.BlockSpec((None, None, d, tf), lambda i, f, t: (li, ki, 0, f)),
            pl.BlockSpec((None, None, d, tf), lambda i, f, t: (li, ki, 0, f)),
            pl.BlockSpec((None, None, tf, d), lambda i, f, t: (li, ki, f, 0)),
            pl.BlockSpec((1, d), lambda i, f, t: (0, 0)),
            pl.BlockSpec((1, d), lambda i, f, t: (0, 0)),
        ],
        out_specs=pl.BlockSpec((tm, d), lambda i, f, t: (i, 0)),
        scratch_shapes=[pltpu.VMEM((tm, d), BF16)],
    )
    return pl.pallas_call(
        kern,
        grid_spec=grid_spec,
        out_shape=jax.ShapeDtypeStruct((rows, d), F32),
        compiler_params=_params(("arbitrary", "arbitrary")),
        name="ffn",
    )(tile_mod, x, mod, wg, wu, wd, lng.reshape(1, d), lnb.reshape(1, d))


def _qkv_kernel(tmod_ref, x_ref, mod_ref, w_ref, cs_ref, sa_ref, sb_ref, o_ref, u_ref, *, chunk,
                n_rot):
    j = pl.program_id(2)

    @pl.when(j == 0)
    def _init():
        _modulate_rows(x_ref, mod_ref, u_ref, chunk)

    heads_per_chunk = o_ref.shape[0]
    tm = u_ref.shape[0]
    group = min(tm, EPILOGUE_ROWS)

    @pl.when(j < n_rot)
    def _rotate():
        for r in range(0, tm, group):
            acc = jnp.dot(u_ref[r:r + group, :], w_ref[...], preferred_element_type=F32)
            cs, sa, sb = cs_ref[r:r + group, :], sa_ref[r:r + group, :], sb_ref[r:r + group, :]
            for hh in range(heads_per_chunk):
                blk = acc[:, hh * LANES:(hh + 1) * LANES]
                rot = (blk * cs + pltpu.roll(blk, LANES - ROT_HALF, 1) * sa
                       + pltpu.roll(blk, ROT_HALF, 1) * sb)
                o_ref[hh, r:r + group, :] = rot.astype(o_ref.dtype)

    @pl.when(j >= n_rot)
    def _plain():
        for r in range(0, tm, group):
            acc = jnp.dot(u_ref[r:r + group, :], w_ref[...], preferred_element_type=F32)
            for hh in range(heads_per_chunk):
                o_ref[hh, r:r + group, :] = acc[:, hh * LANES:(hh + 1) * LANES].astype(o_ref.dtype)


def _qkv(x, batch, tile_mod, mod, w, cs, sa, sb):
    rows, d = x.shape
    n = rows // batch
    heads = d // DA_V_DIM
    tm = _pick(n, (1024, 512, 256, 128))
    tn = _pick(d, (512, 256, 128))
    hpc = tn // LANES
    cpw = d // tn
    chunk = _pick(tm, (128, 64, 32, 16))
    nt = n // tm
    grid_spec = pltpu.PrefetchScalarGridSpec(
        num_scalar_prefetch=1,
        grid=(batch, nt, 3 * cpw),
        in_specs=[
            pl.BlockSpec((tm, d), lambda b, t, j, tmr: (b * nt + t, 0)),
            pl.BlockSpec((None, 3, SUBLANES, d), lambda b, t, j, tmr: (tmr[b * nt + t], 0, 0, 0)),
            pl.BlockSpec((d, tn), lambda b, t, j, tmr: (0, j)),
            pl.BlockSpec((None, tm, LANES), lambda b, t, j, tmr: (jnp.minimum(j // cpw, 1), t, 0)),
            pl.BlockSpec((None, tm, LANES), lambda b, t, j, tmr: (jnp.minimum(j // cpw, 1), t, 0)),
            pl.BlockSpec((None, tm, LANES), lambda b, t, j, tmr: (jnp.minimum(j // cpw, 1), t, 0)),
        ],
        out_specs=pl.BlockSpec((None, None, hpc, tm, LANES),
                               lambda b, t, j, tmr: (j // cpw, b, j % cpw, t, 0)),
        scratch_shapes=[pltpu.VMEM((tm, d), BF16)],
    )
    return pl.pallas_call(
        functools.partial(_qkv_kernel, chunk=chunk, n_rot=2 * cpw),
        grid_spec=grid_spec,
        out_shape=jax.ShapeDtypeStruct((3, batch, heads, n, LANES), BF16),
        compiler_params=_params(("arbitrary", "arbitrary", "arbitrary")),
        name="qkv",
    )(tile_mod, x, mod, w, cs, sa, sb)


def _rope_tables(n, rotate):
    scale_q = DA_HEAD_DIM ** -0.5 * math.log2(math.e)
    ones = jnp.ones((n, LANES), F32)
    zeros = jnp.zeros((n, LANES), F32)
    if not rotate:
        return (jnp.stack([scale_q * ones, ones]), jnp.stack([zeros] * 2), jnp.stack([zeros] * 2))
    rows = n // GRID_W
    row = jnp.repeat(jnp.arange(rows, dtype=F32), GRID_W)
    col = jnp.tile(jnp.arange(GRID_W, dtype=F32), rows)
    inv_freq = ROPE_BASE ** (-jnp.arange(0, AXIS_ROT_DIM, 2, dtype=F32) / AXIS_ROT_DIM)
    ang_r = row[:, None] * inv_freq[None, :]
    ang_c = col[:, None] * inv_freq[None, :]
    z = jnp.zeros_like(ang_r)
    cos64 = jnp.concatenate([jnp.cos(ang_r), jnp.cos(ang_r), jnp.cos(ang_c), jnp.cos(ang_c)], -1)
    sa64 = jnp.concatenate([-jnp.sin(ang_r), z, -jnp.sin(ang_c), z], -1)
    sb64 = jnp.concatenate([z, jnp.sin(ang_r), z, jnp.sin(ang_c)], -1)
    cs = jnp.tile(cos64, (1, 2))
    sa = jnp.tile(sa64, (1, 2))
    sb = jnp.tile(sb64, (1, 2))
    return (jnp.stack([scale_q * cs, cs]), jnp.stack([scale_q * sa, sa]),
            jnp.stack([scale_q * sb, sb]))


def _attn_kernel(*refs, n_parts, lambda_init, n_sub):
    q_ref = refs[0]
    k_refs = refs[1:1 + n_parts]
    v_refs = refs[1 + n_parts:1 + 2 * n_parts]
    lq1_ref, lk1_ref, lq2_ref, lk2_ref, g_ref, o_ref = refs[1 + 2 * n_parts:7 + 2 * n_parts]
    vt_refs = refs[7 + 2 * n_parts:7 + 3 * n_parts]
    kmax_ref = refs[7 + 3 * n_parts]
    lam = (jnp.exp(jnp.sum(lq1_ref[...] * lk1_ref[...], axis=-1, keepdims=True))
           - jnp.exp(jnp.sum(lq2_ref[...] * lk2_ref[...], axis=-1, keepdims=True)) + lambda_init)
    tq = q_ref.shape[0] // n_sub
    lane = lax.broadcasted_iota(jnp.int32, (tq, LANES), 1)
    sub_head = (lane < DA_HEAD_DIM, lane >= DA_HEAD_DIM)
    nt_dims = (((1,), (1,)), ((), ()))

    def masked_q(t):
        q = q_ref[t * tq:(t + 1) * tq, :]
        zero = jnp.zeros_like(q)
        return [jnp.where(mask, q, zero) for mask in sub_head]

    def finish(t, o):
        y = o * lax.rsqrt(jnp.mean(o * o, axis=-1, keepdims=True) + LN_EPS) * g_ref[...]
        o_ref[t * tq:(t + 1) * tq, :] = (y * (1.0 - lambda_init)).astype(o_ref.dtype)

    @pl.when(pl.program_id(2) == 0)
    def _per_head_setup():
        for v_ref, vt_ref in zip(v_refs, vt_refs):
            vt_ref[0:LANES, :] = jnp.transpose(v_ref[...].astype(F32)).astype(BF16)
            vt_ref[LANES:, :] = jnp.ones((vt_ref.shape[0] - LANES, vt_ref.shape[1]), BF16)
        k_abs_max = functools.reduce(jnp.maximum, [
            jnp.max(jnp.abs(k_ref[...].astype(F32)), axis=0, keepdims=True) for k_ref in k_refs])
        kmax_ref[...] = jnp.broadcast_to(k_abs_max, kmax_ref.shape)

    k_abs_max = kmax_ref[0:1, :]
    smallest_sum = None
    for t in range(n_sub):
        qs = masked_q(t)
        reach = jnp.abs(q_ref[t * tq:(t + 1) * tq, :].astype(F32)) * k_abs_max
        norm = []
        for e in range(2):
            bound = jnp.sum(jnp.where(sub_head[e], reach, 0.0), axis=-1,
                            keepdims=True) * ATTN_BOUND_SLACK
            shift = jnp.transpose(jnp.broadcast_to(bound, (tq, LANES)))[0:1, :]
            tot = None
            for k_ref, vt_ref in zip(k_refs, vt_refs):
                s_t = lax.dot_general(k_ref[...], qs[e], nt_dims, preferred_element_type=F32)
                p_t = jnp.exp2(s_t - shift).astype(BF16)
                part = jnp.dot(vt_ref[...], p_t, preferred_element_type=F32)
                tot = part if tot is None else tot + part
            norm.append(tot)
            low = jnp.min(tot[LANES:LANES + 1])
            smallest_sum = low if smallest_sum is None else jnp.minimum(smallest_sum, low)
        o_t = (norm[0][:LANES] * (1.0 / norm[0][LANES:LANES + 1])
               - norm[1][:LANES] * (lam / norm[1][LANES:LANES + 1]))
        finish(t, jnp.transpose(o_t))

    def exact_max():
        for t in range(n_sub):
            qs = masked_q(t)
            scores = [[lax.dot_general(qe, k_ref[...], nt_dims, preferred_element_type=F32)
                       for k_ref in k_refs] for qe in qs]
            weights = []
            for e in range(2):
                m = functools.reduce(jnp.maximum,
                                     [jnp.max(s, axis=-1, keepdims=True) for s in scores[e]])
                ps = [jnp.exp2(s - m) for s in scores[e]]
                l = functools.reduce(jnp.add, [jnp.sum(p, axis=-1, keepdims=True) for p in ps])
                weights.append((ps, 1.0 / l))
            o = None
            for p, v_ref in enumerate(v_refs):
                a = weights[0][0][p] * weights[0][1] - weights[1][0][p] * (lam * weights[1][1])
                part = jnp.dot(a.astype(BF16), v_ref[...], preferred_element_type=F32)
                o = part if o is None else o + part
            finish(t, o)

    pl.when(jnp.logical_not(smallest_sum >= ATTN_MIN_ROW_SUM))(exact_max)


def _attention(qkv, kv_parts, lam_params, subln_g, lambda_init):
    _, batch, heads, n, _ = qkv.shape
    tq = _pick(n, (ATTN_SUB_TILES * ATTN_SUB_ROWS, ATTN_SUB_ROWS, 256, 128))
    n_sub = max(tq // ATTN_SUB_ROWS, 1)
    in_specs = [pl.BlockSpec((None, None, None, tq, LANES), lambda b, h, t: (0, b, h, t, 0))]
    in_specs += [pl.BlockSpec((None, None, None, part.shape[3], LANES),
                              lambda b, h, t: (1, b, h, 0, 0)) for part in kv_parts]
    in_specs += [pl.BlockSpec((None, None, None, part.shape[3], LANES),
                              lambda b, h, t: (2, b, h, 0, 0)) for part in kv_parts]
    args = [qkv] + list(kv_parts) + list(kv_parts)
    for p in lam_params:
        in_specs.append(pl.BlockSpec((1, DA_HEAD_DIM), lambda b, h, t: (0, 0)))
        args.append(p.reshape(1, DA_HEAD_DIM))
    in_specs.append(pl.BlockSpec((1, DA_V_DIM), lambda b, h, t: (0, 0)))
    args.append(subln_g.reshape(1, DA_V_DIM))
    return pl.pallas_call(
        functools.partial(_attn_kernel, n_parts=len(kv_parts), lambda_init=lambda_init,
                          n_sub=n_sub),
        grid=(batch, heads, n // tq),
        in_specs=in_specs,
        out_specs=pl.BlockSpec((None, tq, LANES), lambda b, h, t: (b, t, h)),
        out_shape=jax.ShapeDtypeStruct((batch, n, heads * LANES), BF16),
        scratch_shapes=[pltpu.VMEM((LANES + ATTN_ONES_ROWS, part.shape[3]), BF16)
                        for part in kv_parts] + [pltpu.VMEM((SUBLANES, LANES), F32)],
        compiler_params=_params(("arbitrary", "arbitrary", "arbitrary")),
        name="attn",
    )(*args)


def _proj_ln_kernel(tmod_ref, h_ref, x_ref, mod_ref, w_ref, lng_ref, lnb_ref, o_ref, *, alpha, chunk):
    tm, d = x_ref.shape
    gate, lng, lnb = mod_ref[2], lng_ref[...], lnb_ref[...]
    for r in range(0, tm, chunk):
        hh = jnp.dot(h_ref[r:r + chunk, :], w_ref[...], preferred_element_type=F32)
        o_ref[r:r + chunk, :] = _residual_ln(x_ref[r:r + chunk, :], hh, gate, lng, lnb, alpha)


def _proj_ln(h, x, tile_mod, mod, w, lng, lnb, alpha, tm):
    rows, d = x.shape
    k = h.shape[1]
    chunk = _pick(tm, (128, 64, 32, 16))
    grid_spec = pltpu.PrefetchScalarGridSpec(
        num_scalar_prefetch=1,
        grid=(rows // tm,),
        in_specs=[
            pl.BlockSpec((tm, k), lambda i, t: (i, 0)),
            pl.BlockSpec((tm, d), lambda i, t: (i, 0)),
            pl.BlockSpec((None, 3, SUBLANES, d), lambda i, t: (t[i], 0, 0, 0)),
            pl.BlockSpec((k, d), lambda i, t: (0, 0)),
            pl.BlockSpec((1, d), lambda i, t: (0, 0)),
            pl.BlockSpec((1, d), lambda i, t: (0, 0)),
        ],
        out_specs=pl.BlockSpec((tm, d), lambda i, t: (i, 0)),
    )
    return pl.pallas_call(
        functools.partial(_proj_ln_kernel, alpha=alpha, chunk=chunk),
        grid_spec=grid_spec,
        out_shape=jax.ShapeDtypeStruct((rows, d), F32),
        compiler_params=_params(("arbitrary",)),
        name="proj_ln",
    )(tile_mod, h, x, mod, w, lng.reshape(1, d), lnb.reshape(1, d))


def _lru_in_kernel(tmod_ref, x_ref, mod_ref, w_ref, o_ref, u_ref, *, n_gelu, chunk):
    j = pl.program_id(1)

    @pl.when(j == 0)
    def _init():
        _modulate_rows(x_ref, mod_ref, u_ref, chunk)

    tm = u_ref.shape[0]
    group = min(tm, EPILOGUE_ROWS)

    @pl.when(j < n_gelu)
    def _gate_branch():
        for r in range(0, tm, group):
            acc = jnp.dot(u_ref[r:r + group, :], w_ref[...], preferred_element_type=F32)
            cdf = 0.5 * (1.0 + jnp.tanh(GELU_C0 * (acc + GELU_C1 * (acc * acc * acc))))
            o_ref[r:r + group, :] = acc * cdf

    @pl.when(j >= n_gelu)
    def _recurrent_branch():
        o_ref[...] = jnp.dot(u_ref[...], w_ref[...], preferred_element_type=F32)


def _lru_in(x, tile_mod, mod, w, tm):
    rows, d = x.shape
    n = w.shape[1]
    tn = _pick(n // 2, (512, 256, 128))
    chunk = _pick(tm, (128, 64, 32, 16))
    grid_spec = pltpu.PrefetchScalarGridSpec(
        num_scalar_prefetch=1,
        grid=(rows // tm, n // tn),
        in_specs=[
            pl.BlockSpec((tm, d), lambda i, j, t: (i, 0)),
            pl.BlockSpec((None, 3, SUBLANES, d), lambda i, j, t: (t[i], 0, 0, 0)),
            pl.BlockSpec((d, tn), lambda i, j, t: (0, j)),
        ],
        out_specs=pl.BlockSpec((tm, tn), lambda i, j, t: (i, j)),
        scratch_shapes=[pltpu.VMEM((tm, d), BF16)],
    )
    return pl.pallas_call(
        functools.partial(_lru_in_kernel, n_gelu=(n // 2) // tn, chunk=chunk),
        grid_spec=grid_spec,
        out_shape=jax.ShapeDtypeStruct((rows, n), F32),
        compiler_params=_params(("arbitrary", "arbitrary")),
        name="lru_in",
    )(tile_mod, x, mod, w)


def _lru_scan_kernel(*refs, reverse, combine, n_chunks, n_taps):
    if combine:
        (xr_ref, prev_ref, next_ref, cw_ref, cb_ref, wa_ref, ba_ref, wi_ref, bi_ref, ap_ref, h0_ref,
         hf_ref, g_ref, o_ref, hl_ref, a_s, b_s, carry) = refs
    else:
        (xr_ref, prev_ref, next_ref, cw_ref, cb_ref, wa_ref, ba_ref, wi_ref, bi_ref, ap_ref, h0_ref,
         o_ref, hl_ref, a_s, b_s, carry) = refs
    c = pl.program_id(1)
    c_eff = n_chunks - 1 - c if reverse else c
    rows = xr_ref.shape[0]
    n_blocks, block_w = wa_ref.shape[0], wa_ref.shape[1]
    steps = rows // SUBLANES

    @pl.when(c == 0)
    def _init():
        carry[...] = h0_ref[...]

    halo_l = jnp.where(c_eff > 0, prev_ref[...], 0.0)
    halo_r = jnp.where(c_eff < n_chunks - 1, next_ref[...], 0.0)
    ext = jnp.concatenate([halo_l, xr_ref[...], halo_r], axis=0)
    xc = cb_ref[...]
    for j in range(n_taps):
        xc = xc + ext[j * SUBLANES:j * SUBLANES + rows] * cw_ref[j:j + 1, :]
    xb = xc.astype(BF16)

    def block_diag(w_ref):
        return jnp.concatenate(
            [jnp.dot(xb[:, s * block_w:(s + 1) * block_w], w_ref[s], preferred_element_type=F32)
             for s in range(n_blocks)], axis=1)

    r = jax.nn.sigmoid(block_diag(wa_ref) + ba_ref[...])
    i = jax.nn.sigmoid(block_diag(wi_ref) + bi_ref[...])
    z = -ap_ref[...]
    softplus = jnp.maximum(z, 0.0) + jnp.log(1.0 + jnp.exp(-jnp.abs(z)))
    rate = (-LRU_C * math.log2(math.e)) * softplus
    a = jnp.exp2(r * rate)
    a_s[...] = a
    b_s[...] = jnp.sqrt(1.0 - a * a) * i * xc

    def step(t, h):
        tt = steps - 1 - t if reverse else t
        sl = pl.ds(pl.multiple_of(tt * SUBLANES, SUBLANES), SUBLANES)
        h = a_s[sl, :] * h + b_s[sl, :]
        b_s[sl, :] = h
        return h
    h_last = lax.fori_loop(0, steps, step, carry[...], unroll=8)
    carry[...] = h_last
    hl_ref[...] = h_last
    if combine:
        o_ref[...] = ((hf_ref[...] + b_s[...]) * g_ref[...]).astype(o_ref.dtype)
    else:
        o_ref[...] = b_s[...]


def _lru_scan(y, conv_w, conv_b, w_a, b_a, w_i, b_i, a_param, h0, direction, reverse, hf=None):
    rows = y.shape[0]
    d_rnn = y.shape[1] // 2
    n_taps = conv_w.shape[0]
    block_w = w_a.shape[-1]
    group = _pick(d_rnn // block_w, (LRU_BLOCKS_PER_STEP, 2, 1))
    width = group * block_w
    n_blk = d_rnn // width
    tokens = rows // SUBLANES
    tc = _pick(tokens, (LRU_CHUNK_TOKENS, 32, 16, 8))
    rc = tc * SUBLANES
    n_chunks = tokens // tc
    halo_l = CONV_PAD_LEFT * SUBLANES
    halo_r = (n_taps - 1 - CONV_PAD_LEFT) * SUBLANES
    combine = hf is not None
    last_r = rows // halo_r - 1

    def ce(c):
        return n_chunks - 1 - c if reverse else c

    vec = lambda: pl.BlockSpec((None, 1, width), lambda h, c: (direction, 0, h))
    mat = lambda: pl.BlockSpec((None, group, block_w, block_w), lambda h, c: (direction, h, 0, 0))
    in_specs = [
        pl.BlockSpec((rc, width), lambda h, c: (ce(c), n_blk + h)),
        pl.BlockSpec((halo_l, width),
                     lambda h, c: (jnp.maximum(ce(c) * (rc // halo_l) - 1, 0), n_blk + h)),
        pl.BlockSpec((halo_r, width),
                     lambda h, c: (jnp.minimum((ce(c) + 1) * (rc // halo_r), last_r), n_blk + h)),
        pl.BlockSpec((n_taps, width), lambda h, c: (0, h)),
        pl.BlockSpec((1, width), lambda h, c: (0, h)),
        mat(),
        vec(),
        mat(),
        vec(),
        vec(),
        pl.BlockSpec((SUBLANES, width), lambda h, c: (0, h)),
    ]
    args = [y, y, y, conv_w, conv_b.reshape(1, d_rnn), w_a, b_a.reshape(2, 1, d_rnn), w_i,
            b_i.reshape(2, 1, d_rnn), a_param.reshape(2, 1, d_rnn), h0]
    if combine:
        in_specs += [pl.BlockSpec((rc, width), lambda h, c: (ce(c), h)),
                     pl.BlockSpec((rc, width), lambda h, c: (ce(c), h))]
        args += [hf, y]
    out_dtype = BF16 if combine else F32
    return pl.pallas_call(
        functools.partial(_lru_scan_kernel, reverse=reverse, combine=combine, n_chunks=n_chunks,
                          n_taps=n_taps),
        grid=(n_blk, n_chunks),
        in_specs=in_specs,
        out_specs=[pl.BlockSpec((rc, width), lambda h, c: (ce(c), h)),
                   pl.BlockSpec((SUBLANES, width), lambda h, c: (0, h))],
        out_shape=[jax.ShapeDtypeStruct((rows, d_rnn), out_dtype),
                   jax.ShapeDtypeStruct((SUBLANES, d_rnn), F32)],
        scratch_shapes=[pltpu.VMEM((rc, width), F32), pltpu.VMEM((rc, width), F32),
                        pltpu.VMEM((SUBLANES, width), F32)],
        compiler_params=_params(("arbitrary", "arbitrary")),
        name="lru_scan",
    )(*args)


def _bcast_mod(m):
    return jnp.broadcast_to(m[..., None, :], m.shape[:-1] + (SUBLANES, m.shape[-1]))


def kernel(x, c, ctx, c_ctx, w_ada, b_ada, ln_g, ln_b, ffn_w_gate, ffn_w_up, ffn_w_down, attn_w_qkv, attn_w_o, attn_lambda_q1, attn_lambda_k1, attn_lambda_q2, attn_lambda_k2, attn_subln_g, lru_w_in, lru_conv_w, lru_conv_b, lru_w_a, lru_b_a, lru_w_i, lru_b_i, lru_a_param, lru_w_out):
    batch, seq, d = x.shape
    n_ctx = ctx.shape[1]
    depth = w_ada.shape[0]
    assert depth == 2 and batch == SUBLANES
    alpha = (2.0 * depth) ** 0.25

    ada_rows = 2 * SUBLANES
    cc = jnp.zeros((ada_rows, d), F32).at[:batch].set(c).at[batch].set(c_ctx)
    ada = _ada(cc, w_ada, b_ada).reshape(depth, ada_rows, 3, 3, d)

    tm_lat = _pick(seq, (512, 256, 128))
    tm_ctx = _pick(batch * n_ctx, (512, 256, 128))
    tf_lat = _pick(seq, (1024, 512, 256, 128))
    tf_ctx = _pick(batch * n_ctx, (1024, 512, 256, 128))
    zeros_lat = jnp.zeros((batch * seq // tm_lat,), jnp.int32)
    zeros_ctx = jnp.zeros((batch * n_ctx // tm_ctx,), jnp.int32)
    zeros_lat_f = jnp.zeros((batch * seq // tf_lat,), jnp.int32)
    zeros_ctx_f = jnp.zeros((batch * n_ctx // tf_ctx,), jnp.int32)
    bf = lambda w: w.astype(BF16)

    def batch_of_tile(tile):
        return jnp.arange(batch * seq // tile, dtype=jnp.int32) // (seq // tile)

    mod_b = _bcast_mod(ada[0, :batch])
    mod_c = _bcast_mod(ada[0, batch:batch + 1])
    tile_b = batch_of_tile(tm_lat)
    tile_bf = batch_of_tile(tf_lat)
    xl = x.reshape(batch * seq, d)
    xc = ctx.reshape(batch * n_ctx, d)

    ffn_w = (bf(ffn_w_gate), bf(ffn_w_up), bf(ffn_w_down))
    xl = _ffn(xl, tile_bf, mod_b[:, 0], ffn_w, (0, 0), ln_g[0, 0], ln_b[0, 0], alpha, tf_lat)
    xc = _ffn(xc, zeros_ctx_f, mod_c[:, 0], ffn_w, (0, 0), ln_g[0, 0], ln_b[0, 0], alpha, tf_ctx)

    lambda_init = 0.8 - 0.6 * math.exp(-0.3 * 0)
    w_qkv = bf(attn_w_qkv[0])
    tq_lat = _pick(seq, (1024, 512, 256, 128))
    tile_q = jnp.arange(batch * seq // tq_lat, dtype=jnp.int32) // (seq // tq_lat)
    qkv_l = _qkv(xl, batch, tile_q, mod_b[:, 1], w_qkv, *_rope_tables(seq, True))
    tq_ctx = _pick(n_ctx, (1024, 512, 256, 128))
    qkv_c = _qkv(xc, batch, jnp.zeros((batch * n_ctx // tq_ctx,), jnp.int32), mod_c[:, 1], w_qkv,
                 *_rope_tables(n_ctx, False))
    lam_params = (attn_lambda_q1[0], attn_lambda_k1[0], attn_lambda_q2[0], attn_lambda_k2[0])
    o_l = _attention(qkv_l, (qkv_c, qkv_l), lam_params, attn_subln_g[0], lambda_init)
    o_c = _attention(qkv_c, (qkv_c,), lam_params, attn_subln_g[0], lambda_init)
    w_o = bf(attn_w_o[0])
    xl = _proj_ln(o_l.reshape(batch * seq, d), xl, tile_b, mod_b[:, 1], w_o, ln_g[0, 1], ln_b[0, 1],
                  alpha, tm_lat)
    xc = _proj_ln(o_c.reshape(batch * n_ctx, d), xc, zeros_ctx, mod_c[:, 1], w_o, ln_g[0, 1],
                  ln_b[0, 1], alpha, tm_ctx)

    xl = _ffn(xl, tile_bf, mod_b[:, 2], ffn_w, (0, 1), ln_g[0, 2], ln_b[0, 2], alpha, tf_lat)
    xc = _ffn(xc, zeros_ctx_f, mod_c[:, 2], ffn_w, (0, 1), ln_g[0, 2], ln_b[0, 2], alpha, tf_ctx)

    xl = xl.reshape(batch, seq, d).transpose(1, 0, 2).reshape(seq * batch, d)
    xc = xc.reshape(batch, n_ctx, d).transpose(1, 0, 2).reshape(n_ctx * batch, d)
    mod_p = ada[1, :batch].transpose(1, 2, 0, 3)[None]
    mod_c = _bcast_mod(ada[1, batch:batch + 1])

    xl = _ffn(xl, zeros_lat_f, mod_p[:, 0], ffn_w, (1, 0), ln_g[1, 0], ln_b[1, 0], alpha, tf_lat)
    xc = _ffn(xc, zeros_ctx_f, mod_c[:, 0], ffn_w, (1, 0), ln_g[1, 0], ln_b[1, 0], alpha, tf_ctx)

    w_in = bf(lru_w_in[0])
    y_l = _lru_in(xl, zeros_lat_f, mod_p[:, 1], w_in, tf_lat)
    y_c = _lru_in(xc, zeros_ctx_f, mod_c[:, 1], w_in, tf_ctx)
    d_rnn = w_in.shape[1] // 2
    w_a, w_i = bf(lru_w_a[0]), bf(lru_w_i[0])
    scan = functools.partial(_lru_scan, conv_w=lru_conv_w[0], conv_b=lru_conv_b[0], w_a=w_a,
                             b_a=lru_b_a[0], w_i=w_i, b_i=lru_b_i[0], a_param=lru_a_param[0])
    h_zero = jnp.zeros((SUBLANES, d_rnn), F32)
    _, h0_f = scan(y_c, h0=h_zero, direction=0, reverse=False)
    _, h0_b = scan(y_c, h0=h_zero, direction=1, reverse=True)
    hf, _ = scan(y_l, h0=h0_f, direction=0, reverse=False)
    hg, _ = scan(y_l, h0=h0_b, direction=1, reverse=True, hf=hf)
    xl = _proj_ln(hg, xl, zeros_lat, mod_p[:, 1], bf(lru_w_out[0]), ln_g[1, 1], ln_b[1, 1], alpha,
                  tm_lat)

    xl = _ffn(xl, zeros_lat_f, mod_p[:, 2], ffn_w, (1, 1), ln_g[1, 2], ln_b[1, 2], alpha, tf_lat)
    return xl.reshape(seq, batch, d).transpose(1, 0, 2)
```

```python
import functools
import math

import jax
import jax.numpy as jnp
from jax import lax
from jax.experimental import pallas as pl
from jax.experimental.pallas import tpu as pltpu

F32 = jnp.float32
BF16 = jnp.bfloat16

SUBLANES = 8
LANES = 128
VMEM_LIMIT_BYTES = 56 * 1024 * 1024
MATMUL_COLS = 512
LRU_BLOCKS_PER_STEP = 5
LRU_CHUNK_TOKENS = 64
EPILOGUE_ROWS = 256
ATTN_SUB_ROWS = 512
ATTN_SUB_TILES = 2
ATTN_MIN_ROW_SUM = 2.0 ** -80
ATTN_BOUND_SLACK = 1.001
ATTN_ONES_ROWS = 16

GRID_W = 64
DA_HEAD_DIM = 64
DA_V_DIM = 2 * DA_HEAD_DIM
ROPE_BASE = 10000.0
AXIS_ROT_DIM = DA_HEAD_DIM // 2
ROT_HALF = AXIS_ROT_DIM // 2
CONV_PAD_LEFT = 2
LRU_C = 8.0
FFN_RES_WEIGHT = 0.5
LN_EPS = 1e-6
GELU_C0 = math.sqrt(2.0 / math.pi)
GELU_C1 = 0.044715


def _params(semantics):
    return pltpu.CompilerParams(dimension_semantics=semantics, vmem_limit_bytes=VMEM_LIMIT_BYTES)


def _pick(n, prefs):
    for p in prefs:
        if n % p == 0:
            return p
    return n


def _column_chunks(w, tn=None):
    *lead, k, n = w.shape
    tn = tn or _pick(n, (MATMUL_COLS, 256, 128))
    return jnp.moveaxis(w.astype(BF16).reshape(*lead, k, n // tn, tn), -2, -3)


def _layer_norm(y, g, b):
    mu = jnp.mean(y, axis=-1, keepdims=True)
    yc = y - mu
    var = jnp.mean(yc * yc, axis=-1, keepdims=True)
    return yc * lax.rsqrt(var + LN_EPS) * g + b


def _row_chunks(n_rows, chunk, fn):
    def body(r, carry):
        fn(pl.ds(pl.multiple_of(r * chunk, chunk), chunk))
        return carry
    lax.fori_loop(0, n_rows // chunk, body, 0)


def _modulate_rows(x_ref, mod_ref, u_ref, chunk):
    tm, d = x_ref.shape
    shift = mod_ref[0]
    scale1 = 1.0 + mod_ref[1]

    def fn(rows):
        xx = x_ref[rows, :].reshape(chunk // SUBLANES, SUBLANES, d)
        u_ref[rows, :] = (xx * scale1 + shift).reshape(chunk, d).astype(u_ref.dtype)
    _row_chunks(tm, chunk, fn)


def _residual_ln(x, h, gate, lng, lnb, alpha):
    rows, d = x.shape
    y = (alpha * x.reshape(rows // SUBLANES, SUBLANES, d)
         + gate * h.reshape(rows // SUBLANES, SUBLANES, d))
    return _layer_norm(y, lng, lnb).reshape(rows, d)


def _ada_kernel(c_ref, w_ref, b_ref, o_ref):
    c = c_ref[...]
    s = c * jax.nn.sigmoid(c)
    o_ref[...] = jnp.dot(s.astype(BF16), w_ref[...].astype(BF16),
                         preferred_element_type=F32) + b_ref[...]


def _ada(cc, w_ada, b_ada):
    depth, d, n = w_ada.shape
    rows = cc.shape[0]
    tn = _pick(n, (1024, 512, 256, 128))
    return pl.pallas_call(
        _ada_kernel,
        grid=(depth, n // tn),
        in_specs=[
            pl.BlockSpec((rows, d), lambda l, j: (0, 0)),
            pl.BlockSpec((None, d, tn), lambda l, j: (l, 0, j)),
            pl.BlockSpec((None, 1, tn), lambda l, j: (l, 0, j)),
        ],
        out_specs=pl.BlockSpec((None, rows, tn), lambda l, j: (l, 0, j)),
        out_shape=jax.ShapeDtypeStruct((depth, rows, n), F32),
        compiler_params=_params(("arbitrary", "arbitrary")),
        name="ada",
    )(cc, w_ada, b_ada.reshape(depth, 1, n))


def _ffn_kernel(tmod_ref, x_ref, mod_ref, wg_ref, wu_ref, wd_ref, lng_ref, lnb_ref, o_ref, u_ref,
                *, alpha, chunk):
    f = pl.program_id(1)
    tm, d = x_ref.shape

    @pl.when(f == 0)
    def _init():
        _modulate_rows(x_ref, mod_ref, u_ref, chunk)
        o_ref[...] = jnp.zeros((tm, d), F32)

    def swiglu(rows):
        u = u_ref[rows, :]
        ga = jnp.dot(u, wg_ref[...], preferred_element_type=F32)
        up = jnp.dot(u, wu_ref[...], preferred_element_type=F32)
        h = (ga * jax.nn.sigmoid(ga) * up).astype(BF16)
        return jnp.dot(h, wd_ref[...], preferred_element_type=F32)

    last = pl.num_programs(1) - 1

    @pl.when(f < last)
    def _accumulate():
        o_ref[...] += swiglu(slice(None))

    @pl.when(f == last)
    def _finish():
        gate, lng, lnb = FFN_RES_WEIGHT * mod_ref[2], lng_ref[...], lnb_ref[...]
        group = min(tm, EPILOGUE_ROWS)
        for r in range(0, tm, group):
            rows = slice(r, r + group)
            o_ref[rows, :] = _residual_ln(x_ref[rows, :], o_ref[rows, :] + swiglu(rows), gate,
                                          lng, lnb, alpha)


def _ffn(x, tile_mod, mod, weights, which, lng, lnb, alpha, tm):
    rows, d = x.shape
    wg, wu, wd = weights
    li, ki = which
    ff, tf = wd.shape[-2], wg.shape[-1]
    chunk = _pick(tm, (128, 64, 32, 16))
    kern = functools.partial(_ffn_kernel, alpha=alpha, chunk=chunk)
    grid_spec = pltpu.PrefetchScalarGridSpec(
        num_scalar_prefetch=1,
        grid=(rows // tm, ff // tf),
        in_specs=[
            pl.BlockSpec((tm, d), lambda i, f, t: (i, 0)),
            pl.BlockSpec((None, 3, SUBLANES, d), lambda i, f, t: (t[i], 0, 0, 0)),
            pl.BlockSpec((None, None, None, d, tf), lambda i, f, t: (li, ki, f, 0, 0)),
            pl.BlockSpec((None, None, None, d, tf), lambda i, f, t: (li, ki, f, 0, 0)),
            pl.BlockSpec((None, None, tf, d), lambda i, f, t: (li, ki, f, 0)),
            pl.BlockSpec((1, d), lambda i, f, t: (0, 0)),
            pl.BlockSpec((1, d), lambda i, f, t: (0, 0)),
        ],
        out_specs=pl.BlockSpec((tm, d), lambda i, f, t: (i, 0)),
        scratch_shapes=[pltpu.VMEM((tm, d), BF16)],
    )
    return pl.pallas_call(
        kern,
        grid_spec=grid_spec,
        out_shape=jax.ShapeDtypeStruct((rows, d), F32),
        compiler_params=_params(("arbitrary", "arbitrary")),
        name="ffn",
    )(tile_mod, x, mod, wg, wu, wd, lng.reshape(1, d), lnb.reshape(1, d))


def _qkv_kernel(tmod_ref, x_ref, mod_ref, w_ref, cs_ref, sa_ref, sb_ref, o_ref, u_ref, *, chunk,
                n_rot):
    j = pl.program_id(2)

    @pl.when(j == 0)
    def _init():
        _modulate_rows(x_ref, mod_ref, u_ref, chunk)

    heads_per_chunk = o_ref.shape[0]
    tm = u_ref.shape[0]
    group = min(tm, EPILOGUE_ROWS)

    @pl.when(j < n_rot)
    def _rotate():
        for r in range(0, tm, group):
            acc = jnp.dot(u_ref[r:r + group, :], w_ref[...], preferred_element_type=F32)
            cs, sa, sb = cs_ref[r:r + group, :], sa_ref[r:r + group, :], sb_ref[r:r + group, :]
            for hh in range(heads_per_chunk):
                blk = acc[:, hh * LANES:(hh + 1) * LANES]
                rot = (blk * cs + pltpu.roll(blk, LANES - ROT_HALF, 1) * sa
                       + pltpu.roll(blk, ROT_HALF, 1) * sb)
                o_ref[hh, r:r + group, :] = rot.astype(o_ref.dtype)

    @pl.when(j >= n_rot)
    def _plain():
        for r in range(0, tm, group):
            acc = jnp.dot(u_ref[r:r + group, :], w_ref[...], preferred_element_type=F32)
            for hh in range(heads_per_chunk):
                o_ref[hh, r:r + group, :] = acc[:, hh * LANES:(hh + 1) * LANES].astype(o_ref.dtype)


def _qkv(x, batch, tile_mod, mod, w, cs, sa, sb):
    rows, d = x.shape
    n = rows // batch
    heads = d // DA_V_DIM
    tm = _pick(n, (1024, 512, 256, 128))
    tn = w.shape[-1]
    hpc = tn // LANES
    cpw = d // tn
    chunk = _pick(tm, (128, 64, 32, 16))
    nt = n // tm
    grid_spec = pltpu.PrefetchScalarGridSpec(
        num_scalar_prefetch=1,
        grid=(batch, nt, 3 * cpw),
        in_specs=[
            pl.BlockSpec((tm, d), lambda b, t, j, tmr: (b * nt + t, 0)),
            pl.BlockSpec((None, 3, SUBLANES, d), lambda b, t, j, tmr: (tmr[b * nt + t], 0, 0, 0)),
            pl.BlockSpec((None, d, tn), lambda b, t, j, tmr: (j, 0, 0)),
            pl.BlockSpec((None, tm, LANES), lambda b, t, j, tmr: (jnp.minimum(j // cpw, 1), t, 0)),
            pl.BlockSpec((None, tm, LANES), lambda b, t, j, tmr: (jnp.minimum(j // cpw, 1), t, 0)),
            pl.BlockSpec((None, tm, LANES), lambda b, t, j, tmr: (jnp.minimum(j // cpw, 1), t, 0)),
        ],
        out_specs=pl.BlockSpec((None, None, hpc, tm, LANES),
                               lambda b, t, j, tmr: (j // cpw, b, j % cpw, t, 0)),
        scratch_shapes=[pltpu.VMEM((tm, d), BF16)],
    )
    return pl.pallas_call(
        functools.partial(_qkv_kernel, chunk=chunk, n_rot=2 * cpw),
        grid_spec=grid_spec,
        out_shape=jax.ShapeDtypeStruct((3, batch, heads, n, LANES), BF16),
        compiler_params=_params(("arbitrary", "arbitrary", "arbitrary")),
        name="qkv",
    )(tile_mod, x, mod, w, cs, sa, sb)


def _rope_tables(n, rotate):
    scale_q = DA_HEAD_DIM ** -0.5 * math.log2(math.e)
    ones = jnp.ones((n, LANES), F32)
    zeros = jnp.zeros((n, LANES), F32)
    if not rotate:
        return (jnp.stack([scale_q * ones, ones]), jnp.stack([zeros] * 2), jnp.stack([zeros] * 2))
    rows = n // GRID_W
    row = jnp.repeat(jnp.arange(rows, dtype=F32), GRID_W)
    col = jnp.tile(jnp.arange(GRID_W, dtype=F32), rows)
    inv_freq = ROPE_BASE ** (-jnp.arange(0, AXIS_ROT_DIM, 2, dtype=F32) / AXIS_ROT_DIM)
    ang_r = row[:, None] * inv_freq[None, :]
    ang_c = col[:, None] * inv_freq[None, :]
    z = jnp.zeros_like(ang_r)
    cos64 = jnp.concatenate([jnp.cos(ang_r), jnp.cos(ang_r), jnp.cos(ang_c), jnp.cos(ang_c)], -1)
    sa64 = jnp.concatenate([-jnp.sin(ang_r), z, -jnp.sin(ang_c), z], -1)
    sb64 = jnp.concatenate([z, jnp.sin(ang_r), z, jnp.sin(ang_c)], -1)
    cs = jnp.tile(cos64, (1, 2))
    sa = jnp.tile(sa64, (1, 2))
    sb = jnp.tile(sb64, (1, 2))
    return (jnp.stack([scale_q * cs, cs]), jnp.stack([scale_q * sa, sa]),
            jnp.stack([scale_q * sb, sb]))


def _attn_kernel(*refs, n_parts, lambda_init, n_sub):
    q_ref = refs[0]
    k_refs = refs[1:1 + n_parts]
    v_refs = refs[1 + n_parts:1 + 2 * n_parts]
    lq1_ref, lk1_ref, lq2_ref, lk2_ref, g_ref, o_ref = refs[1 + 2 * n_parts:7 + 2 * n_parts]
    vt_refs = refs[7 + 2 * n_parts:7 + 3 * n_parts]
    kmax_ref = refs[7 + 3 * n_parts]
    lam = (jnp.exp(jnp.sum(lq1_ref[...] * lk1_ref[...], axis=-1, keepdims=True))
           - jnp.exp(jnp.sum(lq2_ref[...] * lk2_ref[...], axis=-1, keepdims=True)) + lambda_init)
    tq = q_ref.shape[0] // n_sub
    lane = lax.broadcasted_iota(jnp.int32, (tq, LANES), 1)
    sub_head = (lane < DA_HEAD_DIM, lane >= DA_HEAD_DIM)
    nt_dims = (((1,), (1,)), ((), ()))

    def masked_q(t):
        q = q_ref[t * tq:(t + 1) * tq, :]
        zero = jnp.zeros_like(q)
        return [jnp.where(mask, q, zero) for mask in sub_head]

    def finish(t, o):
        y = o * lax.rsqrt(jnp.mean(o * o, axis=-1, keepdims=True) + LN_EPS) * g_ref[...]
        o_ref[t * tq:(t + 1) * tq, :] = (y * (1.0 - lambda_init)).astype(o_ref.dtype)

    @pl.when(pl.program_id(2) == 0)
    def _per_head_setup():
        for v_ref, vt_ref in zip(v_refs, vt_refs):
            vt_ref[0:LANES, :] = jnp.transpose(v_ref[...].astype(F32)).astype(BF16)
            vt_ref[LANES:, :] = jnp.ones((vt_ref.shape[0] - LANES, vt_ref.shape[1]), BF16)
        k_abs_max = functools.reduce(jnp.maximum, [
            jnp.max(jnp.abs(k_ref[...].astype(F32)), axis=0, keepdims=True) for k_ref in k_refs])
        kmax_ref[...] = jnp.broadcast_to(k_abs_max, kmax_ref.shape)

    k_abs_max = kmax_ref[0:1, :]
    smallest_sum = None
    for t in range(n_sub):
        qs = masked_q(t)
        reach = jnp.abs(q_ref[t * tq:(t + 1) * tq, :].astype(F32)) * k_abs_max
        norm = []
        for e in range(2):
            bound = jnp.sum(jnp.where(sub_head[e], reach, 0.0), axis=-1,
                            keepdims=True) * ATTN_BOUND_SLACK
            shift = jnp.transpose(jnp.broadcast_to(bound, (tq, LANES)))[0:1, :]
            tot = None
            for k_ref, vt_ref in zip(k_refs, vt_refs):
                s_t = lax.dot_general(k_ref[...], qs[e], nt_dims, preferred_element_type=F32)
                p_t = jnp.exp2(s_t - shift).astype(BF16)
                part = jnp.dot(vt_ref[...], p_t, preferred_element_type=F32)
                tot = part if tot is None else tot + part
            norm.append(tot)
            low = jnp.min(tot[LANES:LANES + 1])
            smallest_sum = low if smallest_sum is None else jnp.minimum(smallest_sum, low)
        o_t = (norm[0][:LANES] * (1.0 / norm[0][LANES:LANES + 1])
               - norm[1][:LANES] * (lam / norm[1][LANES:LANES + 1]))
        finish(t, jnp.transpose(o_t))

    def exact_max():
        for t in range(n_sub):
            qs = masked_q(t)
            scores = [[lax.dot_general(qe, k_ref[...], nt_dims, preferred_element_type=F32)
                       for k_ref in k_refs] for qe in qs]
            weights = []
            for e in range(2):
                m = functools.reduce(jnp.maximum,
                                     [jnp.max(s, axis=-1, keepdims=True) for s in scores[e]])
                ps = [jnp.exp2(s - m) for s in scores[e]]
                l = functools.reduce(jnp.add, [jnp.sum(p, axis=-1, keepdims=True) for p in ps])
                weights.append((ps, 1.0 / l))
            o = None
            for p, v_ref in enumerate(v_refs):
                a = weights[0][0][p] * weights[0][1] - weights[1][0][p] * (lam * weights[1][1])
                part = jnp.dot(a.astype(BF16), v_ref[...], preferred_element_type=F32)
                o = part if o is None else o + part
            finish(t, o)

    pl.when(jnp.logical_not(smallest_sum >= ATTN_MIN_ROW_SUM))(exact_max)


def _attention(qkv, kv_parts, lam_params, subln_g, lambda_init):
    _, batch, heads, n, _ = qkv.shape
    tq = _pick(n, (ATTN_SUB_TILES * ATTN_SUB_ROWS, ATTN_SUB_ROWS, 256, 128))
    n_sub = max(tq // ATTN_SUB_ROWS, 1)
    in_specs = [pl.BlockSpec((None, None, None, tq, LANES), lambda b, h, t: (0, b, h, t, 0))]
    in_specs += [pl.BlockSpec((None, None, None, part.shape[3], LANES),
                              lambda b, h, t: (1, b, h, 0, 0)) for part in kv_parts]
    in_specs += [pl.BlockSpec((None, None, None, part.shape[3], LANES),
                              lambda b, h, t: (2, b, h, 0, 0)) for part in kv_parts]
    args = [qkv] + list(kv_parts) + list(kv_parts)
    for p in lam_params:
        in_specs.append(pl.BlockSpec((1, DA_HEAD_DIM), lambda b, h, t: (0, 0)))
        args.append(p.reshape(1, DA_HEAD_DIM))
    in_specs.append(pl.BlockSpec((1, DA_V_DIM), lambda b, h, t: (0, 0)))
    args.append(subln_g.reshape(1, DA_V_DIM))
    return pl.pallas_call(
        functools.partial(_attn_kernel, n_parts=len(kv_parts), lambda_init=lambda_init,
                          n_sub=n_sub),
        grid=(batch, heads, n // tq),
        in_specs=in_specs,
        out_specs=pl.BlockSpec((None, tq, LANES), lambda b, h, t: (b, t, h)),
        out_shape=jax.ShapeDtypeStruct((batch, n, heads * LANES), BF16),
        scratch_shapes=[pltpu.VMEM((LANES + ATTN_ONES_ROWS, part.shape[3]), BF16)
                        for part in kv_parts] + [pltpu.VMEM((SUBLANES, LANES), F32)],
        compiler_params=_params(("arbitrary", "arbitrary", "arbitrary")),
        name="attn",
    )(*args)


def _proj_ln_kernel(tmod_ref, h_ref, x_ref, mod_ref, w_ref, lng_ref, lnb_ref, o_ref, *, alpha, chunk):
    tm, d = x_ref.shape
    gate, lng, lnb = mod_ref[2], lng_ref[...], lnb_ref[...]
    for r in range(0, tm, chunk):
        hh = jnp.dot(h_ref[r:r + chunk, :], w_ref[...], preferred_element_type=F32)
        o_ref[r:r + chunk, :] = _residual_ln(x_ref[r:r + chunk, :], hh, gate, lng, lnb, alpha)


def _proj_ln(h, x, tile_mod, mod, w, lng, lnb, alpha, tm):
    rows, d = x.shape
    k = h.shape[1]
    chunk = _pick(tm, (128, 64, 32, 16))
    grid_spec = pltpu.PrefetchScalarGridSpec(
        num_scalar_prefetch=1,
        grid=(rows // tm,),
        in_specs=[
            pl.BlockSpec((tm, k), lambda i, t: (i, 0)),
            pl.BlockSpec((tm, d), lambda i, t: (i, 0)),
            pl.BlockSpec((None, 3, SUBLANES, d), lambda i, t: (t[i], 0, 0, 0)),
            pl.BlockSpec((k, d), lambda i, t: (0, 0)),
            pl.BlockSpec((1, d), lambda i, t: (0, 0)),
            pl.BlockSpec((1, d), lambda i, t: (0, 0)),
        ],
        out_specs=pl.BlockSpec((tm, d), lambda i, t: (i, 0)),
    )
    return pl.pallas_call(
        functools.partial(_proj_ln_kernel, alpha=alpha, chunk=chunk),
        grid_spec=grid_spec,
        out_shape=jax.ShapeDtypeStruct((rows, d), F32),
        compiler_params=_params(("arbitrary",)),
        name="proj_ln",
    )(tile_mod, h, x, mod, w, lng.reshape(1, d), lnb.reshape(1, d))


def _lru_in_kernel(tmod_ref, x_ref, mod_ref, w_ref, o_ref, u_ref, *, n_gelu, chunk):
    j = pl.program_id(1)

    @pl.when(j == 0)
    def _init():
        _modulate_rows(x_ref, mod_ref, u_ref, chunk)

    tm = u_ref.shape[0]
    group = min(tm, EPILOGUE_ROWS)

    @pl.when(j < n_gelu)
    def _gate_branch():
        for r in range(0, tm, group):
            acc = jnp.dot(u_ref[r:r + group, :], w_ref[...], preferred_element_type=F32)
            cdf = 0.5 * (1.0 + jnp.tanh(GELU_C0 * (acc + GELU_C1 * (acc * acc * acc))))
            o_ref[r:r + group, :] = acc * cdf

    @pl.when(j >= n_gelu)
    def _recurrent_branch():
        o_ref[...] = jnp.dot(u_ref[...], w_ref[...], preferred_element_type=F32)


def _lru_in(x, tile_mod, mod, w, tm):
    rows, d = x.shape
    tn = w.shape[-1]
    n = w.shape[0] * tn
    chunk = _pick(tm, (128, 64, 32, 16))
    grid_spec = pltpu.PrefetchScalarGridSpec(
        num_scalar_prefetch=1,
        grid=(rows // tm, n // tn),
        in_specs=[
            pl.BlockSpec((tm, d), lambda i, j, t: (i, 0)),
            pl.BlockSpec((None, 3, SUBLANES, d), lambda i, j, t: (t[i], 0, 0, 0)),
            pl.BlockSpec((None, d, tn), lambda i, j, t: (j, 0, 0)),
        ],
        out_specs=pl.BlockSpec((None, tm, tn), lambda i, j, t: (j, i, 0)),
        scratch_shapes=[pltpu.VMEM((tm, d), BF16)],
    )
    return pl.pallas_call(
        functools.partial(_lru_in_kernel, n_gelu=(n // 2) // tn, chunk=chunk),
        grid_spec=grid_spec,
        out_shape=jax.ShapeDtypeStruct((n // tn, rows, tn), F32),
        compiler_params=_params(("arbitrary", "arbitrary")),
        name="lru_in",
    )(tile_mod, x, mod, w)


def _lru_scan_kernel(*refs, reverse, combine, n_chunks, n_taps):
    if combine:
        (xr_ref, prev_ref, next_ref, cw_ref, cb_ref, wa_ref, ba_ref, wi_ref, bi_ref, ap_ref, h0_ref,
         hf_ref, g_ref, o_ref, hl_ref, a_s, b_s, carry) = refs
    else:
        (xr_ref, prev_ref, next_ref, cw_ref, cb_ref, wa_ref, ba_ref, wi_ref, bi_ref, ap_ref, h0_ref,
         o_ref, hl_ref, a_s, b_s, carry) = refs
    c = pl.program_id(1)
    c_eff = n_chunks - 1 - c if reverse else c
    rows = xr_ref.shape[0]
    n_blocks, block_w = wa_ref.shape[0], wa_ref.shape[1]
    steps = rows // SUBLANES

    @pl.when(c == 0)
    def _init():
        carry[...] = h0_ref[...]

    halo_l = jnp.where(c_eff > 0, prev_ref[...], 0.0)
    halo_r = jnp.where(c_eff < n_chunks - 1, next_ref[...], 0.0)
    ext = jnp.concatenate([halo_l, xr_ref[...], halo_r], axis=0)
    xc = cb_ref[...]
    for j in range(n_taps):
        xc = xc + ext[j * SUBLANES:j * SUBLANES + rows] * cw_ref[j:j + 1, :]
    xb = xc.astype(BF16)

    def block_diag(w_ref):
        return jnp.concatenate(
            [jnp.dot(xb[:, s * block_w:(s + 1) * block_w], w_ref[s], preferred_element_type=F32)
             for s in range(n_blocks)], axis=1)

    r = jax.nn.sigmoid(block_diag(wa_ref) + ba_ref[...])
    i = jax.nn.sigmoid(block_diag(wi_ref) + bi_ref[...])
    z = -ap_ref[...]
    softplus = jnp.maximum(z, 0.0) + jnp.log(1.0 + jnp.exp(-jnp.abs(z)))
    rate = (-LRU_C * math.log2(math.e)) * softplus
    a = jnp.exp2(r * rate)
    a_s[...] = a
    b_s[...] = jnp.sqrt(1.0 - a * a) * i * xc

    def step(t, h):
        tt = steps - 1 - t if reverse else t
        sl = pl.ds(pl.multiple_of(tt * SUBLANES, SUBLANES), SUBLANES)
        h = a_s[sl, :] * h + b_s[sl, :]
        b_s[sl, :] = h
        return h
    h_last = lax.fori_loop(0, steps, step, carry[...], unroll=8)
    carry[...] = h_last
    hl_ref[...] = h_last
    if combine:
        o_ref[...] = ((hf_ref[...] + b_s[...]) * g_ref[...]).astype(o_ref.dtype)
    else:
        o_ref[...] = b_s[...]


def _lru_scan(y, conv_w, conv_b, w_a, b_a, w_i, b_i, a_param, h0, direction, reverse, hf=None):
    n_blk, rows, width = y.shape[0] // 2, y.shape[1], y.shape[2]
    d_rnn = n_blk * width
    n_taps = conv_w.shape[0]
    block_w = w_a.shape[-1]
    group = width // block_w
    tokens = rows // SUBLANES
    tc = _pick(tokens, (LRU_CHUNK_TOKENS, 32, 16, 8))
    rc = tc * SUBLANES
    n_chunks = tokens // tc
    halo_l = CONV_PAD_LEFT * SUBLANES
    halo_r = (n_taps - 1 - CONV_PAD_LEFT) * SUBLANES
    combine = hf is not None
    last_r = rows // halo_r - 1

    def ce(c):
        return n_chunks - 1 - c if reverse else c

    vec = lambda: pl.BlockSpec((None, 1, width), lambda h, c: (direction, 0, h))
    mat = lambda: pl.BlockSpec((None, group, block_w, block_w), lambda h, c: (direction, h, 0, 0))
    in_specs = [
        pl.BlockSpec((None, rc, width), lambda h, c: (n_blk + h, ce(c), 0)),
        pl.BlockSpec((None, halo_l, width),
                     lambda h, c: (n_blk + h, jnp.maximum(ce(c) * (rc // halo_l) - 1, 0), 0)),
        pl.BlockSpec((None, halo_r, width),
                     lambda h, c: (n_blk + h, jnp.minimum((ce(c) + 1) * (rc // halo_r), last_r), 0)),
        pl.BlockSpec((n_taps, width), lambda h, c: (0, h)),
        pl.BlockSpec((1, width), lambda h, c: (0, h)),
        mat(),
        vec(),
        mat(),
        vec(),
        vec(),
        pl.BlockSpec((SUBLANES, width), lambda h, c: (0, h)),
    ]
    args = [y, y, y, conv_w, conv_b.reshape(1, d_rnn), w_a, b_a.reshape(2, 1, d_rnn), w_i,
            b_i.reshape(2, 1, d_rnn), a_param.reshape(2, 1, d_rnn), h0]
    if combine:
        in_specs += [pl.BlockSpec((None, rc, width), lambda h, c: (h, ce(c), 0)),
                     pl.BlockSpec((None, rc, width), lambda h, c: (h, ce(c), 0))]
        args += [hf, y]
    if combine:
        seq_spec = pl.BlockSpec((rc, width), lambda h, c: (ce(c), h))
        seq_shape = jax.ShapeDtypeStruct((rows, d_rnn), BF16)
    else:
        seq_spec = pl.BlockSpec((None, rc, width), lambda h, c: (h, ce(c), 0))
        seq_shape = jax.ShapeDtypeStruct((n_blk, rows, width), F32)
    return pl.pallas_call(
        functools.partial(_lru_scan_kernel, reverse=reverse, combine=combine, n_chunks=n_chunks,
                          n_taps=n_taps),
        grid=(n_blk, n_chunks),
        in_specs=in_specs,
        out_specs=[seq_spec, pl.BlockSpec((SUBLANES, width), lambda h, c: (0, h))],
        out_shape=[seq_shape, jax.ShapeDtypeStruct((SUBLANES, d_rnn), F32)],
        scratch_shapes=[pltpu.VMEM((rc, width), F32), pltpu.VMEM((rc, width), F32),
                        pltpu.VMEM((SUBLANES, width), F32)],
        compiler_params=_params(("arbitrary", "arbitrary")),
        name="lru_scan",
    )(*args)


def _bcast_mod(m):
    return jnp.broadcast_to(m[..., None, :], m.shape[:-1] + (SUBLANES, m.shape[-1]))


def kernel(x, c, ctx, c_ctx, w_ada, b_ada, ln_g, ln_b, ffn_w_gate, ffn_w_up, ffn_w_down, attn_w_qkv, attn_w_o, attn_lambda_q1, attn_lambda_k1, attn_lambda_q2, attn_lambda_k2, attn_subln_g, lru_w_in, lru_conv_w, lru_conv_b, lru_w_a, lru_b_a, lru_w_i, lru_b_i, lru_a_param, lru_w_out):
    batch, seq, d = x.shape
    n_ctx = ctx.shape[1]
    depth = w_ada.shape[0]
    assert depth == 2 and batch == SUBLANES
    alpha = (2.0 * depth) ** 0.25

    ada_rows = 2 * SUBLANES
    cc = jnp.zeros((ada_rows, d), F32).at[:batch].set(c).at[batch].set(c_ctx)
    ada = _ada(cc, w_ada, b_ada).reshape(depth, ada_rows, 3, 3, d)

    tm_lat = _pick(seq, (512, 256, 128))
    tm_ctx = _pick(batch * n_ctx, (512, 256, 128))
    tf_lat = _pick(seq, (1024, 512, 256, 128))
    tf_ctx = _pick(batch * n_ctx, (1024, 512, 256, 128))
    zeros_lat = jnp.zeros((batch * seq // tm_lat,), jnp.int32)
    zeros_ctx = jnp.zeros((batch * n_ctx // tm_ctx,), jnp.int32)
    zeros_lat_f = jnp.zeros((batch * seq // tf_lat,), jnp.int32)
    zeros_ctx_f = jnp.zeros((batch * n_ctx // tf_ctx,), jnp.int32)
    bf = lambda w: w.astype(BF16)

    def batch_of_tile(tile):
        return jnp.arange(batch * seq // tile, dtype=jnp.int32) // (seq // tile)

    mod_b = _bcast_mod(ada[0, :batch])
    mod_c = _bcast_mod(ada[0, batch:batch + 1])
    tile_b = batch_of_tile(tm_lat)
    tile_bf = batch_of_tile(tf_lat)
    xl = x.reshape(batch * seq, d)
    xc = ctx.reshape(batch * n_ctx, d)

    ffn_w = (_column_chunks(ffn_w_gate), _column_chunks(ffn_w_up), bf(ffn_w_down))
    xl = _ffn(xl, tile_bf, mod_b[:, 0], ffn_w, (0, 0), ln_g[0, 0], ln_b[0, 0], alpha, tf_lat)
    xc = _ffn(xc, zeros_ctx_f, mod_c[:, 0], ffn_w, (0, 0), ln_g[0, 0], ln_b[0, 0], alpha, tf_ctx)

    lambda_init = 0.8 - 0.6 * math.exp(-0.3 * 0)
    w_qkv = _column_chunks(attn_w_qkv[0], _pick(d, (MATMUL_COLS, 256, 128)))
    tq_lat = _pick(seq, (1024, 512, 256, 128))
    tile_q = jnp.arange(batch * seq // tq_lat, dtype=jnp.int32) // (seq // tq_lat)
    qkv_l = _qkv(xl, batch, tile_q, mod_b[:, 1], w_qkv, *_rope_tables(seq, True))
    tq_ctx = _pick(n_ctx, (1024, 512, 256, 128))
    qkv_c = _qkv(xc, batch, jnp.zeros((batch * n_ctx // tq_ctx,), jnp.int32), mod_c[:, 1], w_qkv,
                 *_rope_tables(n_ctx, False))
    lam_params = (attn_lambda_q1[0], attn_lambda_k1[0], attn_lambda_q2[0], attn_lambda_k2[0])
    o_l = _attention(qkv_l, (qkv_c, qkv_l), lam_params, attn_subln_g[0], lambda_init)
    o_c = _attention(qkv_c, (qkv_c,), lam_params, attn_subln_g[0], lambda_init)
    w_o = bf(attn_w_o[0])
    xl = _proj_ln(o_l.reshape(batch * seq, d), xl, tile_b, mod_b[:, 1], w_o, ln_g[0, 1], ln_b[0, 1],
                  alpha, tm_lat)
    xc = _proj_ln(o_c.reshape(batch * n_ctx, d), xc, zeros_ctx, mod_c[:, 1], w_o, ln_g[0, 1],
                  ln_b[0, 1], alpha, tm_ctx)

    xl = _ffn(xl, tile_bf, mod_b[:, 2], ffn_w, (0, 1), ln_g[0, 2], ln_b[0, 2], alpha, tf_lat)
    xc = _ffn(xc, zeros_ctx_f, mod_c[:, 2], ffn_w, (0, 1), ln_g[0, 2], ln_b[0, 2], alpha, tf_ctx)

    xl = xl.reshape(batch, seq, d).transpose(1, 0, 2).reshape(seq * batch, d)
    xc = xc.reshape(batch, n_ctx, d).transpose(1, 0, 2).reshape(n_ctx * batch, d)
    mod_p = ada[1, :batch].transpose(1, 2, 0, 3)[None]
    mod_c = _bcast_mod(ada[1, batch:batch + 1])

    xl = _ffn(xl, zeros_lat_f, mod_p[:, 0], ffn_w, (1, 0), ln_g[1, 0], ln_b[1, 0], alpha, tf_lat)
    xc = _ffn(xc, zeros_ctx_f, mod_c[:, 0], ffn_w, (1, 0), ln_g[1, 0], ln_b[1, 0], alpha, tf_ctx)

    d_rnn, lru_block_w = lru_w_in.shape[2] // 2, lru_w_a.shape[-1]
    lru_cols = _pick(d_rnn // lru_block_w, (LRU_BLOCKS_PER_STEP, 2, 1)) * lru_block_w
    w_in = _column_chunks(lru_w_in[0], lru_cols)
    y_l = _lru_in(xl, zeros_lat_f, mod_p[:, 1], w_in, tf_lat)
    y_c = _lru_in(xc, zeros_ctx_f, mod_c[:, 1], w_in, tf_ctx)
    w_a, w_i = bf(lru_w_a[0]), bf(lru_w_i[0])
    scan = functools.partial(_lru_scan, conv_w=lru_conv_w[0], conv_b=lru_conv_b[0], w_a=w_a,
                             b_a=lru_b_a[0], w_i=w_i, b_i=lru_b_i[0], a_param=lru_a_param[0])
    h_zero = jnp.zeros((SUBLANES, d_rnn), F32)
    _, h0_f = scan(y_c, h0=h_zero, direction=0, reverse=False)
    _, h0_b = scan(y_c, h0=h_zero, direction=1, reverse=True)
    hf, _ = scan(y_l, h0=h0_f, direction=0, reverse=False)
    hg, _ = scan(y_l, h0=h0_b, direction=1, reverse=True, hf=hf)
    xl = _proj_ln(hg, xl, zeros_lat, mod_p[:, 1], bf(lru_w_out[0]), ln_g[1, 1], ln_b[1, 1], alpha,
                  tm_lat)

    xl = _ffn(xl, zeros_lat_f, mod_p[:, 2], ffn_w, (1, 1), ln_g[1, 2], ln_b[1, 2], alpha, tf_lat)
    return xl.reshape(seq, batch, d).transpose(1, 0, 2)
```

```python
import functools
import math

import jax
import jax.numpy as jnp
from jax import lax
from jax.experimental import pallas as pl
from jax.experimental.pallas import tpu as pltpu

F32 = jnp.float32
BF16 = jnp.bfloat16

SUBLANES = 8
LANES = 128
VMEM_LIMIT_BYTES = 56 * 1024 * 1024
MATMUL_COLS = 512
LRU_BLOCKS_PER_STEP = 5
LRU_CHUNK_TOKENS = 64
EPILOGUE_ROWS = 256
ATTN_SUB_ROWS = 512
ATTN_SUB_TILES = 2
ATTN_MIN_ROW_SUM = 2.0 ** -80
ATTN_BOUND_SLACK = 1.001
ATTN_ONES_ROWS = 16

GRID_W = 64
DA_HEAD_DIM = 64
DA_V_DIM = 2 * DA_HEAD_DIM
ROPE_BASE = 10000.0
AXIS_ROT_DIM = DA_HEAD_DIM // 2
ROT_HALF = AXIS_ROT_DIM // 2
CONV_PAD_LEFT = 2
LRU_C = 8.0
FFN_RES_WEIGHT = 0.5
LN_EPS = 1e-6
GELU_C0 = math.sqrt(2.0 / math.pi)
GELU_C1 = 0.044715


def _params(semantics):
    return pltpu.CompilerParams(dimension_semantics=semantics, vmem_limit_bytes=VMEM_LIMIT_BYTES)


def _pick(n, prefs):
    for p in prefs:
        if n % p == 0:
            return p
    return n


def _column_chunks(w, tn):
    k, n = w.shape
    return w.astype(BF16).reshape(k, n // tn, tn).transpose(1, 0, 2)


def _layer_norm(y, g, b):
    mu = jnp.mean(y, axis=-1, keepdims=True)
    yc = y - mu
    var = jnp.mean(yc * yc, axis=-1, keepdims=True)
    return yc * lax.rsqrt(var + LN_EPS) * g + b


def _row_chunks(n_rows, chunk, fn):
    def body(r, carry):
        fn(pl.ds(pl.multiple_of(r * chunk, chunk), chunk))
        return carry
    lax.fori_loop(0, n_rows // chunk, body, 0)


def _modulate_rows(x_ref, mod_ref, u_ref, chunk):
    tm, d = x_ref.shape
    shift = mod_ref[0]
    scale1 = 1.0 + mod_ref[1]

    def fn(rows):
        xx = x_ref[rows, :].reshape(chunk // SUBLANES, SUBLANES, d)
        u_ref[rows, :] = (xx * scale1 + shift).reshape(chunk, d).astype(u_ref.dtype)
    _row_chunks(tm, chunk, fn)


def _residual_ln(x, h, gate, lng, lnb, alpha):
    rows, d = x.shape
    y = (alpha * x.reshape(rows // SUBLANES, SUBLANES, d)
         + gate * h.reshape(rows // SUBLANES, SUBLANES, d))
    return _layer_norm(y, lng, lnb).reshape(rows, d)


def _ada_kernel(c_ref, w_ref, b_ref, o_ref):
    c = c_ref[...]
    s = c * jax.nn.sigmoid(c)
    o_ref[...] = jnp.dot(s.astype(BF16), w_ref[...].astype(BF16),
                         preferred_element_type=F32) + b_ref[...]


def _ada(cc, w_ada, b_ada):
    depth, d, n = w_ada.shape
    rows = cc.shape[0]
    tn = _pick(n, (1024, 512, 256, 128))
    return pl.pallas_call(
        _ada_kernel,
        grid=(depth, n // tn),
        in_specs=[
            pl.BlockSpec((rows, d), lambda l, j: (0, 0)),
            pl.BlockSpec((None, d, tn), lambda l, j: (l, 0, j)),
            pl.BlockSpec((None, 1, tn), lambda l, j: (l, 0, j)),
        ],
        out_specs=pl.BlockSpec((None, rows, tn), lambda l, j: (l, 0, j)),
        out_shape=jax.ShapeDtypeStruct((depth, rows, n), F32),
        compiler_params=_params(("arbitrary", "arbitrary")),
        name="ada",
    )(cc, w_ada, b_ada.reshape(depth, 1, n))


def _ffn_kernel(tmod_ref, x_ref, mod_ref, wg_ref, wu_ref, wd_ref, lng_ref, lnb_ref, o_ref, u_ref,
                *, alpha, chunk):
    f = pl.program_id(1)
    tm, d = x_ref.shape

    @pl.when(f == 0)
    def _init():
        _modulate_rows(x_ref, mod_ref, u_ref, chunk)
        o_ref[...] = jnp.zeros((tm, d), F32)

    def swiglu(rows):
        u = u_ref[rows, :]
        ga = jnp.dot(u, wg_ref[...], preferred_element_type=F32)
        up = jnp.dot(u, wu_ref[...], preferred_element_type=F32)
        h = (ga * jax.nn.sigmoid(ga) * up).astype(BF16)
        return jnp.dot(h, wd_ref[...], preferred_element_type=F32)

    last = pl.num_programs(1) - 1

    @pl.when(f < last)
    def _accumulate():
        o_ref[...] += swiglu(slice(None))

    @pl.when(f == last)
    def _finish():
        gate, lng, lnb = FFN_RES_WEIGHT * mod_ref[2], lng_ref[...], lnb_ref[...]
        group = min(tm, EPILOGUE_ROWS)
        for r in range(0, tm, group):
            rows = slice(r, r + group)
            o_ref[rows, :] = _residual_ln(x_ref[rows, :], o_ref[rows, :] + swiglu(rows), gate,
                                          lng, lnb, alpha)


def _ffn(x, tile_mod, mod, weights, which, lng, lnb, alpha, tm):
    rows, d = x.shape
    wg, wu, wd = weights
    li, ki = which
    ff = wg.shape[-1]
    tf = _pick(ff, (MATMUL_COLS, 256, 128))
    chunk = _pick(tm, (128, 64, 32, 16))
    kern = functools.partial(_ffn_kernel, alpha=alpha, chunk=chunk)
    grid_spec = pltpu.PrefetchScalarGridSpec(
        num_scalar_prefetch=1,
        grid=(rows // tm, ff // tf),
        in_specs=[
            pl.BlockSpec((tm, d), lambda i, f, t: (i, 0)),
            pl.BlockSpec((None, 3, SUBLANES, d), lambda i, f, t: (t[i], 0, 0, 0)),
            pl.BlockSpec((None, None, d, tf), lambda i, f, t: (li, ki, 0, f)),
            pl.BlockSpec((None, None, d, tf), lambda i, f, t: (li, ki, 0, f)),
            pl.BlockSpec((None, None, tf, d), lambda i, f, t: (li, ki, f, 0)),
            pl.BlockSpec((1, d), lambda i, f, t: (0, 0)),
            pl.BlockSpec((1, d), lambda i, f, t: (0, 0)),
        ],
        out_specs=pl.BlockSpec((tm, d), lambda i, f, t: (i, 0)),
        scratch_shapes=[pltpu.VMEM((tm, d), BF16)],
    )
    return pl.pallas_call(
        kern,
        grid_spec=grid_spec,
        out_shape=jax.ShapeDtypeStruct((rows, d), F32),
        compiler_params=_params(("arbitrary", "arbitrary")),
        name="ffn",
    )(tile_mod, x, mod, wg, wu, wd, lng.reshape(1, d), lnb.reshape(1, d))


def _qkv_kernel(tmod_ref, x_ref, mod_ref, w_ref, cs_ref, sa_ref, sb_ref, o_ref, u_ref, *, chunk,
                n_rot):
    j = pl.program_id(2)

    @pl.when(j == 0)
    def _init():
        _modulate_rows(x_ref, mod_ref, u_ref, chunk)

    heads_per_chunk = o_ref.shape[0]
    tm = u_ref.shape[0]
    group = min(tm, EPILOGUE_ROWS)

    @pl.when(j < n_rot)
    def _rotate():
        for r in range(0, tm, group):
            acc = jnp.dot(u_ref[r:r + group, :], w_ref[...], preferred_element_type=F32)
            cs, sa, sb = cs_ref[r:r + group, :], sa_ref[r:r + group, :], sb_ref[r:r + group, :]
            for hh in range(heads_per_chunk):
                blk = acc[:, hh * LANES:(hh + 1) * LANES]
                rot = (blk * cs + pltpu.roll(blk, LANES - ROT_HALF, 1) * sa
                       + pltpu.roll(blk, ROT_HALF, 1) * sb)
                o_ref[hh, r:r + group, :] = rot.astype(o_ref.dtype)

    @pl.when(j >= n_rot)
    def _plain():
        for r in range(0, tm, group):
            acc = jnp.dot(u_ref[r:r + group, :], w_ref[...], preferred_element_type=F32)
            for hh in range(heads_per_chunk):
                o_ref[hh, r:r + group, :] = acc[:, hh * LANES:(hh + 1) * LANES].astype(o_ref.dtype)


def _qkv(x, batch, tile_mod, mod, w, cs, sa, sb):
    rows, d = x.shape
    n = rows // batch
    heads = d // DA_V_DIM
    tm = _pick(n, (1024, 512, 256, 128))
    tn = _pick(d, (MATMUL_COLS, 256, 128))
    hpc = tn // LANES
    cpw = d // tn
    chunk = _pick(tm, (128, 64, 32, 16))
    nt = n // tm
    grid_spec = pltpu.PrefetchScalarGridSpec(
        num_scalar_prefetch=1,
        grid=(batch, nt, 3 * cpw),
        in_specs=[
            pl.BlockSpec((tm, d), lambda b, t, j, tmr: (b * nt + t, 0)),
            pl.BlockSpec((None, 3, SUBLANES, d), lambda b, t, j, tmr: (tmr[b * nt + t], 0, 0, 0)),
            pl.BlockSpec((d, tn), lambda b, t, j, tmr: (0, j)),
            pl.BlockSpec((None, tm, LANES), lambda b, t, j, tmr: (jnp.minimum(j // cpw, 1), t, 0)),
            pl.BlockSpec((None, tm, LANES), lambda b, t, j, tmr: (jnp.minimum(j // cpw, 1), t, 0)),
            pl.BlockSpec((None, tm, LANES), lambda b, t, j, tmr: (jnp.minimum(j // cpw, 1), t, 0)),
        ],
        out_specs=pl.BlockSpec((None, None, hpc, tm, LANES),
                               lambda b, t, j, tmr: (j // cpw, b, j % cpw, t, 0)),
        scratch_shapes=[pltpu.VMEM((tm, d), BF16)],
    )
    return pl.pallas_call(
        functools.partial(_qkv_kernel, chunk=chunk, n_rot=2 * cpw),
        grid_spec=grid_spec,
        out_shape=jax.ShapeDtypeStruct((3, batch, heads, n, LANES), BF16),
        compiler_params=_params(("arbitrary", "arbitrary", "arbitrary")),
        name="qkv",
    )(tile_mod, x, mod, w, cs, sa, sb)


def _rope_tables(n, rotate):
    scale_q = DA_HEAD_DIM ** -0.5 * math.log2(math.e)
    ones = jnp.ones((n, LANES), F32)
    zeros = jnp.zeros((n, LANES), F32)
    if not rotate:
        return (jnp.stack([scale_q * ones, ones]), jnp.stack([zeros] * 2), jnp.stack([zeros] * 2))
    rows = n // GRID_W
    row = jnp.repeat(jnp.arange(rows, dtype=F32), GRID_W)
    col = jnp.tile(jnp.arange(GRID_W, dtype=F32), rows)
    inv_freq = ROPE_BASE ** (-jnp.arange(0, AXIS_ROT_DIM, 2, dtype=F32) / AXIS_ROT_DIM)
    ang_r = row[:, None] * inv_freq[None, :]
    ang_c = col[:, None] * inv_freq[None, :]
    z = jnp.zeros_like(ang_r)
    cos64 = jnp.concatenate([jnp.cos(ang_r), jnp.cos(ang_r), jnp.cos(ang_c), jnp.cos(ang_c)], -1)
    sa64 = jnp.concatenate([-jnp.sin(ang_r), z, -jnp.sin(ang_c), z], -1)
    sb64 = jnp.concatenate([z, jnp.sin(ang_r), z, jnp.sin(ang_c)], -1)
    cs = jnp.tile(cos64, (1, 2))
    sa = jnp.tile(sa64, (1, 2))
    sb = jnp.tile(sb64, (1, 2))
    return (jnp.stack([scale_q * cs, cs]), jnp.stack([scale_q * sa, sa]),
            jnp.stack([scale_q * sb, sb]))


def _attn_kernel(*refs, n_parts, lambda_init, n_sub):
    q_ref = refs[0]
    k_refs = refs[1:1 + n_parts]
    v_refs = refs[1 + n_parts:1 + 2 * n_parts]
    lq1_ref, lk1_ref, lq2_ref, lk2_ref, g_ref, o_ref = refs[1 + 2 * n_parts:7 + 2 * n_parts]
    vt_refs = refs[7 + 2 * n_parts:7 + 3 * n_parts]
    kmax_ref = refs[7 + 3 * n_parts]
    lam = (jnp.exp(jnp.sum(lq1_ref[...] * lk1_ref[...], axis=-1, keepdims=True))
           - jnp.exp(jnp.sum(lq2_ref[...] * lk2_ref[...], axis=-1, keepdims=True)) + lambda_init)
    tq = q_ref.shape[0] // n_sub
    lane = lax.broadcasted_iota(jnp.int32, (tq, LANES), 1)
    sub_head = (lane < DA_HEAD_DIM, lane >= DA_HEAD_DIM)
    nt_dims = (((1,), (1,)), ((), ()))

    def masked_q(t):
        q = q_ref[t * tq:(t + 1) * tq, :]
        zero = jnp.zeros_like(q)
        return [jnp.where(mask, q, zero) for mask in sub_head]

    def finish(t, o):
        y = o * lax.rsqrt(jnp.mean(o * o, axis=-1, keepdims=True) + LN_EPS) * g_ref[...]
        o_ref[t * tq:(t + 1) * tq, :] = (y * (1.0 - lambda_init)).astype(o_ref.dtype)

    @pl.when(pl.program_id(2) == 0)
    def _per_head_setup():
        for v_ref, vt_ref in zip(v_refs, vt_refs):
            vt_ref[0:LANES, :] = jnp.transpose(v_ref[...].astype(F32)).astype(BF16)
            vt_ref[LANES:, :] = jnp.ones((vt_ref.shape[0] - LANES, vt_ref.shape[1]), BF16)
        k_abs_max = functools.reduce(jnp.maximum, [
            jnp.max(jnp.abs(k_ref[...].astype(F32)), axis=0, keepdims=True) for k_ref in k_refs])
        kmax_ref[...] = jnp.broadcast_to(k_abs_max, kmax_ref.shape)

    k_abs_max = kmax_ref[0:1, :]
    smallest_sum = None
    for t in range(n_sub):
        qs = masked_q(t)
        reach = jnp.abs(q_ref[t * tq:(t + 1) * tq, :].astype(F32)) * k_abs_max
        norm = []
        for e in range(2):
            bound = jnp.sum(jnp.where(sub_head[e], reach, 0.0), axis=-1,
                            keepdims=True) * ATTN_BOUND_SLACK
            shift = jnp.transpose(jnp.broadcast_to(bound, (tq, LANES)))[0:1, :]
            tot = None
            for k_ref, vt_ref in zip(k_refs, vt_refs):
                s_t = lax.dot_general(k_ref[...], qs[e], nt_dims, preferred_element_type=F32)
                p_t = jnp.exp2(s_t - shift).astype(BF16)
                part = jnp.dot(vt_ref[...], p_t, preferred_element_type=F32)
                tot = part if tot is None else tot + part
            norm.append(tot)
            low = jnp.min(tot[LANES:LANES + 1])
            smallest_sum = low if smallest_sum is None else jnp.minimum(smallest_sum, low)
        o_t = (norm[0][:LANES] * (1.0 / norm[0][LANES:LANES + 1])
               - norm[1][:LANES] * (lam / norm[1][LANES:LANES + 1]))
        finish(t, jnp.transpose(o_t))

    def exact_max():
        for t in range(n_sub):
            qs = masked_q(t)
            scores = [[lax.dot_general(qe, k_ref[...], nt_dims, preferred_element_type=F32)
                       for k_ref in k_refs] for qe in qs]
            weights = []
            for e in range(2):
                m = functools.reduce(jnp.maximum,
                                     [jnp.max(s, axis=-1, keepdims=True) for s in scores[e]])
                ps = [jnp.exp2(s - m) for s in scores[e]]
                l = functools.reduce(jnp.add, [jnp.sum(p, axis=-1, keepdims=True) for p in ps])
                weights.append((ps, 1.0 / l))
            o = None
            for p, v_ref in enumerate(v_refs):
                a = weights[0][0][p] * weights[0][1] - weights[1][0][p] * (lam * weights[1][1])
                part = jnp.dot(a.astype(BF16), v_ref[...], preferred_element_type=F32)
                o = part if o is None else o + part
            finish(t, o)

    pl.when(jnp.logical_not(smallest_sum >= ATTN_MIN_ROW_SUM))(exact_max)


def _attention(qkv, kv_parts, lam_params, subln_g, lambda_init):
    _, batch, heads, n, _ = qkv.shape
    tq = _pick(n, (ATTN_SUB_TILES * ATTN_SUB_ROWS, ATTN_SUB_ROWS, 256, 128))
    n_sub = max(tq // ATTN_SUB_ROWS, 1)
    in_specs = [pl.BlockSpec((None, None, None, tq, LANES), lambda b, h, t: (0, b, h, t, 0))]
    in_specs += [pl.BlockSpec((None, None, None, part.shape[3], LANES),
                              lambda b, h, t: (1, b, h, 0, 0)) for part in kv_parts]
    in_specs += [pl.BlockSpec((None, None, None, part.shape[3], LANES),
                              lambda b, h, t: (2, b, h, 0, 0)) for part in kv_parts]
    args = [qkv] + list(kv_parts) + list(kv_parts)
    for p in lam_params:
        in_specs.append(pl.BlockSpec((1, DA_HEAD_DIM), lambda b, h, t: (0, 0)))
        args.append(p.reshape(1, DA_HEAD_DIM))
    in_specs.append(pl.BlockSpec((1, DA_V_DIM), lambda b, h, t: (0, 0)))
    args.append(subln_g.reshape(1, DA_V_DIM))
    return pl.pallas_call(
        functools.partial(_attn_kernel, n_parts=len(kv_parts), lambda_init=lambda_init,
                          n_sub=n_sub),
        grid=(batch, heads, n // tq),
        in_specs=in_specs,
        out_specs=pl.BlockSpec((None, tq, LANES), lambda b, h, t: (b, t, h)),
        out_shape=jax.ShapeDtypeStruct((batch, n, heads * LANES), BF16),
        scratch_shapes=[pltpu.VMEM((LANES + ATTN_ONES_ROWS, part.shape[3]), BF16)
                        for part in kv_parts] + [pltpu.VMEM((SUBLANES, LANES), F32)],
        compiler_params=_params(("arbitrary", "arbitrary", "arbitrary")),
        name="attn",
    )(*args)


def _proj_ln_kernel(tmod_ref, h_ref, x_ref, mod_ref, w_ref, lng_ref, lnb_ref, o_ref, *, alpha, chunk):
    tm, d = x_ref.shape
    gate, lng, lnb = mod_ref[2], lng_ref[...], lnb_ref[...]
    for r in range(0, tm, chunk):
        hh = jnp.dot(h_ref[r:r + chunk, :], w_ref[...], preferred_element_type=F32)
        o_ref[r:r + chunk, :] = _residual_ln(x_ref[r:r + chunk, :], hh, gate, lng, lnb, alpha)


def _proj_ln(h, x, tile_mod, mod, w, lng, lnb, alpha, tm):
    rows, d = x.shape
    k = h.shape[1]
    chunk = _pick(tm, (128, 64, 32, 16))
    grid_spec = pltpu.PrefetchScalarGridSpec(
        num_scalar_prefetch=1,
        grid=(rows // tm,),
        in_specs=[
            pl.BlockSpec((tm, k), lambda i, t: (i, 0)),
            pl.BlockSpec((tm, d), lambda i, t: (i, 0)),
            pl.BlockSpec((None, 3, SUBLANES, d), lambda i, t: (t[i], 0, 0, 0)),
            pl.BlockSpec((k, d), lambda i, t: (0, 0)),
            pl.BlockSpec((1, d), lambda i, t: (0, 0)),
            pl.BlockSpec((1, d), lambda i, t: (0, 0)),
        ],
        out_specs=pl.BlockSpec((tm, d), lambda i, t: (i, 0)),
    )
    return pl.pallas_call(
        functools.partial(_proj_ln_kernel, alpha=alpha, chunk=chunk),
        grid_spec=grid_spec,
        out_shape=jax.ShapeDtypeStruct((rows, d), F32),
        compiler_params=_params(("arbitrary",)),
        name="proj_ln",
    )(tile_mod, h, x, mod, w, lng.reshape(1, d), lnb.reshape(1, d))


def _lru_in_kernel(tmod_ref, x_ref, mod_ref, w_ref, o_ref, u_ref, *, n_gelu, chunk):
    j = pl.program_id(1)

    @pl.when(j == 0)
    def _init():
        _modulate_rows(x_ref, mod_ref, u_ref, chunk)

    tm = u_ref.shape[0]
    group = min(tm, EPILOGUE_ROWS)

    @pl.when(j < n_gelu)
    def _gate_branch():
        for r in range(0, tm, group):
            acc = jnp.dot(u_ref[r:r + group, :], w_ref[...], preferred_element_type=F32)
            cdf = 0.5 * (1.0 + jnp.tanh(GELU_C0 * (acc + GELU_C1 * (acc * acc * acc))))
            o_ref[r:r + group, :] = acc * cdf

    @pl.when(j >= n_gelu)
    def _recurrent_branch():
        o_ref[...] = jnp.dot(u_ref[...], w_ref[...], preferred_element_type=F32)


def _lru_in(x, tile_mod, mod, w, tm):
    rows, d = x.shape
    tn = w.shape[-1]
    n = w.shape[0] * tn
    chunk = _pick(tm, (128, 64, 32, 16))
    grid_spec = pltpu.PrefetchScalarGridSpec(
        num_scalar_prefetch=1,
        grid=(rows // tm, n // tn),
        in_specs=[
            pl.BlockSpec((tm, d), lambda i, j, t: (i, 0)),
            pl.BlockSpec((None, 3, SUBLANES, d), lambda i, j, t: (t[i], 0, 0, 0)),
            pl.BlockSpec((None, d, tn), lambda i, j, t: (j, 0, 0)),
        ],
        out_specs=pl.BlockSpec((None, tm, tn), lambda i, j, t: (j, i, 0)),
        scratch_shapes=[pltpu.VMEM((tm, d), BF16)],
    )
    return pl.pallas_call(
        functools.partial(_lru_in_kernel, n_gelu=(n // 2) // tn, chunk=chunk),
        grid_spec=grid_spec,
        out_shape=jax.ShapeDtypeStruct((n // tn, rows, tn), F32),
        compiler_params=_params(("arbitrary", "arbitrary")),
        name="lru_in",
    )(tile_mod, x, mod, w)


def _lru_scan_kernel(*refs, reverse, combine, n_chunks, n_taps):
    if combine:
        (xr_ref, prev_ref, next_ref, cw_ref, cb_ref, wa_ref, ba_ref, wi_ref, bi_ref, ap_ref, h0_ref,
         hf_ref, g_ref, o_ref, hl_ref, a_s, b_s, carry) = refs
    else:
        (xr_ref, prev_ref, next_ref, cw_ref, cb_ref, wa_ref, ba_ref, wi_ref, bi_ref, ap_ref, h0_ref,
         o_ref, hl_ref, a_s, b_s, carry) = refs
    c = pl.program_id(1)
    c_eff = n_chunks - 1 - c if reverse else c
    rows = xr_ref.shape[0]
    n_blocks, block_w = wa_ref.shape[0], wa_ref.shape[1]
    steps = rows // SUBLANES

    @pl.when(c == 0)
    def _init():
        carry[...] = h0_ref[...]

    halo_l = jnp.where(c_eff > 0, prev_ref[...], 0.0)
    halo_r = jnp.where(c_eff < n_chunks - 1, next_ref[...], 0.0)
    ext = jnp.concatenate([halo_l, xr_ref[...], halo_r], axis=0)
    xc = cb_ref[...]
    for j in range(n_taps):
        xc = xc + ext[j * SUBLANES:j * SUBLANES + rows] * cw_ref[j:j + 1, :]
    xb = xc.astype(BF16)

    def block_diag(w_ref):
        return jnp.concatenate(
            [jnp.dot(xb[:, s * block_w:(s + 1) * block_w], w_ref[s], preferred_element_type=F32)
             for s in range(n_blocks)], axis=1)

    r = jax.nn.sigmoid(block_diag(wa_ref) + ba_ref[...])
    i = jax.nn.sigmoid(block_diag(wi_ref) + bi_ref[...])
    z = -ap_ref[...]
    softplus = jnp.maximum(z, 0.0) + jnp.log(1.0 + jnp.exp(-jnp.abs(z)))
    rate = (-LRU_C * math.log2(math.e)) * softplus
    a = jnp.exp2(r * rate)
    a_s[...] = a
    b_s[...] = jnp.sqrt(1.0 - a * a) * i * xc

    def step(t, h):
        tt = steps - 1 - t if reverse else t
        sl = pl.ds(pl.multiple_of(tt * SUBLANES, SUBLANES), SUBLANES)
        h = a_s[sl, :] * h + b_s[sl, :]
        b_s[sl, :] = h
        return h
    h_last = lax.fori_loop(0, steps, step, carry[...], unroll=8)
    carry[...] = h_last
    hl_ref[...] = h_last
    if combine:
        o_ref[...] = ((hf_ref[...] + b_s[...]) * g_ref[...]).astype(o_ref.dtype)
    else:
        o_ref[...] = b_s[...]


def _lru_scan(y, conv_w, conv_b, w_a, b_a, w_i, b_i, a_param, h0, direction, reverse, hf=None):
    n_blk, rows, width = y.shape[0] // 2, y.shape[1], y.shape[2]
    d_rnn = n_blk * width
    n_taps = conv_w.shape[0]
    block_w = w_a.shape[-1]
    group = width // block_w
    tokens = rows // SUBLANES
    tc = _pick(tokens, (LRU_CHUNK_TOKENS, 32, 16, 8))
    rc = tc * SUBLANES
    n_chunks = tokens // tc
    halo_l = CONV_PAD_LEFT * SUBLANES
    halo_r = (n_taps - 1 - CONV_PAD_LEFT) * SUBLANES
    combine = hf is not None
    last_r = rows // halo_r - 1

    def ce(c):
        return n_chunks - 1 - c if reverse else c

    vec = lambda: pl.BlockSpec((None, 1, width), lambda h, c: (direction, 0, h))
    mat = lambda: pl.BlockSpec((None, group, block_w, block_w), lambda h, c: (direction, h, 0, 0))
    in_specs = [
        pl.BlockSpec((None, rc, width), lambda h, c: (n_blk + h, ce(c), 0)),
        pl.BlockSpec((None, halo_l, width),
                     lambda h, c: (n_blk + h, jnp.maximum(ce(c) * (rc // halo_l) - 1, 0), 0)),
        pl.BlockSpec((None, halo_r, width),
                     lambda h, c: (n_blk + h, jnp.minimum((ce(c) + 1) * (rc // halo_r), last_r), 0)),
        pl.BlockSpec((n_taps, width), lambda h, c: (0, h)),
        pl.BlockSpec((1, width), lambda h, c: (0, h)),
        mat(),
        vec(),
        mat(),
        vec(),
        vec(),
        pl.BlockSpec((SUBLANES, width), lambda h, c: (0, h)),
    ]
    args = [y, y, y, conv_w, conv_b.reshape(1, d_rnn), w_a, b_a.reshape(2, 1, d_rnn), w_i,
            b_i.reshape(2, 1, d_rnn), a_param.reshape(2, 1, d_rnn), h0]
    if combine:
        in_specs += [pl.BlockSpec((None, rc, width), lambda h, c: (h, ce(c), 0)),
                     pl.BlockSpec((None, rc, width), lambda h, c: (h, ce(c), 0))]
        args += [hf, y]
    if combine:
        seq_spec = pl.BlockSpec((rc, width), lambda h, c: (ce(c), h))
        seq_shape = jax.ShapeDtypeStruct((rows, d_rnn), BF16)
    else:
        seq_spec = pl.BlockSpec((None, rc, width), lambda h, c: (h, ce(c), 0))
        seq_shape = jax.ShapeDtypeStruct((n_blk, rows, width), F32)
    return pl.pallas_call(
        functools.partial(_lru_scan_kernel, reverse=reverse, combine=combine, n_chunks=n_chunks,
                          n_taps=n_taps),
        grid=(n_blk, n_chunks),
        in_specs=in_specs,
        out_specs=[seq_spec, pl.BlockSpec((SUBLANES, width), lambda h, c: (0, h))],
        out_shape=[seq_shape, jax.ShapeDtypeStruct((SUBLANES, d_rnn), F32)],
        scratch_shapes=[pltpu.VMEM((rc, width), F32), pltpu.VMEM((rc, width), F32),
                        pltpu.VMEM((SUBLANES, width), F32)],
        compiler_params=_params(("arbitrary", "arbitrary")),
        name="lru_scan",
    )(*args)


def _bcast_mod(m):
    return jnp.broadcast_to(m[..., None, :], m.shape[:-1] + (SUBLANES, m.shape[-1]))


def kernel(x, c, ctx, c_ctx, w_ada, b_ada, ln_g, ln_b, ffn_w_gate, ffn_w_up, ffn_w_down, attn_w_qkv, attn_w_o, attn_lambda_q1, attn_lambda_k1, attn_lambda_q2, attn_lambda_k2, attn_subln_g, lru_w_in, lru_conv_w, lru_conv_b, lru_w_a, lru_b_a, lru_w_i, lru_b_i, lru_a_param, lru_w_out):
    batch, seq, d = x.shape
    n_ctx = ctx.shape[1]
    depth = w_ada.shape[0]
    assert depth == 2 and batch == SUBLANES
    alpha = (2.0 * depth) ** 0.25

    ada_rows = 2 * SUBLANES
    cc = jnp.zeros((ada_rows, d), F32).at[:batch].set(c).at[batch].set(c_ctx)
    ada = _ada(cc, w_ada, b_ada).reshape(depth, ada_rows, 3, 3, d)

    tm_lat = _pick(seq, (512, 256, 128))
    tm_ctx = _pick(batch * n_ctx, (512, 256, 128))
    tf_lat = _pick(seq, (1024, 512, 256, 128))
    tf_ctx = _pick(batch * n_ctx, (1024, 512, 256, 128))
    zeros_lat = jnp.zeros((batch * seq // tm_lat,), jnp.int32)
    zeros_ctx = jnp.zeros((batch * n_ctx // tm_ctx,), jnp.int32)
    zeros_lat_f = jnp.zeros((batch * seq // tf_lat,), jnp.int32)
    zeros_ctx_f = jnp.zeros((batch * n_ctx // tf_ctx,), jnp.int32)
    bf = lambda w: w.astype(BF16)

    def batch_of_tile(tile):
        return jnp.arange(batch * seq // tile, dtype=jnp.int32) // (seq // tile)

    mod_b = _bcast_mod(ada[0, :batch])
    mod_c = _bcast_mod(ada[0, batch:batch + 1])
    tile_b = batch_of_tile(tm_lat)
    tile_bf = batch_of_tile(tf_lat)
    xl = x.reshape(batch * seq, d)
    xc = ctx.reshape(batch * n_ctx, d)

    ffn_w = (bf(ffn_w_gate), bf(ffn_w_up), bf(ffn_w_down))
    xl = _ffn(xl, tile_bf, mod_b[:, 0], ffn_w, (0, 0), ln_g[0, 0], ln_b[0, 0], alpha, tf_lat)
    xc = _ffn(xc, zeros_ctx_f, mod_c[:, 0], ffn_w, (0, 0), ln_g[0, 0], ln_b[0, 0], alpha, tf_ctx)

    lambda_init = 0.8 - 0.6 * math.exp(-0.3 * 0)
    w_qkv = bf(attn_w_qkv[0])
    tq_lat = _pick(seq, (1024, 512, 256, 128))
    tile_q = jnp.arange(batch * seq // tq_lat, dtype=jnp.int32) // (seq // tq_lat)
    qkv_l = _qkv(xl, batch, tile_q, mod_b[:, 1], w_qkv, *_rope_tables(seq, True))
    tq_ctx = _pick(n_ctx, (1024, 512, 256, 128))
    qkv_c = _qkv(xc, batch, jnp.zeros((batch * n_ctx // tq_ctx,), jnp.int32), mod_c[:, 1], w_qkv,
                 *_rope_tables(n_ctx, False))
    lam_params = (attn_lambda_q1[0], attn_lambda_k1[0], attn_lambda_q2[0], attn_lambda_k2[0])
    o_l = _attention(qkv_l, (qkv_c, qkv_l), lam_params, attn_subln_g[0], lambda_init)
    o_c = _attention(qkv_c, (qkv_c,), lam_params, attn_subln_g[0], lambda_init)
    w_o = bf(attn_w_o[0])
    xl = _proj_ln(o_l.reshape(batch * seq, d), xl, tile_b, mod_b[:, 1], w_o, ln_g[0, 1], ln_b[0, 1],
                  alpha, tm_lat)
    xc = _proj_ln(o_c.reshape(batch * n_ctx, d), xc, zeros_ctx, mod_c[:, 1], w_o, ln_g[0, 1],
                  ln_b[0, 1], alpha, tm_ctx)

    xl = _ffn(xl, tile_bf, mod_b[:, 2], ffn_w, (0, 1), ln_g[0, 2], ln_b[0, 2], alpha, tf_lat)
    xc = _ffn(xc, zeros_ctx_f, mod_c[:, 2], ffn_w, (0, 1), ln_g[0, 2], ln_b[0, 2], alpha, tf_ctx)

    xl = xl.reshape(batch, seq, d).transpose(1, 0, 2).reshape(seq * batch, d)
    xc = xc.reshape(batch, n_ctx, d).transpose(1, 0, 2).reshape(n_ctx * batch, d)
    mod_p = ada[1, :batch].transpose(1, 2, 0, 3)[None]
    mod_c = _bcast_mod(ada[1, batch:batch + 1])

    xl = _ffn(xl, zeros_lat_f, mod_p[:, 0], ffn_w, (1, 0), ln_g[1, 0], ln_b[1, 0], alpha, tf_lat)
    xc = _ffn(xc, zeros_ctx_f, mod_c[:, 0], ffn_w, (1, 0), ln_g[1, 0], ln_b[1, 0], alpha, tf_ctx)

    d_rnn, lru_block_w = lru_w_in.shape[2] // 2, lru_w_a.shape[-1]
    lru_cols = _pick(d_rnn // lru_block_w, (LRU_BLOCKS_PER_STEP, 2, 1)) * lru_block_w
    w_in = _column_chunks(lru_w_in[0], lru_cols)
    y_l = _lru_in(xl, zeros_lat_f, mod_p[:, 1], w_in, tf_lat)
    y_c = _lru_in(xc, zeros_ctx_f, mod_c[:, 1], w_in, tf_ctx)
    w_a, w_i = bf(lru_w_a[0]), bf(lru_w_i[0])
    scan = functools.partial(_lru_scan, conv_w=lru_conv_w[0], conv_b=lru_conv_b[0], w_a=w_a,
                             b_a=lru_b_a[0], w_i=w_i, b_i=lru_b_i[0], a_param=lru_a_param[0])
    h_zero = jnp.zeros((SUBLANES, d_rnn), F32)
    _, h0_f = scan(y_c, h0=h_zero, direction=0, reverse=False)
    _, h0_b = scan(y_c, h0=h_zero, direction=1, reverse=True)
    hf, _ = scan(y_l, h0=h0_f, direction=0, reverse=False)
    hg, _ = scan(y_l, h0=h0_b, direction=1, reverse=True, hf=hf)
    xl = _proj_ln(hg, xl, zeros_lat, mod_p[:, 1], bf(lru_w_out[0]), ln_g[1, 1], ln_b[1, 1], alpha,
                  tm_lat)

    xl = _ffn(xl, zeros_lat_f, mod_p[:, 2], ffn_w, (1, 1), ln_g[1, 2], ln_b[1, 2], alpha, tf_lat)
    return xl.reshape(seq, batch, d).transpose(1, 0, 2)
```

```python
import functools
import math

import jax
import jax.numpy as jnp
from jax import lax
from jax.experimental import pallas as pl
from jax.experimental.pallas import tpu as pltpu

F32 = jnp.float32
BF16 = jnp.bfloat16

SUBLANES = 8
LANES = 128
VMEM_LIMIT_BYTES = 56 * 1024 * 1024
MATMUL_COLS = 512
LRU_BLOCKS_PER_STEP = 5
LRU_CHUNK_TOKENS = 64
EPILOGUE_ROWS = 256
ATTN_SUB_ROWS = 512
ATTN_SUB_TILES = 4
ATTN_MIN_ROW_SUM = 2.0 ** -80
ATTN_BOUND_SLACK = 1.001
ATTN_ONES_ROWS = 16

GRID_W = 64
DA_HEAD_DIM = 64
DA_V_DIM = 2 * DA_HEAD_DIM
ROPE_BASE = 10000.0
AXIS_ROT_DIM = DA_HEAD_DIM // 2
ROT_HALF = AXIS_ROT_DIM // 2
CONV_PAD_LEFT = 2
LRU_C = 8.0
FFN_RES_WEIGHT = 0.5
LN_EPS = 1e-6
GELU_C0 = math.sqrt(2.0 / math.pi)
GELU_C1 = 0.044715


def _params(semantics):
    return pltpu.CompilerParams(dimension_semantics=semantics, vmem_limit_bytes=VMEM_LIMIT_BYTES)


def _pick(n, prefs):
    for p in prefs:
        if n % p == 0:
            return p
    return n


def _column_chunks(w, tn):
    k, n = w.shape
    return w.astype(BF16).reshape(k, n // tn, tn).transpose(1, 0, 2)


def _layer_norm(y, g, b):
    mu = jnp.mean(y, axis=-1, keepdims=True)
    yc = y - mu
    var = jnp.mean(yc * yc, axis=-1, keepdims=True)
    return yc * lax.rsqrt(var + LN_EPS) * g + b


def _row_chunks(n_rows, chunk, fn):
    def body(r, carry):
        fn(pl.ds(pl.multiple_of(r * chunk, chunk), chunk))
        return carry
    lax.fori_loop(0, n_rows // chunk, body, 0)


def _modulate_rows(x_ref, mod_ref, u_ref, chunk):
    tm, d = x_ref.shape
    shift = mod_ref[0]
    scale1 = 1.0 + mod_ref[1]

    def fn(rows):
        xx = x_ref[rows, :].reshape(chunk // SUBLANES, SUBLANES, d)
        u_ref[rows, :] = (xx * scale1 + shift).reshape(chunk, d).astype(u_ref.dtype)
    _row_chunks(tm, chunk, fn)


def _residual_ln(x, h, gate, lng, lnb, alpha):
    rows, d = x.shape
    y = (alpha * x.reshape(rows // SUBLANES, SUBLANES, d)
         + gate * h.reshape(rows // SUBLANES, SUBLANES, d))
    return _layer_norm(y, lng, lnb).reshape(rows, d)


def _ada_kernel(c_ref, w_ref, b_ref, o_ref):
    c = c_ref[...]
    s = c * jax.nn.sigmoid(c)
    o_ref[...] = jnp.dot(s.astype(BF16), w_ref[...].astype(BF16),
                         preferred_element_type=F32) + b_ref[...]


def _ada(cc, w_ada, b_ada):
    depth, d, n = w_ada.shape
    rows = cc.shape[0]
    tn = _pick(n, (1024, 512, 256, 128))
    return pl.pallas_call(
        _ada_kernel,
        grid=(depth, n // tn),
        in_specs=[
            pl.BlockSpec((rows, d), lambda l, j: (0, 0)),
            pl.BlockSpec((None, d, tn), lambda l, j: (l, 0, j)),
            pl.BlockSpec((None, 1, tn), lambda l, j: (l, 0, j)),
        ],
        out_specs=pl.BlockSpec((None, rows, tn), lambda l, j: (l, 0, j)),
        out_shape=jax.ShapeDtypeStruct((depth, rows, n), F32),
        compiler_params=_params(("arbitrary", "arbitrary")),
        name="ada",
    )(cc, w_ada, b_ada.reshape(depth, 1, n))


def _ffn_kernel(tmod_ref, x_ref, mod_ref, wg_ref, wu_ref, wd_ref, lng_ref, lnb_ref, o_ref, u_ref,
                *, alpha, chunk):
    f = pl.program_id(1)
    tm, d = x_ref.shape

    @pl.when(f == 0)
    def _init():
        _modulate_rows(x_ref, mod_ref, u_ref, chunk)
        o_ref[...] = jnp.zeros((tm, d), F32)

    def swiglu(rows):
        u = u_ref[rows, :]
        ga = jnp.dot(u, wg_ref[...], preferred_element_type=F32)
        up = jnp.dot(u, wu_ref[...], preferred_element_type=F32)
        h = (ga * jax.nn.sigmoid(ga) * up).astype(BF16)
        return jnp.dot(h, wd_ref[...], preferred_element_type=F32)

    last = pl.num_programs(1) - 1

    @pl.when(f < last)
    def _accumulate():
        o_ref[...] += swiglu(slice(None))

    @pl.when(f == last)
    def _finish():
        gate, lng, lnb = FFN_RES_WEIGHT * mod_ref[2], lng_ref[...], lnb_ref[...]
        group = min(tm, EPILOGUE_ROWS)
        for r in range(0, tm, group):
            rows = slice(r, r + group)
            o_ref[rows, :] = _residual_ln(x_ref[rows, :], o_ref[rows, :] + swiglu(rows), gate,
                                          lng, lnb, alpha)


def _ffn(x, tile_mod, mod, weights, which, lng, lnb, alpha, tm):
    rows, d = x.shape
    wg, wu, wd = weights
    li, ki = which
    ff = wg.shape[-1]
    tf = _pick(ff, (MATMUL_COLS, 256, 128))
    chunk = _pick(tm, (128, 64, 32, 16))
    kern = functools.partial(_ffn_kernel, alpha=alpha, chunk=chunk)
    grid_spec = pltpu.PrefetchScalarGridSpec(
        num_scalar_prefetch=1,
        grid=(rows // tm, ff // tf),
        in_specs=[
            pl.BlockSpec((tm, d), lambda i, f, t: (i, 0)),
            pl.BlockSpec((None, 3, SUBLANES, d), lambda i, f, t: (t[i], 0, 0, 0)),
            pl.BlockSpec((None, None, d, tf), lambda i, f, t: (li, ki, 0, f)),
            pl.BlockSpec((None, None, d, tf), lambda i, f, t: (li, ki, 0, f)),
            pl.BlockSpec((None, None, tf, d), lambda i, f, t: (li, ki, f, 0)),
            pl.BlockSpec((1, d), lambda i, f, t: (0, 0)),
            pl.BlockSpec((1, d), lambda i, f, t: (0, 0)),
        ],
        out_specs=pl.BlockSpec((tm, d), lambda i, f, t: (i, 0)),
        scratch_shapes=[pltpu.VMEM((tm, d), BF16)],
    )
    return pl.pallas_call(
        kern,
        grid_spec=grid_spec,
        out_shape=jax.ShapeDtypeStruct((rows, d), F32),
        compiler_params=_params(("arbitrary", "arbitrary")),
        name="ffn",
    )(tile_mod, x, mod, wg, wu, wd, lng.reshape(1, d), lnb.reshape(1, d))


def _qkv_kernel(tmod_ref, x_ref, mod_ref, w_ref, cs_ref, sa_ref, sb_ref, o_ref, u_ref, *, chunk,
                n_rot):
    j = pl.program_id(2)

    @pl.when(j == 0)
    def _init():
        _modulate_rows(x_ref, mod_ref, u_ref, chunk)

    heads_per_chunk = o_ref.shape[0]
    tm = u_ref.shape[0]
    group = min(tm, EPILOGUE_ROWS)

    @pl.when(j < n_rot)
    def _rotate():
        for r in range(0, tm, group):
            acc = jnp.dot(u_ref[r:r + group, :], w_ref[...], preferred_element_type=F32)
            cs, sa, sb = cs_ref[r:r + group, :], sa_ref[r:r + group, :], sb_ref[r:r + group, :]
            for hh in range(heads_per_chunk):
                blk = acc[:, hh * LANES:(hh + 1) * LANES]
                rot = (blk * cs + pltpu.roll(blk, LANES - ROT_HALF, 1) * sa
                       + pltpu.roll(blk, ROT_HALF, 1) * sb)
                o_ref[hh, r:r + group, :] = rot.astype(o_ref.dtype)

    @pl.when(j >= n_rot)
    def _plain():
        for r in range(0, tm, group):
            acc = jnp.dot(u_ref[r:r + group, :], w_ref[...], preferred_element_type=F32)
            for hh in range(heads_per_chunk):
                o_ref[hh, r:r + group, :] = acc[:, hh * LANES:(hh + 1) * LANES].astype(o_ref.dtype)


def _qkv(x, batch, tile_mod, mod, w, cs, sa, sb):
    rows, d = x.shape
    n = rows // batch
    heads = d // DA_V_DIM
    tm = _pick(n, (1024, 512, 256, 128))
    tn = _pick(d, (MATMUL_COLS, 256, 128))
    hpc = tn // LANES
    cpw = d // tn
    chunk = _pick(tm, (128, 64, 32, 16))
    nt = n // tm
    grid_spec = pltpu.PrefetchScalarGridSpec(
        num_scalar_prefetch=1,
        grid=(batch, nt, 3 * cpw),
        in_specs=[
            pl.BlockSpec((tm, d), lambda b, t, j, tmr: (b * nt + t, 0)),
            pl.BlockSpec((None, 3, SUBLANES, d), lambda b, t, j, tmr: (tmr[b * nt + t], 0, 0, 0)),
            pl.BlockSpec((d, tn), lambda b, t, j, tmr: (0, j)),
            pl.BlockSpec((None, tm, LANES), lambda b, t, j, tmr: (jnp.minimum(j // cpw, 1), t, 0)),
            pl.BlockSpec((None, tm, LANES), lambda b, t, j, tmr: (jnp.minimum(j // cpw, 1), t, 0)),
            pl.BlockSpec((None, tm, LANES), lambda b, t, j, tmr: (jnp.minimum(j // cpw, 1), t, 0)),
        ],
        out_specs=pl.BlockSpec((None, None, hpc, tm, LANES),
                               lambda b, t, j, tmr: (j // cpw, b, j % cpw, t, 0)),
        scratch_shapes=[pltpu.VMEM((tm, d), BF16)],
    )
    return pl.pallas_call(
        functools.partial(_qkv_kernel, chunk=chunk, n_rot=2 * cpw),
        grid_spec=grid_spec,
        out_shape=jax.ShapeDtypeStruct((3, batch, heads, n, LANES), BF16),
        compiler_params=_params(("arbitrary", "arbitrary", "arbitrary")),
        name="qkv",
    )(tile_mod, x, mod, w, cs, sa, sb)


def _rope_tables(n, rotate):
    scale_q = DA_HEAD_DIM ** -0.5 * math.log2(math.e)
    ones = jnp.ones((n, LANES), F32)
    zeros = jnp.zeros((n, LANES), F32)
    if not rotate:
        return (jnp.stack([scale_q * ones, ones]), jnp.stack([zeros] * 2), jnp.stack([zeros] * 2))
    rows = n // GRID_W
    row = jnp.repeat(jnp.arange(rows, dtype=F32), GRID_W)
    col = jnp.tile(jnp.arange(GRID_W, dtype=F32), rows)
    inv_freq = ROPE_BASE ** (-jnp.arange(0, AXIS_ROT_DIM, 2, dtype=F32) / AXIS_ROT_DIM)
    ang_r = row[:, None] * inv_freq[None, :]
    ang_c = col[:, None] * inv_freq[None, :]
    z = jnp.zeros_like(ang_r)
    cos64 = jnp.concatenate([jnp.cos(ang_r), jnp.cos(ang_r), jnp.cos(ang_c), jnp.cos(ang_c)], -1)
    sa64 = jnp.concatenate([-jnp.sin(ang_r), z, -jnp.sin(ang_c), z], -1)
    sb64 = jnp.concatenate([z, jnp.sin(ang_r), z, jnp.sin(ang_c)], -1)
    cs = jnp.tile(cos64, (1, 2))
    sa = jnp.tile(sa64, (1, 2))
    sb = jnp.tile(sb64, (1, 2))
    return (jnp.stack([scale_q * cs, cs]), jnp.stack([scale_q * sa, sa]),
            jnp.stack([scale_q * sb, sb]))


def _attn_kernel(*refs, n_parts, lambda_init, n_sub):
    q_ref = refs[0]
    k_refs = refs[1:1 + n_parts]
    v_refs = refs[1 + n_parts:1 + 2 * n_parts]
    lq1_ref, lk1_ref, lq2_ref, lk2_ref, g_ref, o_ref = refs[1 + 2 * n_parts:7 + 2 * n_parts]
    vt_refs = refs[7 + 2 * n_parts:7 + 3 * n_parts]
    kmax_ref = refs[7 + 3 * n_parts]
    lam = (jnp.exp(jnp.sum(lq1_ref[...] * lk1_ref[...], axis=-1, keepdims=True))
           - jnp.exp(jnp.sum(lq2_ref[...] * lk2_ref[...], axis=-1, keepdims=True)) + lambda_init)
    tq = q_ref.shape[0] // n_sub
    lane = lax.broadcasted_iota(jnp.int32, (tq, LANES), 1)
    sub_head = (lane < DA_HEAD_DIM, lane >= DA_HEAD_DIM)
    nt_dims = (((1,), (1,)), ((), ()))

    def masked_q(t):
        q = q_ref[t * tq:(t + 1) * tq, :]
        zero = jnp.zeros_like(q)
        return [jnp.where(mask, q, zero) for mask in sub_head]

    def finish(t, o):
        y = o * lax.rsqrt(jnp.mean(o * o, axis=-1, keepdims=True) + LN_EPS) * g_ref[...]
        o_ref[t * tq:(t + 1) * tq, :] = (y * (1.0 - lambda_init)).astype(o_ref.dtype)

    @pl.when(pl.program_id(2) == 0)
    def _per_head_setup():
        for v_ref, vt_ref in zip(v_refs, vt_refs):
            vt_ref[0:LANES, :] = jnp.transpose(v_ref[...].astype(F32)).astype(BF16)
            vt_ref[LANES:, :] = jnp.ones((vt_ref.shape[0] - LANES, vt_ref.shape[1]), BF16)
        k_abs_max = functools.reduce(jnp.maximum, [
            jnp.max(jnp.abs(k_ref[...].astype(F32)), axis=0, keepdims=True) for k_ref in k_refs])
        kmax_ref[...] = jnp.broadcast_to(k_abs_max, kmax_ref.shape)

    k_abs_max = kmax_ref[0:1, :]
    smallest_sum = None
    for t in range(n_sub):
        qs = masked_q(t)
        reach = jnp.abs(q_ref[t * tq:(t + 1) * tq, :].astype(F32)) * k_abs_max
        norm = []
        for e in range(2):
            bound = jnp.sum(jnp.where(sub_head[e], reach, 0.0), axis=-1,
                            keepdims=True) * ATTN_BOUND_SLACK
            shift = jnp.transpose(jnp.broadcast_to(bound, (tq, LANES)))[0:1, :]
            tot = None
            for k_ref, vt_ref in zip(k_refs, vt_refs):
                s_t = lax.dot_general(k_ref[...], qs[e], nt_dims, preferred_element_type=F32)
                p_t = jnp.exp2(s_t - shift).astype(BF16)
                part = jnp.dot(vt_ref[...], p_t, preferred_element_type=F32)
                tot = part if tot is None else tot + part
            norm.append(tot)
            low = jnp.min(tot[LANES:LANES + 1])
            smallest_sum = low if smallest_sum is None else jnp.minimum(smallest_sum, low)
        o_t = (norm[0][:LANES] * (1.0 / norm[0][LANES:LANES + 1])
               - norm[1][:LANES] * (lam / norm[1][LANES:LANES + 1]))
        finish(t, jnp.transpose(o_t))

    def exact_max():
        for t in range(n_sub):
            qs = masked_q(t)
            scores = [[lax.dot_general(qe, k_ref[...], nt_dims, preferred_element_type=F32)
                       for k_ref in k_refs] for qe in qs]
            weights = []
            for e in range(2):
                m = functools.reduce(jnp.maximum,
                                     [jnp.max(s, axis=-1, keepdims=True) for s in scores[e]])
                ps = [jnp.exp2(s - m) for s in scores[e]]
                l = functools.reduce(jnp.add, [jnp.sum(p, axis=-1, keepdims=True) for p in ps])
                weights.append((ps, 1.0 / l))
            o = None
            for p, v_ref in enumerate(v_refs):
                a = weights[0][0][p] * weights[0][1] - weights[1][0][p] * (lam * weights[1][1])
                part = jnp.dot(a.astype(BF16), v_ref[...], preferred_element_type=F32)
                o = part if o is None else o + part
            finish(t, o)

    pl.when(jnp.logical_not(smallest_sum >= ATTN_MIN_ROW_SUM))(exact_max)


def _attention(qkv, kv_parts, lam_params, subln_g, lambda_init):
    _, batch, heads, n, _ = qkv.shape
    tq = _pick(n, (ATTN_SUB_TILES * ATTN_SUB_ROWS, ATTN_SUB_ROWS, 256, 128))
    n_sub = max(tq // ATTN_SUB_ROWS, 1)
    in_specs = [pl.BlockSpec((None, None, None, tq, LANES), lambda b, h, t: (0, b, h, t, 0))]
    in_specs += [pl.BlockSpec((None, None, None, part.shape[3], LANES),
                              lambda b, h, t: (1, b, h, 0, 0)) for part in kv_parts]
    in_specs += [pl.BlockSpec((None, None, None, part.shape[3], LANES),
                              lambda b, h, t: (2, b, h, 0, 0)) for part in kv_parts]
    args = [qkv] + list(kv_parts) + list(kv_parts)
    for p in lam_params:
        in_specs.append(pl.BlockSpec((1, DA_HEAD_DIM), lambda b, h, t: (0, 0)))
        args.append(p.reshape(1, DA_HEAD_DIM))
    in_specs.append(pl.BlockSpec((1, DA_V_DIM), lambda b, h, t: (0, 0)))
    args.append(subln_g.reshape(1, DA_V_DIM))
    return pl.pallas_call(
        functools.partial(_attn_kernel, n_parts=len(kv_parts), lambda_init=lambda_init,
                          n_sub=n_sub),
        grid=(batch, heads, n // tq),
        in_specs=in_specs,
        out_specs=pl.BlockSpec((None, tq, LANES), lambda b, h, t: (b, t, h)),
        out_shape=jax.ShapeDtypeStruct((batch, n, heads * LANES), BF16),
        scratch_shapes=[pltpu.VMEM((LANES + ATTN_ONES_ROWS, part.shape[3]), BF16)
                        for part in kv_parts] + [pltpu.VMEM((SUBLANES, LANES), F32)],
        compiler_params=_params(("arbitrary", "arbitrary", "arbitrary")),
        name="attn",
    )(*args)


def _proj_ln_kernel(tmod_ref, h_ref, x_ref, mod_ref, w_ref, lng_ref, lnb_ref, o_ref, *, alpha, chunk):
    tm, d = x_ref.shape
    gate, lng, lnb = mod_ref[2], lng_ref[...], lnb_ref[...]
    for r in range(0, tm, chunk):
        hh = jnp.dot(h_ref[r:r + chunk, :], w_ref[...], preferred_element_type=F32)
        o_ref[r:r + chunk, :] = _residual_ln(x_ref[r:r + chunk, :], hh, gate, lng, lnb, alpha)


def _proj_ln(h, x, tile_mod, mod, w, lng, lnb, alpha, tm):
    rows, d = x.shape
    k = h.shape[1]
    chunk = _pick(tm, (128, 64, 32, 16))
    grid_spec = pltpu.PrefetchScalarGridSpec(
        num_scalar_prefetch=1,
        grid=(rows // tm,),
        in_specs=[
            pl.BlockSpec((tm, k), lambda i, t: (i, 0)),
            pl.BlockSpec((tm, d), lambda i, t: (i, 0)),
            pl.BlockSpec((None, 3, SUBLANES, d), lambda i, t: (t[i], 0, 0, 0)),
            pl.BlockSpec((k, d), lambda i, t: (0, 0)),
            pl.BlockSpec((1, d), lambda i, t: (0, 0)),
            pl.BlockSpec((1, d), lambda i, t: (0, 0)),
        ],
        out_specs=pl.BlockSpec((tm, d), lambda i, t: (i, 0)),
    )
    return pl.pallas_call(
        functools.partial(_proj_ln_kernel, alpha=alpha, chunk=chunk),
        grid_spec=grid_spec,
        out_shape=jax.ShapeDtypeStruct((rows, d), F32),
        compiler_params=_params(("arbitrary",)),
        name="proj_ln",
    )(tile_mod, h, x, mod, w, lng.reshape(1, d), lnb.reshape(1, d))


def _lru_in_kernel(tmod_ref, x_ref, mod_ref, w_ref, o_ref, u_ref, *, n_gelu, chunk):
    j = pl.program_id(1)

    @pl.when(j == 0)
    def _init():
        _modulate_rows(x_ref, mod_ref, u_ref, chunk)

    tm = u_ref.shape[0]
    group = min(tm, EPILOGUE_ROWS)

    @pl.when(j < n_gelu)
    def _gate_branch():
        for r in range(0, tm, group):
            acc = jnp.dot(u_ref[r:r + group, :], w_ref[...], preferred_element_type=F32)
            cdf = 0.5 * (1.0 + jnp.tanh(GELU_C0 * (acc + GELU_C1 * (acc * acc * acc))))
            o_ref[r:r + group, :] = acc * cdf

    @pl.when(j >= n_gelu)
    def _recurrent_branch():
        o_ref[...] = jnp.dot(u_ref[...], w_ref[...], preferred_element_type=F32)


def _lru_in(x, tile_mod, mod, w, tm):
    rows, d = x.shape
    tn = w.shape[-1]
    n = w.shape[0] * tn
    chunk = _pick(tm, (128, 64, 32, 16))
    grid_spec = pltpu.PrefetchScalarGridSpec(
        num_scalar_prefetch=1,
        grid=(rows // tm, n // tn),
        in_specs=[
            pl.BlockSpec((tm, d), lambda i, j, t: (i, 0)),
            pl.BlockSpec((None, 3, SUBLANES, d), lambda i, j, t: (t[i], 0, 0, 0)),
            pl.BlockSpec((None, d, tn), lambda i, j, t: (j, 0, 0)),
        ],
        out_specs=pl.BlockSpec((None, tm, tn), lambda i, j, t: (j, i, 0)),
        scratch_shapes=[pltpu.VMEM((tm, d), BF16)],
    )
    return pl.pallas_call(
        functools.partial(_lru_in_kernel, n_gelu=(n // 2) // tn, chunk=chunk),
        grid_spec=grid_spec,
        out_shape=jax.ShapeDtypeStruct((n // tn, rows, tn), F32),
        compiler_params=_params(("arbitrary", "arbitrary")),
        name="lru_in",
    )(tile_mod, x, mod, w)


def _lru_scan_kernel(*refs, reverse, combine, n_chunks, n_taps):
    if combine:
        (xr_ref, prev_ref, next_ref, cw_ref, cb_ref, wa_ref, ba_ref, wi_ref, bi_ref, ap_ref, h0_ref,
         hf_ref, g_ref, o_ref, hl_ref, a_s, b_s, carry) = refs
    else:
        (xr_ref, prev_ref, next_ref, cw_ref, cb_ref, wa_ref, ba_ref, wi_ref, bi_ref, ap_ref, h0_ref,
         o_ref, hl_ref, a_s, b_s, carry) = refs
    c = pl.program_id(1)
    c_eff = n_chunks - 1 - c if reverse else c
    rows = xr_ref.shape[0]
    n_blocks, block_w = wa_ref.shape[0], wa_ref.shape[1]
    steps = rows // SUBLANES

    @pl.when(c == 0)
    def _init():
        carry[...] = h0_ref[...]

    halo_l = jnp.where(c_eff > 0, prev_ref[...], 0.0)
    halo_r = jnp.where(c_eff < n_chunks - 1, next_ref[...], 0.0)
    ext = jnp.concatenate([halo_l, xr_ref[...], halo_r], axis=0)
    xc = cb_ref[...]
    for j in range(n_taps):
        xc = xc + ext[j * SUBLANES:j * SUBLANES + rows] * cw_ref[j:j + 1, :]
    xb = xc.astype(BF16)

    def block_diag(w_ref):
        return jnp.concatenate(
            [jnp.dot(xb[:, s * block_w:(s + 1) * block_w], w_ref[s], preferred_element_type=F32)
             for s in range(n_blocks)], axis=1)

    r = jax.nn.sigmoid(block_diag(wa_ref) + ba_ref[...])
    i = jax.nn.sigmoid(block_diag(wi_ref) + bi_ref[...])
    z = -ap_ref[...]
    softplus = jnp.maximum(z, 0.0) + jnp.log(1.0 + jnp.exp(-jnp.abs(z)))
    rate = (-LRU_C * math.log2(math.e)) * softplus
    a = jnp.exp2(r * rate)
    a_s[...] = a
    b_s[...] = jnp.sqrt(1.0 - a * a) * i * xc

    def step(t, h):
        tt = steps - 1 - t if reverse else t
        sl = pl.ds(pl.multiple_of(tt * SUBLANES, SUBLANES), SUBLANES)
        h = a_s[sl, :] * h + b_s[sl, :]
        b_s[sl, :] = h
        return h
    h_last = lax.fori_loop(0, steps, step, carry[...], unroll=8)
    carry[...] = h_last
    hl_ref[...] = h_last
    if combine:
        o_ref[...] = ((hf_ref[...] + b_s[...]) * g_ref[...]).astype(o_ref.dtype)
    else:
        o_ref[...] = b_s[...]


def _lru_scan(y, conv_w, conv_b, w_a, b_a, w_i, b_i, a_param, h0, direction, reverse, hf=None):
    n_blk, rows, width = y.shape[0] // 2, y.shape[1], y.shape[2]
    d_rnn = n_blk * width
    n_taps = conv_w.shape[0]
    block_w = w_a.shape[-1]
    group = width // block_w
    tokens = rows // SUBLANES
    tc = _pick(tokens, (LRU_CHUNK_TOKENS, 32, 16, 8))
    rc = tc * SUBLANES
    n_chunks = tokens // tc
    halo_l = CONV_PAD_LEFT * SUBLANES
    halo_r = (n_taps - 1 - CONV_PAD_LEFT) * SUBLANES
    combine = hf is not None
    last_r = rows // halo_r - 1

    def ce(c):
        return n_chunks - 1 - c if reverse else c

    vec = lambda: pl.BlockSpec((None, 1, width), lambda h, c: (direction, 0, h))
    mat = lambda: pl.BlockSpec((None, group, block_w, block_w), lambda h, c: (direction, h, 0, 0))
    in_specs = [
        pl.BlockSpec((None, rc, width), lambda h, c: (n_blk + h, ce(c), 0)),
        pl.BlockSpec((None, halo_l, width),
                     lambda h, c: (n_blk + h, jnp.maximum(ce(c) * (rc // halo_l) - 1, 0), 0)),
        pl.BlockSpec((None, halo_r, width),
                     lambda h, c: (n_blk + h, jnp.minimum((ce(c) + 1) * (rc // halo_r), last_r), 0)),
        pl.BlockSpec((n_taps, width), lambda h, c: (0, h)),
        pl.BlockSpec((1, width), lambda h, c: (0, h)),
        mat(),
        vec(),
        mat(),
        vec(),
        vec(),
        pl.BlockSpec((SUBLANES, width), lambda h, c: (0, h)),
    ]
    args = [y, y, y, conv_w, conv_b.reshape(1, d_rnn), w_a, b_a.reshape(2, 1, d_rnn), w_i,
            b_i.reshape(2, 1, d_rnn), a_param.reshape(2, 1, d_rnn), h0]
    if combine:
        in_specs += [pl.BlockSpec((None, rc, width), lambda h, c: (h, ce(c), 0)),
                     pl.BlockSpec((None, rc, width), lambda h, c: (h, ce(c), 0))]
        args += [hf, y]
    if combine:
        seq_spec = pl.BlockSpec((rc, width), lambda h, c: (ce(c), h))
        seq_shape = jax.ShapeDtypeStruct((rows, d_rnn), BF16)
    else:
        seq_spec = pl.BlockSpec((None, rc, width), lambda h, c: (h, ce(c), 0))
        seq_shape = jax.ShapeDtypeStruct((n_blk, rows, width), F32)
    return pl.pallas_call(
        functools.partial(_lru_scan_kernel, reverse=reverse, combine=combine, n_chunks=n_chunks,
                          n_taps=n_taps),
        grid=(n_blk, n_chunks),
        in_specs=in_specs,
        out_specs=[seq_spec, pl.BlockSpec((SUBLANES, width), lambda h, c: (0, h))],
        out_shape=[seq_shape, jax.ShapeDtypeStruct((SUBLANES, d_rnn), F32)],
        scratch_shapes=[pltpu.VMEM((rc, width), F32), pltpu.VMEM((rc, width), F32),
                        pltpu.VMEM((SUBLANES, width), F32)],
        compiler_params=_params(("arbitrary", "arbitrary")),
        name="lru_scan",
    )(*args)


def _bcast_mod(m):
    return jnp.broadcast_to(m[..., None, :], m.shape[:-1] + (SUBLANES, m.shape[-1]))


def kernel(x, c, ctx, c_ctx, w_ada, b_ada, ln_g, ln_b, ffn_w_gate, ffn_w_up, ffn_w_down, attn_w_qkv, attn_w_o, attn_lambda_q1, attn_lambda_k1, attn_lambda_q2, attn_lambda_k2, attn_subln_g, lru_w_in, lru_conv_w, lru_conv_b, lru_w_a, lru_b_a, lru_w_i, lru_b_i, lru_a_param, lru_w_out):
    batch, seq, d = x.shape
    n_ctx = ctx.shape[1]
    depth = w_ada.shape[0]
    assert depth == 2 and batch == SUBLANES
    alpha = (2.0 * depth) ** 0.25

    ada_rows = 2 * SUBLANES
    cc = jnp.zeros((ada_rows, d), F32).at[:batch].set(c).at[batch].set(c_ctx)
    ada = _ada(cc, w_ada, b_ada).reshape(depth, ada_rows, 3, 3, d)

    tm_lat = _pick(seq, (512, 256, 128))
    tm_ctx = _pick(batch * n_ctx, (512, 256, 128))
    tf_lat = _pick(seq, (1024, 512, 256, 128))
    tf_ctx = _pick(batch * n_ctx, (1024, 512, 256, 128))
    zeros_lat = jnp.zeros((batch * seq // tm_lat,), jnp.int32)
    zeros_ctx = jnp.zeros((batch * n_ctx // tm_ctx,), jnp.int32)
    zeros_lat_f = jnp.zeros((batch * seq // tf_lat,), jnp.int32)
    zeros_ctx_f = jnp.zeros((batch * n_ctx // tf_ctx,), jnp.int32)
    bf = lambda w: w.astype(BF16)

    def batch_of_tile(tile):
        return jnp.arange(batch * seq // tile, dtype=jnp.int32) // (seq // tile)

    mod_b = _bcast_mod(ada[0, :batch])
    mod_c = _bcast_mod(ada[0, batch:batch + 1])
    tile_b = batch_of_tile(tm_lat)
    tile_bf = batch_of_tile(tf_lat)
    xl = x.reshape(batch * seq, d)
    xc = ctx.reshape(batch * n_ctx, d)

    ffn_w = (bf(ffn_w_gate), bf(ffn_w_up), bf(ffn_w_down))
    xl = _ffn(xl, tile_bf, mod_b[:, 0], ffn_w, (0, 0), ln_g[0, 0], ln_b[0, 0], alpha, tf_lat)
    xc = _ffn(xc, zeros_ctx_f, mod_c[:, 0], ffn_w, (0, 0), ln_g[0, 0], ln_b[0, 0], alpha, tf_ctx)

    lambda_init = 0.8 - 0.6 * math.exp(-0.3 * 0)
    w_qkv = bf(attn_w_qkv[0])
    tq_lat = _pick(seq, (1024, 512, 256, 128))
    tile_q = jnp.arange(batch * seq // tq_lat, dtype=jnp.int32) // (seq // tq_lat)
    qkv_l = _qkv(xl, batch, tile_q, mod_b[:, 1], w_qkv, *_rope_tables(seq, True))
    tq_ctx = _pick(n_ctx, (1024, 512, 256, 128))
    qkv_c = _qkv(xc, batch, jnp.zeros((batch * n_ctx // tq_ctx,), jnp.int32), mod_c[:, 1], w_qkv,
                 *_rope_tables(n_ctx, False))
    lam_params = (attn_lambda_q1[0], attn_lambda_k1[0], attn_lambda_q2[0], attn_lambda_k2[0])
    o_l = _attention(qkv_l, (qkv_c, qkv_l), lam_params, attn_subln_g[0], lambda_init)
    o_c = _attention(qkv_c, (qkv_c,), lam_params, attn_subln_g[0], lambda_init)
    w_o = bf(attn_w_o[0])
    xl = _proj_ln(o_l.reshape(batch * seq, d), xl, tile_b, mod_b[:, 1], w_o, ln_g[0, 1], ln_b[0, 1],
                  alpha, tm_lat)
    xc = _proj_ln(o_c.reshape(batch * n_ctx, d), xc, zeros_ctx, mod_c[:, 1], w_o, ln_g[0, 1],
                  ln_b[0, 1], alpha, tm_ctx)

    xl = _ffn(xl, tile_bf, mod_b[:, 2], ffn_w, (0, 1), ln_g[0, 2], ln_b[0, 2], alpha, tf_lat)
    xc = _ffn(xc, zeros_ctx_f, mod_c[:, 2], ffn_w, (0, 1), ln_g[0, 2], ln_b[0, 2], alpha, tf_ctx)

    xl = xl.reshape(batch, seq, d).transpose(1, 0, 2).reshape(seq * batch, d)
    xc = xc.reshape(batch, n_ctx, d).transpose(1, 0, 2).reshape(n_ctx * batch, d)
    mod_p = ada[1, :batch].transpose(1, 2, 0, 3)[None]
    mod_c = _bcast_mod(ada[1, batch:batch + 1])

    xl = _ffn(xl, zeros_lat_f, mod_p[:, 0], ffn_w, (1, 0), ln_g[1, 0], ln_b[1, 0], alpha, tf_lat)
    xc = _ffn(xc, zeros_ctx_f, mod_c[:, 0], ffn_w, (1, 0), ln_g[1, 0], ln_b[1, 0], alpha, tf_ctx)

    d_rnn, lru_block_w = lru_w_in.shape[2] // 2, lru_w_a.shape[-1]
    lru_cols = _pick(d_rnn // lru_block_w, (LRU_BLOCKS_PER_STEP, 2, 1)) * lru_block_w
    w_in = _column_chunks(lru_w_in[0], lru_cols)
    y_l = _lru_in(xl, zeros_lat_f, mod_p[:, 1], w_in, tf_lat)
    y_c = _lru_in(xc, zeros_ctx_f, mod_c[:, 1], w_in, tf_ctx)
    w_a, w_i = bf(lru_w_a[0]), bf(lru_w_i[0])
    scan = functools.partial(_lru_scan, conv_w=lru_conv_w[0], conv_b=lru_conv_b[0], w_a=w_a,
                             b_a=lru_b_a[0], w_i=w_i, b_i=lru_b_i[0], a_param=lru_a_param[0])
    h_zero = jnp.zeros((SUBLANES, d_rnn), F32)
    _, h0_f = scan(y_c, h0=h_zero, direction=0, reverse=False)
    _, h0_b = scan(y_c, h0=h_zero, direction=1, reverse=True)
    hf, _ = scan(y_l, h0=h0_f, direction=0, reverse=False)
    hg, _ = scan(y_l, h0=h0_b, direction=1, reverse=True, hf=hf)
    xl = _proj_ln(hg, xl, zeros_lat, mod_p[:, 1], bf(lru_w_out[0]), ln_g[1, 1], ln_b[1, 1], alpha,
                  tm_lat)

    xl = _ffn(xl, zeros_lat_f, mod_p[:, 2], ffn_w, (1, 1), ln_g[1, 2], ln_b[1, 2], alpha, tf_lat)
    return xl.reshape(seq, batch, d).transpose(1, 0, 2)
```

```python
import functools
import math

import jax
import jax.numpy as jnp
from jax import lax
from jax.experimental import pallas as pl
from jax.experimental.pallas import tpu as pltpu

F32 = jnp.float32
BF16 = jnp.bfloat16

SUBLANES = 8
LANES = 128
VMEM_LIMIT_BYTES = 56 * 1024 * 1024
MATMUL_COLS = 512
LRU_BLOCKS_PER_STEP = 5
LRU_CHUNK_TOKENS = 64
EPILOGUE_ROWS = 256
ATTN_SUB_ROWS = 512
ATTN_SUB_TILES = 1
ATTN_MIN_ROW_SUM = 2.0 ** -80
ATTN_BOUND_SLACK = 1.001
ATTN_ONES_ROWS = 16

GRID_W = 64
DA_HEAD_DIM = 64
DA_V_DIM = 2 * DA_HEAD_DIM
ROPE_BASE = 10000.0
AXIS_ROT_DIM = DA_HEAD_DIM // 2
ROT_HALF = AXIS_ROT_DIM // 2
CONV_PAD_LEFT = 2
LRU_C = 8.0
FFN_RES_WEIGHT = 0.5
LN_EPS = 1e-6
GELU_C0 = math.sqrt(2.0 / math.pi)
GELU_C1 = 0.044715


def _params(semantics):
    return pltpu.CompilerParams(dimension_semantics=semantics, vmem_limit_bytes=VMEM_LIMIT_BYTES)


def _pick(n, prefs):
    for p in prefs:
        if n % p == 0:
            return p
    return n


def _column_chunks(w, tn):
    k, n = w.shape
    return w.astype(BF16).reshape(k, n // tn, tn).transpose(1, 0, 2)


def _layer_norm(y, g, b):
    mu = jnp.mean(y, axis=-1, keepdims=True)
    yc = y - mu
    var = jnp.mean(yc * yc, axis=-1, keepdims=True)
    return yc * lax.rsqrt(var + LN_EPS) * g + b


def _row_chunks(n_rows, chunk, fn):
    def body(r, carry):
        fn(pl.ds(pl.multiple_of(r * chunk, chunk), chunk))
        return carry
    lax.fori_loop(0, n_rows // chunk, body, 0)


def _modulate_rows(x_ref, mod_ref, u_ref, chunk):
    tm, d = x_ref.shape
    shift = mod_ref[0]
    scale1 = 1.0 + mod_ref[1]

    def fn(rows):
        xx = x_ref[rows, :].reshape(chunk // SUBLANES, SUBLANES, d)
        u_ref[rows, :] = (xx * scale1 + shift).reshape(chunk, d).astype(u_ref.dtype)
    _row_chunks(tm, chunk, fn)


def _residual_ln(x, h, gate, lng, lnb, alpha):
    rows, d = x.shape
    y = (alpha * x.reshape(rows // SUBLANES, SUBLANES, d)
         + gate * h.reshape(rows // SUBLANES, SUBLANES, d))
    return _layer_norm(y, lng, lnb).reshape(rows, d)


def _ada_kernel(c_ref, w_ref, b_ref, o_ref):
    c = c_ref[...]
    s = c * jax.nn.sigmoid(c)
    o_ref[...] = jnp.dot(s.astype(BF16), w_ref[...].astype(BF16),
                         preferred_element_type=F32) + b_ref[...]


def _ada(cc, w_ada, b_ada):
    depth, d, n = w_ada.shape
    rows = cc.shape[0]
    tn = _pick(n, (1024, 512, 256, 128))
    return pl.pallas_call(
        _ada_kernel,
        grid=(depth, n // tn),
        in_specs=[
            pl.BlockSpec((rows, d), lambda l, j: (0, 0)),
            pl.BlockSpec((None, d, tn), lambda l, j: (l, 0, j)),
            pl.BlockSpec((None, 1, tn), lambda l, j: (l, 0, j)),
        ],
        out_specs=pl.BlockSpec((None, rows, tn), lambda l, j: (l, 0, j)),
        out_shape=jax.ShapeDtypeStruct((depth, rows, n), F32),
        compiler_params=_params(("arbitrary", "arbitrary")),
        name="ada",
    )(cc, w_ada, b_ada.reshape(depth, 1, n))


def _ffn_kernel(tmod_ref, x_ref, mod_ref, wg_ref, wu_ref, wd_ref, lng_ref, lnb_ref, o_ref, u_ref,
                *, alpha, chunk):
    f = pl.program_id(1)
    tm, d = x_ref.shape

    @pl.when(f == 0)
    def _init():
        _modulate_rows(x_ref, mod_ref, u_ref, chunk)
        o_ref[...] = jnp.zeros((tm, d), F32)

    def swiglu(rows):
        u = u_ref[rows, :]
        ga = jnp.dot(u, wg_ref[...], preferred_element_type=F32)
        up = jnp.dot(u, wu_ref[...], preferred_element_type=F32)
        h = (ga * jax.nn.sigmoid(ga) * up).astype(BF16)
        return jnp.dot(h, wd_ref[...], preferred_element_type=F32)

    last = pl.num_programs(1) - 1

    @pl.when(f < last)
    def _accumulate():
        o_ref[...] += swiglu(slice(None))

    @pl.when(f == last)
    def _finish():
        gate, lng, lnb = FFN_RES_WEIGHT * mod_ref[2], lng_ref[...], lnb_ref[...]
        group = min(tm, EPILOGUE_ROWS)
        for r in range(0, tm, group):
            rows = slice(r, r + group)
            o_ref[rows, :] = _residual_ln(x_ref[rows, :], o_ref[rows, :] + swiglu(rows), gate,
                                          lng, lnb, alpha)


def _ffn(x, tile_mod, mod, weights, which, lng, lnb, alpha, tm):
    rows, d = x.shape
    wg, wu, wd = weights
    li, ki = which
    ff = wg.shape[-1]
    tf = _pick(ff, (MATMUL_COLS, 256, 128))
    chunk = _pick(tm, (128, 64, 32, 16))
    kern = functools.partial(_ffn_kernel, alpha=alpha, chunk=chunk)
    grid_spec = pltpu.PrefetchScalarGridSpec(
        num_scalar_prefetch=1,
        grid=(rows // tm, ff // tf),
        in_specs=[
            pl.BlockSpec((tm, d), lambda i, f, t: (i, 0)),
            pl.BlockSpec((None, 3, SUBLANES, d), lambda i, f, t: (t[i], 0, 0, 0)),
            pl.BlockSpec((None, None, d, tf), lambda i, f, t: (li, ki, 0, f)),
            pl.BlockSpec((None, None, d, tf), lambda i, f, t: (li, ki, 0, f)),
            pl.BlockSpec((None, None, tf, d), lambda i, f, t: (li, ki, f, 0)),
            pl.BlockSpec((1, d), lambda i, f, t: (0, 0)),
            pl.BlockSpec((1, d), lambda i, f, t: (0, 0)),
        ],
        out_specs=pl.BlockSpec((tm, d), lambda i, f, t: (i, 0)),
        scratch_shapes=[pltpu.VMEM((tm, d), BF16)],
    )
    return pl.pallas_call(
        kern,
        grid_spec=grid_spec,
        out_shape=jax.ShapeDtypeStruct((rows, d), F32),
        compiler_params=_params(("arbitrary", "arbitrary")),
        name="ffn",
    )(tile_mod, x, mod, wg, wu, wd, lng.reshape(1, d), lnb.reshape(1, d))


def _qkv_kernel(tmod_ref, x_ref, mod_ref, w_ref, cs_ref, sa_ref, sb_ref, o_ref, u_ref, *, chunk,
                n_rot):
    j = pl.program_id(2)

    @pl.when(j == 0)
    def _init():
        _modulate_rows(x_ref, mod_ref, u_ref, chunk)

    heads_per_chunk = o_ref.shape[0]
    tm = u_ref.shape[0]
    group = min(tm, EPILOGUE_ROWS)

    @pl.when(j < n_rot)
    def _rotate():
        for r in range(0, tm, group):
            acc = jnp.dot(u_ref[r:r + group, :], w_ref[...], preferred_element_type=F32)
            cs, sa, sb = cs_ref[r:r + group, :], sa_ref[r:r + group, :], sb_ref[r:r + group, :]
            for hh in range(heads_per_chunk):
                blk = acc[:, hh * LANES:(hh + 1) * LANES]
                rot = (blk * cs + pltpu.roll(blk, LANES - ROT_HALF, 1) * sa
                       + pltpu.roll(blk, ROT_HALF, 1) * sb)
                o_ref[hh, r:r + group, :] = rot.astype(o_ref.dtype)

    @pl.when(j >= n_rot)
    def _plain():
        for r in range(0, tm, group):
            acc = jnp.dot(u_ref[r:r + group, :], w_ref[...], preferred_element_type=F32)
            for hh in range(heads_per_chunk):
                o_ref[hh, r:r + group, :] = acc[:, hh * LANES:(hh + 1) * LANES].astype(o_ref.dtype)


def _qkv(x, batch, tile_mod, mod, w, cs, sa, sb):
    rows, d = x.shape
    n = rows // batch
    heads = d // DA_V_DIM
    tm = _pick(n, (1024, 512, 256, 128))
    tn = _pick(d, (MATMUL_COLS, 256, 128))
    hpc = tn // LANES
    cpw = d // tn
    chunk = _pick(tm, (128, 64, 32, 16))
    nt = n // tm
    grid_spec = pltpu.PrefetchScalarGridSpec(
        num_scalar_prefetch=1,
        grid=(batch, nt, 3 * cpw),
        in_specs=[
            pl.BlockSpec((tm, d), lambda b, t, j, tmr: (b * nt + t, 0)),
            pl.BlockSpec((None, 3, SUBLANES, d), lambda b, t, j, tmr: (tmr[b * nt + t], 0, 0, 0)),
            pl.BlockSpec((d, tn), lambda b, t, j, tmr: (0, j)),
            pl.BlockSpec((None, tm, LANES), lambda b, t, j, tmr: (jnp.minimum(j // cpw, 1), t, 0)),
            pl.BlockSpec((None, tm, LANES), lambda b, t, j, tmr: (jnp.minimum(j // cpw, 1), t, 0)),
            pl.BlockSpec((None, tm, LANES), lambda b, t, j, tmr: (jnp.minimum(j // cpw, 1), t, 0)),
        ],
        out_specs=pl.BlockSpec((None, None, hpc, tm, LANES),
                               lambda b, t, j, tmr: (j // cpw, b, j % cpw, t, 0)),
        scratch_shapes=[pltpu.VMEM((tm, d), BF16)],
    )
    return pl.pallas_call(
        functools.partial(_qkv_kernel, chunk=chunk, n_rot=2 * cpw),
        grid_spec=grid_spec,
        out_shape=jax.ShapeDtypeStruct((3, batch, heads, n, LANES), BF16),
        compiler_params=_params(("arbitrary", "arbitrary", "arbitrary")),
        name="qkv",
    )(tile_mod, x, mod, w, cs, sa, sb)


def _rope_tables(n, rotate):
    scale_q = DA_HEAD_DIM ** -0.5 * math.log2(math.e)
    ones = jnp.ones((n, LANES), F32)
    zeros = jnp.zeros((n, LANES), F32)
    if not rotate:
        return (jnp.stack([scale_q * ones, ones]), jnp.stack([zeros] * 2), jnp.stack([zeros] * 2))
    rows = n // GRID_W
    row = jnp.repeat(jnp.arange(rows, dtype=F32), GRID_W)
    col = jnp.tile(jnp.arange(GRID_W, dtype=F32), rows)
    inv_freq = ROPE_BASE ** (-jnp.arange(0, AXIS_ROT_DIM, 2, dtype=F32) / AXIS_ROT_DIM)
    ang_r = row[:, None] * inv_freq[None, :]
    ang_c = col[:, None] * inv_freq[None, :]
    z = jnp.zeros_like(ang_r)
    cos64 = jnp.concatenate([jnp.cos(ang_r), jnp.cos(ang_r), jnp.cos(ang_c), jnp.cos(ang_c)], -1)
    sa64 = jnp.concatenate([-jnp.sin(ang_r), z, -jnp.sin(ang_c), z], -1)
    sb64 = jnp.concatenate([z, jnp.sin(ang_r), z, jnp.sin(ang_c)], -1)
    cs = jnp.tile(cos64, (1, 2))
    sa = jnp.tile(sa64, (1, 2))
    sb = jnp.tile(sb64, (1, 2))
    return (jnp.stack([scale_q * cs, cs]), jnp.stack([scale_q * sa, sa]),
            jnp.stack([scale_q * sb, sb]))


def _attn_kernel(*refs, n_parts, lambda_init, n_sub):
    q_ref = refs[0]
    k_refs = refs[1:1 + n_parts]
    v_refs = refs[1 + n_parts:1 + 2 * n_parts]
    lq1_ref, lk1_ref, lq2_ref, lk2_ref, g_ref, o_ref = refs[1 + 2 * n_parts:7 + 2 * n_parts]
    vt_refs = refs[7 + 2 * n_parts:7 + 3 * n_parts]
    kmax_ref = refs[7 + 3 * n_parts]
    lam = (jnp.exp(jnp.sum(lq1_ref[...] * lk1_ref[...], axis=-1, keepdims=True))
           - jnp.exp(jnp.sum(lq2_ref[...] * lk2_ref[...], axis=-1, keepdims=True)) + lambda_init)
    tq = q_ref.shape[0] // n_sub
    lane = lax.broadcasted_iota(jnp.int32, (tq, LANES), 1)
    sub_head = (lane < DA_HEAD_DIM, lane >= DA_HEAD_DIM)
    nt_dims = (((1,), (1,)), ((), ()))

    def masked_q(t):
        q = q_ref[t * tq:(t + 1) * tq, :]
        zero = jnp.zeros_like(q)
        return [jnp.where(mask, q, zero) for mask in sub_head]

    def finish(t, o):
        y = o * lax.rsqrt(jnp.mean(o * o, axis=-1, keepdims=True) + LN_EPS) * g_ref[...]
        o_ref[t * tq:(t + 1) * tq, :] = (y * (1.0 - lambda_init)).astype(o_ref.dtype)

    @pl.when(pl.program_id(2) == 0)
    def _per_head_setup():
        for v_ref, vt_ref in zip(v_refs, vt_refs):
            vt_ref[0:LANES, :] = jnp.transpose(v_ref[...].astype(F32)).astype(BF16)
            vt_ref[LANES:, :] = jnp.ones((vt_ref.shape[0] - LANES, vt_ref.shape[1]), BF16)
        k_abs_max = functools.reduce(jnp.maximum, [
            jnp.max(jnp.abs(k_ref[...].astype(F32)), axis=0, keepdims=True) for k_ref in k_refs])
        kmax_ref[...] = jnp.broadcast_to(k_abs_max, kmax_ref.shape)

    k_abs_max = kmax_ref[0:1, :]
    smallest_sum = None
    for t in range(n_sub):
        qs = masked_q(t)
        reach = jnp.abs(q_ref[t * tq:(t + 1) * tq, :].astype(F32)) * k_abs_max
        norm = []
        for e in range(2):
            bound = jnp.sum(jnp.where(sub_head[e], reach, 0.0), axis=-1,
                            keepdims=True) * ATTN_BOUND_SLACK
            shift = jnp.transpose(jnp.broadcast_to(bound, (tq, LANES)))[0:1, :]
            tot = None
            for k_ref, vt_ref in zip(k_refs, vt_refs):
                s_t = lax.dot_general(k_ref[...], qs[e], nt_dims, preferred_element_type=F32)
                p_t = jnp.exp2(s_t - shift).astype(BF16)
                part = jnp.dot(vt_ref[...], p_t, preferred_element_type=F32)
                tot = part if tot is None else tot + part
            norm.append(tot)
            low = jnp.min(tot[LANES:LANES + 1])
            smallest_sum = low if smallest_sum is None else jnp.minimum(smallest_sum, low)
        o_t = (norm[0][:LANES] * (1.0 / norm[0][LANES:LANES + 1])
               - norm[1][:LANES] * (lam / norm[1][LANES:LANES + 1]))
        finish(t, jnp.transpose(o_t))

    def exact_max():
        for t in range(n_sub):
            qs = masked_q(t)
            scores = [[lax.dot_general(qe, k_ref[...], nt_dims, preferred_element_type=F32)
                       for k_ref in k_refs] for qe in qs]
            weights = []
            for e in range(2):
                m = functools.reduce(jnp.maximum,
                                     [jnp.max(s, axis=-1, keepdims=True) for s in scores[e]])
                ps = [jnp.exp2(s - m) for s in scores[e]]
                l = functools.reduce(jnp.add, [jnp.sum(p, axis=-1, keepdims=True) for p in ps])
                weights.append((ps, 1.0 / l))
            o = None
            for p, v_ref in enumerate(v_refs):
                a = weights[0][0][p] * weights[0][1] - weights[1][0][p] * (lam * weights[1][1])
                part = jnp.dot(a.astype(BF16), v_ref[...], preferred_element_type=F32)
                o = part if o is None else o + part
            finish(t, o)

    pl.when(jnp.logical_not(smallest_sum >= ATTN_MIN_ROW_SUM))(exact_max)


def _attention(qkv, kv_parts, lam_params, subln_g, lambda_init):
    _, batch, heads, n, _ = qkv.shape
    tq = _pick(n, (ATTN_SUB_TILES * ATTN_SUB_ROWS, ATTN_SUB_ROWS, 256, 128))
    n_sub = max(tq // ATTN_SUB_ROWS, 1)
    in_specs = [pl.BlockSpec((None, None, None, tq, LANES), lambda b, h, t: (0, b, h, t, 0))]
    in_specs += [pl.BlockSpec((None, None, None, part.shape[3], LANES),
                              lambda b, h, t: (1, b, h, 0, 0)) for part in kv_parts]
    in_specs += [pl.BlockSpec((None, None, None, part.shape[3], LANES),
                              lambda b, h, t: (2, b, h, 0, 0)) for part in kv_parts]
    args = [qkv] + list(kv_parts) + list(kv_parts)
    for p in lam_params:
        in_specs.append(pl.BlockSpec((1, DA_HEAD_DIM), lambda b, h, t: (0, 0)))
        args.append(p.reshape(1, DA_HEAD_DIM))
    in_specs.append(pl.BlockSpec((1, DA_V_DIM), lambda b, h, t: (0, 0)))
    args.append(subln_g.reshape(1, DA_V_DIM))
    return pl.pallas_call(
        functools.partial(_attn_kernel, n_parts=len(kv_parts), lambda_init=lambda_init,
                          n_sub=n_sub),
        grid=(batch, heads, n // tq),
        in_specs=in_specs,
        out_specs=pl.BlockSpec((None, tq, LANES), lambda b, h, t: (b, t, h)),
        out_shape=jax.ShapeDtypeStruct((batch, n, heads * LANES), BF16),
        scratch_shapes=[pltpu.VMEM((LANES + ATTN_ONES_ROWS, part.shape[3]), BF16)
                        for part in kv_parts] + [pltpu.VMEM((SUBLANES, LANES), F32)],
        compiler_params=_params(("arbitrary", "arbitrary", "arbitrary")),
        name="attn",
    )(*args)


def _proj_ln_kernel(tmod_ref, h_ref, x_ref, mod_ref, w_ref, lng_ref, lnb_ref, o_ref, *, alpha, chunk):
    tm, d = x_ref.shape
    gate, lng, lnb = mod_ref[2], lng_ref[...], lnb_ref[...]
    for r in range(0, tm, chunk):
        hh = jnp.dot(h_ref[r:r + chunk, :], w_ref[...], preferred_element_type=F32)
        o_ref[r:r + chunk, :] = _residual_ln(x_ref[r:r + chunk, :], hh, gate, lng, lnb, alpha)


def _proj_ln(h, x, tile_mod, mod, w, lng, lnb, alpha, tm):
    rows, d = x.shape
    k = h.shape[1]
    chunk = _pick(tm, (128, 64, 32, 16))
    grid_spec = pltpu.PrefetchScalarGridSpec(
        num_scalar_prefetch=1,
        grid=(rows // tm,),
        in_specs=[
            pl.BlockSpec((tm, k), lambda i, t: (i, 0)),
            pl.BlockSpec((tm, d), lambda i, t: (i, 0)),
            pl.BlockSpec((None, 3, SUBLANES, d), lambda i, t: (t[i], 0, 0, 0)),
            pl.BlockSpec((k, d), lambda i, t: (0, 0)),
            pl.BlockSpec((1, d), lambda i, t: (0, 0)),
            pl.BlockSpec((1, d), lambda i, t: (0, 0)),
        ],
        out_specs=pl.BlockSpec((tm, d), lambda i, t: (i, 0)),
    )
    return pl.pallas_call(
        functools.partial(_proj_ln_kernel, alpha=alpha, chunk=chunk),
        grid_spec=grid_spec,
        out_shape=jax.ShapeDtypeStruct((rows, d), F32),
        compiler_params=_params(("arbitrary",)),
        name="proj_ln",
    )(tile_mod, h, x, mod, w, lng.reshape(1, d), lnb.reshape(1, d))


def _lru_in_kernel(tmod_ref, x_ref, mod_ref, w_ref, o_ref, u_ref, *, n_gelu, chunk):
    j = pl.program_id(1)

    @pl.when(j == 0)
    def _init():
        _modulate_rows(x_ref, mod_ref, u_ref, chunk)

    tm = u_ref.shape[0]
    group = min(tm, EPILOGUE_ROWS)

    @pl.when(j < n_gelu)
    def _gate_branch():
        for r in range(0, tm, group):
            acc = jnp.dot(u_ref[r:r + group, :], w_ref[...], preferred_element_type=F32)
            cdf = 0.5 * (1.0 + jnp.tanh(GELU_C0 * (acc + GELU_C1 * (acc * acc * acc))))
            o_ref[r:r + group, :] = acc * cdf

    @pl.when(j >= n_gelu)
    def _recurrent_branch():
        o_ref[...] = jnp.dot(u_ref[...], w_ref[...], preferred_element_type=F32)


def _lru_in(x, tile_mod, mod, w, tm):
    rows, d = x.shape
    tn = w.shape[-1]
    n = w.shape[0] * tn
    chunk = _pick(tm, (128, 64, 32, 16))
    grid_spec = pltpu.PrefetchScalarGridSpec(
        num_scalar_prefetch=1,
        grid=(rows // tm, n // tn),
        in_specs=[
            pl.BlockSpec((tm, d), lambda i, j, t: (i, 0)),
            pl.BlockSpec((None, 3, SUBLANES, d), lambda i, j, t: (t[i], 0, 0, 0)),
            pl.BlockSpec((None, d, tn), lambda i, j, t: (j, 0, 0)),
        ],
        out_specs=pl.BlockSpec((None, tm, tn), lambda i, j, t: (j, i, 0)),
        scratch_shapes=[pltpu.VMEM((tm, d), BF16)],
    )
    return pl.pallas_call(
        functools.partial(_lru_in_kernel, n_gelu=(n // 2) // tn, chunk=chunk),
        grid_spec=grid_spec,
        out_shape=jax.ShapeDtypeStruct((n // tn, rows, tn), F32),
        compiler_params=_params(("arbitrary", "arbitrary")),
        name="lru_in",
    )(tile_mod, x, mod, w)


def _lru_scan_kernel(*refs, reverse, combine, n_chunks, n_taps):
    if combine:
        (xr_ref, prev_ref, next_ref, cw_ref, cb_ref, wa_ref, ba_ref, wi_ref, bi_ref, ap_ref, h0_ref,
         hf_ref, g_ref, o_ref, hl_ref, a_s, b_s, carry) = refs
    else:
        (xr_ref, prev_ref, next_ref, cw_ref, cb_ref, wa_ref, ba_ref, wi_ref, bi_ref, ap_ref, h0_ref,
         o_ref, hl_ref, a_s, b_s, carry) = refs
    c = pl.program_id(1)
    c_eff = n_chunks - 1 - c if reverse else c
    rows = xr_ref.shape[0]
    n_blocks, block_w = wa_ref.shape[0], wa_ref.shape[1]
    steps = rows // SUBLANES

    @pl.when(c == 0)
    def _init():
        carry[...] = h0_ref[...]

    halo_l = jnp.where(c_eff > 0, prev_ref[...], 0.0)
    halo_r = jnp.where(c_eff < n_chunks - 1, next_ref[...], 0.0)
    ext = jnp.concatenate([halo_l, xr_ref[...], halo_r], axis=0)
    xc = cb_ref[...]
    for j in range(n_taps):
        xc = xc + ext[j * SUBLANES:j * SUBLANES + rows] * cw_ref[j:j + 1, :]
    xb = xc.astype(BF16)

    def block_diag(w_ref):
        return jnp.concatenate(
            [jnp.dot(xb[:, s * block_w:(s + 1) * block_w], w_ref[s], preferred_element_type=F32)
             for s in range(n_blocks)], axis=1)

    r = jax.nn.sigmoid(block_diag(wa_ref) + ba_ref[...])
    i = jax.nn.sigmoid(block_diag(wi_ref) + bi_ref[...])
    z = -ap_ref[...]
    softplus = jnp.maximum(z, 0.0) + jnp.log(1.0 + jnp.exp(-jnp.abs(z)))
    rate = (-LRU_C * math.log2(math.e)) * softplus
    a = jnp.exp2(r * rate)
    a_s[...] = a
    b_s[...] = jnp.sqrt(1.0 - a * a) * i * xc

    def step(t, h):
        tt = steps - 1 - t if reverse else t
        sl = pl.ds(pl.multiple_of(tt * SUBLANES, SUBLANES), SUBLANES)
        h = a_s[sl, :] * h + b_s[sl, :]
        b_s[sl, :] = h
        return h
    h_last = lax.fori_loop(0, steps, step, carry[...], unroll=8)
    carry[...] = h_last
    hl_ref[...] = h_last
    if combine:
        o_ref[...] = ((hf_ref[...] + b_s[...]) * g_ref[...]).astype(o_ref.dtype)
    else:
        o_ref[...] = b_s[...]


def _lru_scan(y, conv_w, conv_b, w_a, b_a, w_i, b_i, a_param, h0, direction, reverse, hf=None):
    n_blk, rows, width = y.shape[0] // 2, y.shape[1], y.shape[2]
    d_rnn = n_blk * width
    n_taps = conv_w.shape[0]
    block_w = w_a.shape[-1]
    group = width // block_w
    tokens = rows // SUBLANES
    tc = _pick(tokens, (LRU_CHUNK_TOKENS, 32, 16, 8))
    rc = tc * SUBLANES
    n_chunks = tokens // tc
    halo_l = CONV_PAD_LEFT * SUBLANES
    halo_r = (n_taps - 1 - CONV_PAD_LEFT) * SUBLANES
    combine = hf is not None
    last_r = rows // halo_r - 1

    def ce(c):
        return n_chunks - 1 - c if reverse else c

    vec = lambda: pl.BlockSpec((None, 1, width), lambda h, c: (direction, 0, h))
    mat = lambda: pl.BlockSpec((None, group, block_w, block_w), lambda h, c: (direction, h, 0, 0))
    in_specs = [
        pl.BlockSpec((None, rc, width), lambda h, c: (n_blk + h, ce(c), 0)),
        pl.BlockSpec((None, halo_l, width),
                     lambda h, c: (n_blk + h, jnp.maximum(ce(c) * (rc // halo_l) - 1, 0), 0)),
        pl.BlockSpec((None, halo_r, width),
                     lambda h, c: (n_blk + h, jnp.minimum((ce(c) + 1) * (rc // halo_r), last_r), 0)),
        pl.BlockSpec((n_taps, width), lambda h, c: (0, h)),
        pl.BlockSpec((1, width), lambda h, c: (0, h)),
        mat(),
        vec(),
        mat(),
        vec(),
        vec(),
        pl.BlockSpec((SUBLANES, width), lambda h, c: (0, h)),
    ]
    args = [y, y, y, conv_w, conv_b.reshape(1, d_rnn), w_a, b_a.reshape(2, 1, d_rnn), w_i,
            b_i.reshape(2, 1, d_rnn), a_param.reshape(2, 1, d_rnn), h0]
    if combine:
        in_specs += [pl.BlockSpec((None, rc, width), lambda h, c: (h, ce(c), 0)),
                     pl.BlockSpec((None, rc, width), lambda h, c: (h, ce(c), 0))]
        args += [hf, y]
    if combine:
        seq_spec = pl.BlockSpec((rc, width), lambda h, c: (ce(c), h))
        seq_shape = jax.ShapeDtypeStruct((rows, d_rnn), BF16)
    else:
        seq_spec = pl.BlockSpec((None, rc, width), lambda h, c: (h, ce(c), 0))
        seq_shape = jax.ShapeDtypeStruct((n_blk, rows, width), F32)
    return pl.pallas_call(
        functools.partial(_lru_scan_kernel, reverse=reverse, combine=combine, n_chunks=n_chunks,
                          n_taps=n_taps),
        grid=(n_blk, n_chunks),
        in_specs=in_specs,
        out_specs=[seq_spec, pl.BlockSpec((SUBLANES, width), lambda h, c: (0, h))],
        out_shape=[seq_shape, jax.ShapeDtypeStruct((SUBLANES, d_rnn), F32)],
        scratch_shapes=[pltpu.VMEM((rc, width), F32), pltpu.VMEM((rc, width), F32),
                        pltpu.VMEM((SUBLANES, width), F32)],
        compiler_params=_params(("arbitrary", "arbitrary")),
        name="lru_scan",
    )(*args)


def _bcast_mod(m):
    return jnp.broadcast_to(m[..., None, :], m.shape[:-1] + (SUBLANES, m.shape[-1]))


def kernel(x, c, ctx, c_ctx, w_ada, b_ada, ln_g, ln_b, ffn_w_gate, ffn_w_up, ffn_w_down, attn_w_qkv, attn_w_o, attn_lambda_q1, attn_lambda_k1, attn_lambda_q2, attn_lambda_k2, attn_subln_g, lru_w_in, lru_conv_w, lru_conv_b, lru_w_a, lru_b_a, lru_w_i, lru_b_i, lru_a_param, lru_w_out):
    batch, seq, d = x.shape
    n_ctx = ctx.shape[1]
    depth = w_ada.shape[0]
    assert depth == 2 and batch == SUBLANES
    alpha = (2.0 * depth) ** 0.25

    ada_rows = 2 * SUBLANES
    cc = jnp.zeros((ada_rows, d), F32).at[:batch].set(c).at[batch].set(c_ctx)
    ada = _ada(cc, w_ada, b_ada).reshape(depth, ada_rows, 3, 3, d)

    tm_lat = _pick(seq, (512, 256, 128))
    tm_ctx = _pick(batch * n_ctx, (512, 256, 128))
    tf_lat = _pick(seq, (1024, 512, 256, 128))
    tf_ctx = _pick(batch * n_ctx, (1024, 512, 256, 128))
    zeros_lat = jnp.zeros((batch * seq // tm_lat,), jnp.int32)
    zeros_ctx = jnp.zeros((batch * n_ctx // tm_ctx,), jnp.int32)
    zeros_lat_f = jnp.zeros((batch * seq // tf_lat,), jnp.int32)
    zeros_ctx_f = jnp.zeros((batch * n_ctx // tf_ctx,), jnp.int32)
    bf = lambda w: w.astype(BF16)

    def batch_of_tile(tile):
        return jnp.arange(batch * seq // tile, dtype=jnp.int32) // (seq // tile)

    mod_b = _bcast_mod(ada[0, :batch])
    mod_c = _bcast_mod(ada[0, batch:batch + 1])
    tile_b = batch_of_tile(tm_lat)
    tile_bf = batch_of_tile(tf_lat)
    xl = x.reshape(batch * seq, d)
    xc = ctx.reshape(batch * n_ctx, d)

    ffn_w = (bf(ffn_w_gate), bf(ffn_w_up), bf(ffn_w_down))
    xl = _ffn(xl, tile_bf, mod_b[:, 0], ffn_w, (0, 0), ln_g[0, 0], ln_b[0, 0], alpha, tf_lat)
    xc = _ffn(xc, zeros_ctx_f, mod_c[:, 0], ffn_w, (0, 0), ln_g[0, 0], ln_b[0, 0], alpha, tf_ctx)

    lambda_init = 0.8 - 0.6 * math.exp(-0.3 * 0)
    w_qkv = bf(attn_w_qkv[0])
    tq_lat = _pick(seq, (1024, 512, 256, 128))
    tile_q = jnp.arange(batch * seq // tq_lat, dtype=jnp.int32) // (seq // tq_lat)
    qkv_l = _qkv(xl, batch, tile_q, mod_b[:, 1], w_qkv, *_rope_tables(seq, True))
    tq_ctx = _pick(n_ctx, (1024, 512, 256, 128))
    qkv_c = _qkv(xc, batch, jnp.zeros((batch * n_ctx // tq_ctx,), jnp.int32), mod_c[:, 1], w_qkv,
                 *_rope_tables(n_ctx, False))
    lam_params = (attn_lambda_q1[0], attn_lambda_k1[0], attn_lambda_q2[0], attn_lambda_k2[0])
    o_l = _attention(qkv_l, (qkv_c, qkv_l), lam_params, attn_subln_g[0], lambda_init)
    o_c = _attention(qkv_c, (qkv_c,), lam_params, attn_subln_g[0], lambda_init)
    w_o = bf(attn_w_o[0])
    xl = _proj_ln(o_l.reshape(batch * seq, d), xl, tile_b, mod_b[:, 1], w_o, ln_g[0, 1], ln_b[0, 1],
                  alpha, tm_lat)
    xc = _proj_ln(o_c.reshape(batch * n_ctx, d), xc, zeros_ctx, mod_c[:, 1], w_o, ln_g[0, 1],
                  ln_b[0, 1], alpha, tm_ctx)

    xl = _ffn(xl, tile_bf, mod_b[:, 2], ffn_w, (0, 1), ln_g[0, 2], ln_b[0, 2], alpha, tf_lat)
    xc = _ffn(xc, zeros_ctx_f, mod_c[:, 2], ffn_w, (0, 1), ln_g[0, 2], ln_b[0, 2], alpha, tf_ctx)

    xl = xl.reshape(batch, seq, d).transpose(1, 0, 2).reshape(seq * batch, d)
    xc = xc.reshape(batch, n_ctx, d).transpose(1, 0, 2).reshape(n_ctx * batch, d)
    mod_p = ada[1, :batch].transpose(1, 2, 0, 3)[None]
    mod_c = _bcast_mod(ada[1, batch:batch + 1])

    xl = _ffn(xl, zeros_lat_f, mod_p[:, 0], ffn_w, (1, 0), ln_g[1, 0], ln_b[1, 0], alpha, tf_lat)
    xc = _ffn(xc, zeros_ctx_f, mod_c[:, 0], ffn_w, (1, 0), ln_g[1, 0], ln_b[1, 0], alpha, tf_ctx)

    d_rnn, lru_block_w = lru_w_in.shape[2] // 2, lru_w_a.shape[-1]
    lru_cols = _pick(d_rnn // lru_block_w, (LRU_BLOCKS_PER_STEP, 2, 1)) * lru_block_w
    w_in = _column_chunks(lru_w_in[0], lru_cols)
    y_l = _lru_in(xl, zeros_lat_f, mod_p[:, 1], w_in, tf_lat)
    y_c = _lru_in(xc, zeros_ctx_f, mod_c[:, 1], w_in, tf_ctx)
    w_a, w_i = bf(lru_w_a[0]), bf(lru_w_i[0])
    scan = functools.partial(_lru_scan, conv_w=lru_conv_w[0], conv_b=lru_conv_b[0], w_a=w_a,
                             b_a=lru_b_a[0], w_i=w_i, b_i=lru_b_i[0], a_param=lru_a_param[0])
    h_zero = jnp.zeros((SUBLANES, d_rnn), F32)
    _, h0_f = scan(y_c, h0=h_zero, direction=0, reverse=False)
    _, h0_b = scan(y_c, h0=h_zero, direction=1, reverse=True)
    hf, _ = scan(y_l, h0=h0_f, direction=0, reverse=False)
    hg, _ = scan(y_l, h0=h0_b, direction=1, reverse=True, hf=hf)
    xl = _proj_ln(hg, xl, zeros_lat, mod_p[:, 1], bf(lru_w_out[0]), ln_g[1, 1], ln_b[1, 1], alpha,
                  tm_lat)

    xl = _ffn(xl, zeros_lat_f, mod_p[:, 2], ffn_w, (1, 1), ln_g[1, 2], ln_b[1, 2], alpha, tf_lat)
    return xl.reshape(seq, batch, d).transpose(1, 0, 2)
```

```python
import functools
import math

import jax
import jax.numpy as jnp
from jax import lax
from jax.experimental import pallas as pl
from jax.experimental.pallas import tpu as pltpu

F32 = jnp.float32
BF16 = jnp.bfloat16

SUBLANES = 8
LANES = 128
VMEM_LIMIT_BYTES = 56 * 1024 * 1024
MATMUL_COLS = 512
LRU_BLOCKS_PER_STEP = 5
LRU_CHUNK_TOKENS = 64
EPILOGUE_ROWS = 256
ATTN_SUB_ROWS = 512
ATTN_SUB_TILES = 2
ATTN_MIN_ROW_SUM = 2.0 ** -80
ATTN_BOUND_SLACK = 1.001
ATTN_ONES_ROWS = 16

GRID_W = 64
DA_HEAD_DIM = 64
DA_V_DIM = 2 * DA_HEAD_DIM
ROPE_BASE = 10000.0
AXIS_ROT_DIM = DA_HEAD_DIM // 2
ROT_HALF = AXIS_ROT_DIM // 2
CONV_PAD_LEFT = 2
LRU_C = 8.0
FFN_RES_WEIGHT = 0.5
LN_EPS = 1e-6
GELU_C0 = math.sqrt(2.0 / math.pi)
GELU_C1 = 0.044715


def _params(semantics):
    return pltpu.CompilerParams(dimension_semantics=semantics, vmem_limit_bytes=VMEM_LIMIT_BYTES)


def _pick(n, prefs):
    for p in prefs:
        if n % p == 0:
            return p
    return n


def _column_chunks(w, tn):
    k, n = w.shape
    return w.astype(BF16).reshape(k, n // tn, tn).transpose(1, 0, 2)


def _layer_norm(y, g, b):
    mu = jnp.mean(y, axis=-1, keepdims=True)
    yc = y - mu
    var = jnp.mean(yc * yc, axis=-1, keepdims=True)
    return yc * lax.rsqrt(var + LN_EPS) * g + b


def _row_chunks(n_rows, chunk, fn):
    def body(r, carry):
        fn(pl.ds(pl.multiple_of(r * chunk, chunk), chunk))
        return carry
    lax.fori_loop(0, n_rows // chunk, body, 0)


def _modulate_rows(x_ref, mod_ref, u_ref, chunk):
    tm, d = x_ref.shape
    shift = mod_ref[0]
    scale1 = 1.0 + mod_ref[1]

    def fn(rows):
        xx = x_ref[rows, :].reshape(chunk // SUBLANES, SUBLANES, d)
        u_ref[rows, :] = (xx * scale1 + shift).reshape(chunk, d).astype(u_ref.dtype)
    _row_chunks(tm, chunk, fn)


def _residual_ln(x, h, gate, lng, lnb, alpha):
    rows, d = x.shape
    y = (alpha * x.reshape(rows // SUBLANES, SUBLANES, d)
         + gate * h.reshape(rows // SUBLANES, SUBLANES, d))
    return _layer_norm(y, lng, lnb).reshape(rows, d)


def _ada_kernel(c_ref, w_ref, b_ref, o_ref):
    c = c_ref[...]
    s = c * jax.nn.sigmoid(c)
    o_ref[...] = jnp.dot(s.astype(BF16), w_ref[...].astype(BF16),
                         preferred_element_type=F32) + b_ref[...]


def _ada(cc, w_ada, b_ada):
    depth, d, n = w_ada.shape
    rows = cc.shape[0]
    tn = _pick(n, (1024, 512, 256, 128))
    return pl.pallas_call(
        _ada_kernel,
        grid=(depth, n // tn),
        in_specs=[
            pl.BlockSpec((rows, d), lambda l, j: (0, 0)),
            pl.BlockSpec((None, d, tn), lambda l, j: (l, 0, j)),
            pl.BlockSpec((None, 1, tn), lambda l, j: (l, 0, j)),
        ],
        out_specs=pl.BlockSpec((None, rows, tn), lambda l, j: (l, 0, j)),
        out_shape=jax.ShapeDtypeStruct((depth, rows, n), F32),
        compiler_params=_params(("arbitrary", "arbitrary")),
        name="ada",
    )(cc, w_ada, b_ada.reshape(depth, 1, n))


def _ffn_kernel(tmod_ref, x_ref, mod_ref, wg_ref, wu_ref, wd_ref, lng_ref, lnb_ref, o_ref, u_ref,
                *, alpha, chunk, n_steps):
    f = pl.program_id(1)
    tm, d = x_ref.shape

    @pl.when(f == 0)
    def _init():
        _modulate_rows(x_ref, mod_ref, u_ref, chunk)

    def swiglu(rows):
        u = u_ref[rows, :]
        ga = jnp.dot(u, wg_ref[...], preferred_element_type=F32)
        up = jnp.dot(u, wu_ref[...], preferred_element_type=F32)
        h = (ga * jax.nn.sigmoid(ga) * up).astype(BF16)
        return jnp.dot(h, wd_ref[...], preferred_element_type=F32)

    last = n_steps - 1

    @pl.when(jnp.logical_and(f == 0, f < last))
    def _first():
        o_ref[...] = swiglu(slice(None))

    @pl.when(jnp.logical_and(f > 0, f < last))
    def _accumulate():
        o_ref[...] += swiglu(slice(None))

    @pl.when(f == last)
    def _finish():
        gate, lng, lnb = FFN_RES_WEIGHT * mod_ref[2], lng_ref[...], lnb_ref[...]
        group = min(tm, EPILOGUE_ROWS)
        for r in range(0, tm, group):
            rows = slice(r, r + group)
            total = swiglu(rows) if last == 0 else o_ref[rows, :] + swiglu(rows)
            o_ref[rows, :] = _residual_ln(x_ref[rows, :], total, gate, lng, lnb, alpha)


def _ffn(x, tile_mod, mod, weights, which, lng, lnb, alpha, tm):
    rows, d = x.shape
    wg, wu, wd = weights
    li, ki = which
    ff = wg.shape[-1]
    tf = _pick(ff, (MATMUL_COLS, 256, 128))
    chunk = _pick(tm, (128, 64, 32, 16))
    kern = functools.partial(_ffn_kernel, alpha=alpha, chunk=chunk, n_steps=ff // tf)
    grid_spec = pltpu.PrefetchScalarGridSpec(
        num_scalar_prefetch=1,
        grid=(rows // tm, ff // tf),
        in_specs=[
            pl.BlockSpec((tm, d), lambda i, f, t: (i, 0)),
            pl.BlockSpec((None, 3, SUBLANES, d), lambda i, f, t: (t[i], 0, 0, 0)),
            pl.BlockSpec((None, None, d, tf), lambda i, f, t: (li, ki, 0, f)),
            pl.BlockSpec((None, None, d, tf), lambda i, f, t: (li, ki, 0, f)),
            pl.BlockSpec((None, None, tf, d), lambda i, f, t: (li, ki, f, 0)),
            pl.BlockSpec((1, d), lambda i, f, t: (0, 0)),
            pl.BlockSpec((1, d), lambda i, f, t: (0, 0)),
        ],
        out_specs=pl.BlockSpec((tm, d), lambda i, f, t: (i, 0)),
        scratch_shapes=[pltpu.VMEM((tm, d), BF16)],
    )
    return pl.pallas_call(
        kern,
        grid_spec=grid_spec,
        out_shape=jax.ShapeDtypeStruct((rows, d), F32),
        compiler_params=_params(("arbitrary", "arbitrary")),
        name="ffn",
    )(tile_mod, x, mod, wg, wu, wd, lng.reshape(1, d), lnb.reshape(1, d))


def _qkv_kernel(tmod_ref, x_ref, mod_ref, w_ref, cs_ref, sa_ref, sb_ref, o_ref, u_ref, *, chunk,
                n_rot):
    j = pl.program_id(2)

    @pl.when(j == 0)
    def _init():
        _modulate_rows(x_ref, mod_ref, u_ref, chunk)

    heads_per_chunk = o_ref.shape[0]
    tm = u_ref.shape[0]
    group = min(tm, EPILOGUE_ROWS)

    @pl.when(j < n_rot)
    def _rotate():
        for r in range(0, tm, group):
            acc = jnp.dot(u_ref[r:r + group, :], w_ref[...], preferred_element_type=F32)
            cs, sa, sb = cs_ref[r:r + group, :], sa_ref[r:r + group, :], sb_ref[r:r + group, :]
            for hh in range(heads_per_chunk):
                blk = acc[:, hh * LANES:(hh + 1) * LANES]
                rot = (blk * cs + pltpu.roll(blk, LANES - ROT_HALF, 1) * sa
                       + pltpu.roll(blk, ROT_HALF, 1) * sb)
                o_ref[hh, r:r + group, :] = rot.astype(o_ref.dtype)

    @pl.when(j >= n_rot)
    def _plain():
        for r in range(0, tm, group):
            acc = jnp.dot(u_ref[r:r + group, :], w_ref[...], preferred_element_type=F32)
            for hh in range(heads_per_chunk):
                o_ref[hh, r:r + group, :] = acc[:, hh * LANES:(hh + 1) * LANES].astype(o_ref.dtype)


def _qkv(x, batch, tile_mod, mod, w, cs, sa, sb):
    rows, d = x.shape
    n = rows // batch
    heads = d // DA_V_DIM
    tm = _pick(n, (1024, 512, 256, 128))
    tn = _pick(d, (MATMUL_COLS, 256, 128))
    hpc = tn // LANES
    cpw = d // tn
    chunk = _pick(tm, (128, 64, 32, 16))
    nt = n // tm
    grid_spec = pltpu.PrefetchScalarGridSpec(
        num_scalar_prefetch=1,
        grid=(batch, nt, 3 * cpw),
        in_specs=[
            pl.BlockSpec((tm, d), lambda b, t, j, tmr: (b * nt + t, 0)),
            pl.BlockSpec((None, 3, SUBLANES, d), lambda b, t, j, tmr: (tmr[b * nt + t], 0, 0, 0)),
            pl.BlockSpec((d, tn), lambda b, t, j, tmr: (0, j)),
            pl.BlockSpec((None, tm, LANES), lambda b, t, j, tmr: (jnp.minimum(j // cpw, 1), t, 0)),
            pl.BlockSpec((None, tm, LANES), lambda b, t, j, tmr: (jnp.minimum(j // cpw, 1), t, 0)),
            pl.BlockSpec((None, tm, LANES), lambda b, t, j, tmr: (jnp.minimum(j // cpw, 1), t, 0)),
        ],
        out_specs=pl.BlockSpec((None, None, hpc, tm, LANES),
                               lambda b, t, j, tmr: (j // cpw, b, j % cpw, t, 0)),
        scratch_shapes=[pltpu.VMEM((tm, d), BF16)],
    )
    return pl.pallas_call(
        functools.partial(_qkv_kernel, chunk=chunk, n_rot=2 * cpw),
        grid_spec=grid_spec,
        out_shape=jax.ShapeDtypeStruct((3, batch, heads, n, LANES), BF16),
        compiler_params=_params(("arbitrary", "arbitrary", "arbitrary")),
        name="qkv",
    )(tile_mod, x, mod, w, cs, sa, sb)


def _rope_tables(n, rotate):
    scale_q = DA_HEAD_DIM ** -0.5 * math.log2(math.e)
    ones = jnp.ones((n, LANES), F32)
    zeros = jnp.zeros((n, LANES), F32)
    if not rotate:
        return (jnp.stack([scale_q * ones, ones]), jnp.stack([zeros] * 2), jnp.stack([zeros] * 2))
    rows = n // GRID_W
    row = jnp.repeat(jnp.arange(rows, dtype=F32), GRID_W)
    col = jnp.tile(jnp.arange(GRID_W, dtype=F32), rows)
    inv_freq = ROPE_BASE ** (-jnp.arange(0, AXIS_ROT_DIM, 2, dtype=F32) / AXIS_ROT_DIM)
    ang_r = row[:, None] * inv_freq[None, :]
    ang_c = col[:, None] * inv_freq[None, :]
    z = jnp.zeros_like(ang_r)
    cos64 = jnp.concatenate([jnp.cos(ang_r), jnp.cos(ang_r), jnp.cos(ang_c), jnp.cos(ang_c)], -1)
    sa64 = jnp.concatenate([-jnp.sin(ang_r), z, -jnp.sin(ang_c), z], -1)
    sb64 = jnp.concatenate([z, jnp.sin(ang_r), z, jnp.sin(ang_c)], -1)
    cs = jnp.tile(cos64, (1, 2))
    sa = jnp.tile(sa64, (1, 2))
    sb = jnp.tile(sb64, (1, 2))
    return (jnp.stack([scale_q * cs, cs]), jnp.stack([scale_q * sa, sa]),
            jnp.stack([scale_q * sb, sb]))


def _attn_kernel(*refs, n_parts, lambda_init, n_sub):
    q_ref = refs[0]
    k_refs = refs[1:1 + n_parts]
    v_refs = refs[1 + n_parts:1 + 2 * n_parts]
    lq1_ref, lk1_ref, lq2_ref, lk2_ref, g_ref, o_ref = refs[1 + 2 * n_parts:7 + 2 * n_parts]
    vt_refs = refs[7 + 2 * n_parts:7 + 3 * n_parts]
    kmax_ref = refs[7 + 3 * n_parts]
    lam = (jnp.exp(jnp.sum(lq1_ref[...] * lk1_ref[...], axis=-1, keepdims=True))
           - jnp.exp(jnp.sum(lq2_ref[...] * lk2_ref[...], axis=-1, keepdims=True)) + lambda_init)
    tq = q_ref.shape[0] // n_sub
    lane = lax.broadcasted_iota(jnp.int32, (tq, LANES), 1)
    sub_head = (lane < DA_HEAD_DIM, lane >= DA_HEAD_DIM)
    nt_dims = (((1,), (1,)), ((), ()))

    def masked_q(t):
        q = q_ref[t * tq:(t + 1) * tq, :]
        zero = jnp.zeros_like(q)
        return [jnp.where(mask, q, zero) for mask in sub_head]

    def finish(t, o):
        y = o * lax.rsqrt(jnp.mean(o * o, axis=-1, keepdims=True) + LN_EPS) * g_ref[...]
        o_ref[t * tq:(t + 1) * tq, :] = (y * (1.0 - lambda_init)).astype(o_ref.dtype)

    @pl.when(pl.program_id(2) == 0)
    def _per_head_setup():
        for v_ref, vt_ref in zip(v_refs, vt_refs):
            vt_ref[0:LANES, :] = jnp.transpose(v_ref[...].astype(F32)).astype(BF16)
            vt_ref[LANES:, :] = jnp.ones((vt_ref.shape[0] - LANES, vt_ref.shape[1]), BF16)
        k_abs_max = functools.reduce(jnp.maximum, [
            jnp.max(jnp.abs(k_ref[...].astype(F32)), axis=0, keepdims=True) for k_ref in k_refs])
        kmax_ref[...] = jnp.broadcast_to(k_abs_max, kmax_ref.shape)

    k_abs_max = kmax_ref[0:1, :]
    smallest_sum = None
    for t in range(n_sub):
        qs = masked_q(t)
        reach = jnp.abs(q_ref[t * tq:(t + 1) * tq, :].astype(F32)) * k_abs_max
        norm = []
        for e in range(2):
            bound = jnp.sum(jnp.where(sub_head[e], reach, 0.0), axis=-1,
                            keepdims=True) * ATTN_BOUND_SLACK
            shift = jnp.transpose(jnp.broadcast_to(bound, (tq, LANES)))[0:1, :]
            tot = None
            for k_ref, vt_ref in zip(k_refs, vt_refs):
                s_t = lax.dot_general(k_ref[...], qs[e], nt_dims, preferred_element_type=F32)
                p_t = jnp.exp2(s_t - shift).astype(BF16)
                part = jnp.dot(vt_ref[...], p_t, preferred_element_type=F32)
                tot = part if tot is None else tot + part
            norm.append(tot)
            low = jnp.min(tot[LANES:LANES + 1])
            smallest_sum = low if smallest_sum is None else jnp.minimum(smallest_sum, low)
        o_t = (norm[0][:LANES] * (1.0 / norm[0][LANES:LANES + 1])
               - norm[1][:LANES] * (lam / norm[1][LANES:LANES + 1]))
        finish(t, jnp.transpose(o_t))

    def exact_max():
        for t in range(n_sub):
            qs = masked_q(t)
            scores = [[lax.dot_general(qe, k_ref[...], nt_dims, preferred_element_type=F32)
                       for k_ref in k_refs] for qe in qs]
            weights = []
            for e in range(2):
                m = functools.reduce(jnp.maximum,
                                     [jnp.max(s, axis=-1, keepdims=True) for s in scores[e]])
                ps = [jnp.exp2(s - m) for s in scores[e]]
                l = functools.reduce(jnp.add, [jnp.sum(p, axis=-1, keepdims=True) for p in ps])
                weights.append((ps, 1.0 / l))
            o = None
            for p, v_ref in enumerate(v_refs):
                a = weights[0][0][p] * weights[0][1] - weights[1][0][p] * (lam * weights[1][1])
                part = jnp.dot(a.astype(BF16), v_ref[...], preferred_element_type=F32)
                o = part if o is None else o + part
            finish(t, o)

    pl.when(jnp.logical_not(smallest_sum >= ATTN_MIN_ROW_SUM))(exact_max)


def _attention(qkv, kv_parts, lam_params, subln_g, lambda_init):
    _, batch, heads, n, _ = qkv.shape
    tq = _pick(n, (ATTN_SUB_TILES * ATTN_SUB_ROWS, ATTN_SUB_ROWS, 256, 128))
    n_sub = max(tq // ATTN_SUB_ROWS, 1)
    in_specs = [pl.BlockSpec((None, None, None, tq, LANES), lambda b, h, t: (0, b, h, t, 0))]
    in_specs += [pl.BlockSpec((None, None, None, part.shape[3], LANES),
                              lambda b, h, t: (1, b, h, 0, 0)) for part in kv_parts]
    in_specs += [pl.BlockSpec((None, None, None, part.shape[3], LANES),
                              lambda b, h, t: (2, b, h, 0, 0)) for part in kv_parts]
    args = [qkv] + list(kv_parts) + list(kv_parts)
    for p in lam_params:
        in_specs.append(pl.BlockSpec((1, DA_HEAD_DIM), lambda b, h, t: (0, 0)))
        args.append(p.reshape(1, DA_HEAD_DIM))
    in_specs.append(pl.BlockSpec((1, DA_V_DIM), lambda b, h, t: (0, 0)))
    args.append(subln_g.reshape(1, DA_V_DIM))
    return pl.pallas_call(
        functools.partial(_attn_kernel, n_parts=len(kv_parts), lambda_init=lambda_init,
                          n_sub=n_sub),
        grid=(batch, heads, n // tq),
        in_specs=in_specs,
        out_specs=pl.BlockSpec((None, tq, LANES), lambda b, h, t: (b, t, h)),
        out_shape=jax.ShapeDtypeStruct((batch, n, heads * LANES), BF16),
        scratch_shapes=[pltpu.VMEM((LANES + ATTN_ONES_ROWS, part.shape[3]), BF16)
                        for part in kv_parts] + [pltpu.VMEM((SUBLANES, LANES), F32)],
        compiler_params=_params(("arbitrary", "arbitrary", "arbitrary")),
        name="attn",
    )(*args)


def _proj_ln_kernel(tmod_ref, h_ref, x_ref, mod_ref, w_ref, lng_ref, lnb_ref, o_ref, *, alpha, chunk):
    tm, d = x_ref.shape
    gate, lng, lnb = mod_ref[2], lng_ref[...], lnb_ref[...]
    for r in range(0, tm, chunk):
        hh = jnp.dot(h_ref[r:r + chunk, :], w_ref[...], preferred_element_type=F32)
        o_ref[r:r + chunk, :] = _residual_ln(x_ref[r:r + chunk, :], hh, gate, lng, lnb, alpha)


def _proj_ln(h, x, tile_mod, mod, w, lng, lnb, alpha, tm):
    rows, d = x.shape
    k = h.shape[1]
    chunk = _pick(tm, (128, 64, 32, 16))
    grid_spec = pltpu.PrefetchScalarGridSpec(
        num_scalar_prefetch=1,
        grid=(rows // tm,),
        in_specs=[
            pl.BlockSpec((tm, k), lambda i, t: (i, 0)),
            pl.BlockSpec((tm, d), lambda i, t: (i, 0)),
            pl.BlockSpec((None, 3, SUBLANES, d), lambda i, t: (t[i], 0, 0, 0)),
            pl.BlockSpec((k, d), lambda i, t: (0, 0)),
            pl.BlockSpec((1, d), lambda i, t: (0, 0)),
            pl.BlockSpec((1, d), lambda i, t: (0, 0)),
        ],
        out_specs=pl.BlockSpec((tm, d), lambda i, t: (i, 0)),
    )
    return pl.pallas_call(
        functools.partial(_proj_ln_kernel, alpha=alpha, chunk=chunk),
        grid_spec=grid_spec,
        out_shape=jax.ShapeDtypeStruct((rows, d), F32),
        compiler_params=_params(("arbitrary",)),
        name="proj_ln",
    )(tile_mod, h, x, mod, w, lng.reshape(1, d), lnb.reshape(1, d))


def _lru_in_kernel(tmod_ref, x_ref, mod_ref, w_ref, o_ref, u_ref, *, n_gelu, chunk):
    j = pl.program_id(1)

    @pl.when(j == 0)
    def _init():
        _modulate_rows(x_ref, mod_ref, u_ref, chunk)

    tm = u_ref.shape[0]
    group = min(tm, EPILOGUE_ROWS)

    @pl.when(j < n_gelu)
    def _gate_branch():
        for r in range(0, tm, group):
            acc = jnp.dot(u_ref[r:r + group, :], w_ref[...], preferred_element_type=F32)
            cdf = 0.5 * (1.0 + jnp.tanh(GELU_C0 * (acc + GELU_C1 * (acc * acc * acc))))
            o_ref[r:r + group, :] = acc * cdf

    @pl.when(j >= n_gelu)
    def _recurrent_branch():
        o_ref[...] = jnp.dot(u_ref[...], w_ref[...], preferred_element_type=F32)


def _lru_in(x, tile_mod, mod, w, tm):
    rows, d = x.shape
    tn = w.shape[-1]
    n = w.shape[0] * tn
    chunk = _pick(tm, (128, 64, 32, 16))
    grid_spec = pltpu.PrefetchScalarGridSpec(
        num_scalar_prefetch=1,
        grid=(rows // tm, n // tn),
        in_specs=[
            pl.BlockSpec((tm, d), lambda i, j, t: (i, 0)),
            pl.BlockSpec((None, 3, SUBLANES, d), lambda i, j, t: (t[i], 0, 0, 0)),
            pl.BlockSpec((None, d, tn), lambda i, j, t: (j, 0, 0)),
        ],
        out_specs=pl.BlockSpec((None, tm, tn), lambda i, j, t: (j, i, 0)),
        scratch_shapes=[pltpu.VMEM((tm, d), BF16)],
    )
    return pl.pallas_call(
        functools.partial(_lru_in_kernel, n_gelu=(n // 2) // tn, chunk=chunk),
        grid_spec=grid_spec,
        out_shape=jax.ShapeDtypeStruct((n // tn, rows, tn), F32),
        compiler_params=_params(("arbitrary", "arbitrary")),
        name="lru_in",
    )(tile_mod, x, mod, w)


def _lru_scan_kernel(*refs, reverse, combine, n_chunks, n_taps):
    if combine:
        (xr_ref, prev_ref, next_ref, cw_ref, cb_ref, wa_ref, ba_ref, wi_ref, bi_ref, ap_ref, h0_ref,
         hf_ref, g_ref, o_ref, hl_ref, a_s, b_s, carry) = refs
    else:
        (xr_ref, prev_ref, next_ref, cw_ref, cb_ref, wa_ref, ba_ref, wi_ref, bi_ref, ap_ref, h0_ref,
         o_ref, hl_ref, a_s, b_s, carry) = refs
    c = pl.program_id(1)
    c_eff = n_chunks - 1 - c if reverse else c
    rows = xr_ref.shape[0]
    n_blocks, block_w = wa_ref.shape[0], wa_ref.shape[1]
    steps = rows // SUBLANES

    @pl.when(c == 0)
    def _init():
        carry[...] = h0_ref[...]

    halo_l = jnp.where(c_eff > 0, prev_ref[...], 0.0)
    halo_r = jnp.where(c_eff < n_chunks - 1, next_ref[...], 0.0)
    ext = jnp.concatenate([halo_l, xr_ref[...], halo_r], axis=0)
    xc = cb_ref[...]
    for j in range(n_taps):
        xc = xc + ext[j * SUBLANES:j * SUBLANES + rows] * cw_ref[j:j + 1, :]
    xb = xc.astype(BF16)

    def block_diag(w_ref):
        return jnp.concatenate(
            [jnp.dot(xb[:, s * block_w:(s + 1) * block_w], w_ref[s], preferred_element_type=F32)
             for s in range(n_blocks)], axis=1)

    r = jax.nn.sigmoid(block_diag(wa_ref) + ba_ref[...])
    i = jax.nn.sigmoid(block_diag(wi_ref) + bi_ref[...])
    z = -ap_ref[...]
    softplus = jnp.maximum(z, 0.0) + jnp.log(1.0 + jnp.exp(-jnp.abs(z)))
    rate = (-LRU_C * math.log2(math.e)) * softplus
    a = jnp.exp2(r * rate)
    a_s[...] = a
    b_s[...] = jnp.sqrt(1.0 - a * a) * i * xc

    def step(t, h):
        tt = steps - 1 - t if reverse else t
        sl = pl.ds(pl.multiple_of(tt * SUBLANES, SUBLANES), SUBLANES)
        h = a_s[sl, :] * h + b_s[sl, :]
        b_s[sl, :] = h
        return h
    h_last = lax.fori_loop(0, steps, step, carry[...], unroll=8)
    carry[...] = h_last
    hl_ref[...] = h_last
    if combine:
        o_ref[...] = ((hf_ref[...] + b_s[...]) * g_ref[...]).astype(o_ref.dtype)
    else:
        o_ref[...] = b_s[...]


def _lru_scan(y, conv_w, conv_b, w_a, b_a, w_i, b_i, a_param, h0, direction, reverse, hf=None):
    n_blk, rows, width = y.shape[0] // 2, y.shape[1], y.shape[2]
    d_rnn = n_blk * width
    n_taps = conv_w.shape[0]
    block_w = w_a.shape[-1]
    group = width // block_w
    tokens = rows // SUBLANES
    tc = _pick(tokens, (LRU_CHUNK_TOKENS, 32, 16, 8))
    rc = tc * SUBLANES
    n_chunks = tokens // tc
    halo_l = CONV_PAD_LEFT * SUBLANES
    halo_r = (n_taps - 1 - CONV_PAD_LEFT) * SUBLANES
    combine = hf is not None
    last_r = rows // halo_r - 1

    def ce(c):
        return n_chunks - 1 - c if reverse else c

    vec = lambda: pl.BlockSpec((None, 1, width), lambda h, c: (direction, 0, h))
    mat = lambda: pl.BlockSpec((None, group, block_w, block_w), lambda h, c: (direction, h, 0, 0))
    in_specs = [
        pl.BlockSpec((None, rc, width), lambda h, c: (n_blk + h, ce(c), 0)),
        pl.BlockSpec((None, halo_l, width),
                     lambda h, c: (n_blk + h, jnp.maximum(ce(c) * (rc // halo_l) - 1, 0), 0)),
        pl.BlockSpec((None, halo_r, width),
                     lambda h, c: (n_blk + h, jnp.minimum((ce(c) + 1) * (rc // halo_r), last_r), 0)),
        pl.BlockSpec((n_taps, width), lambda h, c: (0, h)),
        pl.BlockSpec((1, width), lambda h, c: (0, h)),
        mat(),
        vec(),
        mat(),
        vec(),
        vec(),
        pl.BlockSpec((SUBLANES, width), lambda h, c: (0, h)),
    ]
    args = [y, y, y, conv_w, conv_b.reshape(1, d_rnn), w_a, b_a.reshape(2, 1, d_rnn), w_i,
            b_i.reshape(2, 1, d_rnn), a_param.reshape(2, 1, d_rnn), h0]
    if combine:
        in_specs += [pl.BlockSpec((None, rc, width), lambda h, c: (h, ce(c), 0)),
                     pl.BlockSpec((None, rc, width), lambda h, c: (h, ce(c), 0))]
        args += [hf, y]
    if combine:
        seq_spec = pl.BlockSpec((rc, width), lambda h, c: (ce(c), h))
        seq_shape = jax.ShapeDtypeStruct((rows, d_rnn), BF16)
    else:
        seq_spec = pl.BlockSpec((None, rc, width), lambda h, c: (h, ce(c), 0))
        seq_shape = jax.ShapeDtypeStruct((n_blk, rows, width), F32)
    return pl.pallas_call(
        functools.partial(_lru_scan_kernel, reverse=reverse, combine=combine, n_chunks=n_chunks,
                          n_taps=n_taps),
        grid=(n_blk, n_chunks),
        in_specs=in_specs,
        out_specs=[seq_spec, pl.BlockSpec((SUBLANES, width), lambda h, c: (0, h))],
        out_shape=[seq_shape, jax.ShapeDtypeStruct((SUBLANES, d_rnn), F32)],
        scratch_shapes=[pltpu.VMEM((rc, width), F32), pltpu.VMEM((rc, width), F32),
                        pltpu.VMEM((SUBLANES, width), F32)],
        compiler_params=_params(("arbitrary", "arbitrary")),
        name="lru_scan",
    )(*args)


def _bcast_mod(m):
    return jnp.broadcast_to(m[..., None, :], m.shape[:-1] + (SUBLANES, m.shape[-1]))


def kernel(x, c, ctx, c_ctx, w_ada, b_ada, ln_g, ln_b, ffn_w_gate, ffn_w_up, ffn_w_down, attn_w_qkv, attn_w_o, attn_lambda_q1, attn_lambda_k1, attn_lambda_q2, attn_lambda_k2, attn_subln_g, lru_w_in, lru_conv_w, lru_conv_b, lru_w_a, lru_b_a, lru_w_i, lru_b_i, lru_a_param, lru_w_out):
    batch, seq, d = x.shape
    n_ctx = ctx.shape[1]
    depth = w_ada.shape[0]
    assert depth == 2 and batch == SUBLANES
    alpha = (2.0 * depth) ** 0.25

    ada_rows = 2 * SUBLANES
    cc = jnp.zeros((ada_rows, d), F32).at[:batch].set(c).at[batch].set(c_ctx)
    ada = _ada(cc, w_ada, b_ada).reshape(depth, ada_rows, 3, 3, d)

    tm_lat = _pick(seq, (512, 256, 128))
    tm_ctx = _pick(batch * n_ctx, (512, 256, 128))
    tf_lat = _pick(seq, (1024, 512, 256, 128))
    tf_ctx = _pick(batch * n_ctx, (1024, 512, 256, 128))
    zeros_lat = jnp.zeros((batch * seq // tm_lat,), jnp.int32)
    zeros_ctx = jnp.zeros((batch * n_ctx // tm_ctx,), jnp.int32)
    zeros_lat_f = jnp.zeros((batch * seq // tf_lat,), jnp.int32)
    zeros_ctx_f = jnp.zeros((batch * n_ctx // tf_ctx,), jnp.int32)
    bf = lambda w: w.astype(BF16)

    def batch_of_tile(tile):
        return jnp.arange(batch * seq // tile, dtype=jnp.int32) // (seq // tile)

    mod_b = _bcast_mod(ada[0, :batch])
    mod_c = _bcast_mod(ada[0, batch:batch + 1])
    tile_b = batch_of_tile(tm_lat)
    tile_bf = batch_of_tile(tf_lat)
    xl = x.reshape(batch * seq, d)
    xc = ctx.reshape(batch * n_ctx, d)

    ffn_w = (bf(ffn_w_gate), bf(ffn_w_up), bf(ffn_w_down))
    xl = _ffn(xl, tile_bf, mod_b[:, 0], ffn_w, (0, 0), ln_g[0, 0], ln_b[0, 0], alpha, tf_lat)
    xc = _ffn(xc, zeros_ctx_f, mod_c[:, 0], ffn_w, (0, 0), ln_g[0, 0], ln_b[0, 0], alpha, tf_ctx)

    lambda_init = 0.8 - 0.6 * math.exp(-0.3 * 0)
    w_qkv = bf(attn_w_qkv[0])
    tq_lat = _pick(seq, (1024, 512, 256, 128))
    tile_q = jnp.arange(batch * seq // tq_lat, dtype=jnp.int32) // (seq // tq_lat)
    qkv_l = _qkv(xl, batch, tile_q, mod_b[:, 1], w_qkv, *_rope_tables(seq, True))
    tq_ctx = _pick(n_ctx, (1024, 512, 256, 128))
    qkv_c = _qkv(xc, batch, jnp.zeros((batch * n_ctx // tq_ctx,), jnp.int32), mod_c[:, 1], w_qkv,
                 *_rope_tables(n_ctx, False))
    lam_params = (attn_lambda_q1[0], attn_lambda_k1[0], attn_lambda_q2[0], attn_lambda_k2[0])
    o_l = _attention(qkv_l, (qkv_c, qkv_l), lam_params, attn_subln_g[0], lambda_init)
    o_c = _attention(qkv_c, (qkv_c,), lam_params, attn_subln_g[0], lambda_init)
    w_o = bf(attn_w_o[0])
    xl = _proj_ln(o_l.reshape(batch * seq, d), xl, tile_b, mod_b[:, 1], w_o, ln_g[0, 1], ln_b[0, 1],
                  alpha, tm_lat)
    xc = _proj_ln(o_c.reshape(batch * n_ctx, d), xc, zeros_ctx, mod_c[:, 1], w_o, ln_g[0, 1],
                  ln_b[0, 1], alpha, tm_ctx)

    xl = _ffn(xl, tile_bf, mod_b[:, 2], ffn_w, (0, 1), ln_g[0, 2], ln_b[0, 2], alpha, tf_lat)
    xc = _ffn(xc, zeros_ctx_f, mod_c[:, 2], ffn_w, (0, 1), ln_g[0, 2], ln_b[0, 2], alpha, tf_ctx)

    xl = xl.reshape(batch, seq, d).transpose(1, 0, 2).reshape(seq * batch, d)
    xc = xc.reshape(batch, n_ctx, d).transpose(1, 0, 2).reshape(n_ctx * batch, d)
    mod_p = ada[1, :batch].transpose(1, 2, 0, 3)[None]
    mod_c = _bcast_mod(ada[1, batch:batch + 1])

    xl = _ffn(xl, zeros_lat_f, mod_p[:, 0], ffn_w, (1, 0), ln_g[1, 0], ln_b[1, 0], alpha, tf_lat)
    xc = _ffn(xc, zeros_ctx_f, mod_c[:, 0], ffn_w, (1, 0), ln_g[1, 0], ln_b[1, 0], alpha, tf_ctx)

    d_rnn, lru_block_w = lru_w_in.shape[2] // 2, lru_w_a.shape[-1]
    lru_cols = _pick(d_rnn // lru_block_w, (LRU_BLOCKS_PER_STEP, 2, 1)) * lru_block_w
    w_in = _column_chunks(lru_w_in[0], lru_cols)
    y_l = _lru_in(xl, zeros_lat_f, mod_p[:, 1], w_in, tf_lat)
    y_c = _lru_in(xc, zeros_ctx_f, mod_c[:, 1], w_in, tf_ctx)
    w_a, w_i = bf(lru_w_a[0]), bf(lru_w_i[0])
    scan = functools.partial(_lru_scan, conv_w=lru_conv_w[0], conv_b=lru_conv_b[0], w_a=w_a,
                             b_a=lru_b_a[0], w_i=w_i, b_i=lru_b_i[0], a_param=lru_a_param[0])
    h_zero = jnp.zeros((SUBLANES, d_rnn), F32)
    _, h0_f = scan(y_c, h0=h_zero, direction=0, reverse=False)
    _, h0_b = scan(y_c, h0=h_zero, direction=1, reverse=True)
    hf, _ = scan(y_l, h0=h0_f, direction=0, reverse=False)
    hg, _ = scan(y_l, h0=h0_b, direction=1, reverse=True, hf=hf)
    xl = _proj_ln(hg, xl, zeros_lat, mod_p[:, 1], bf(lru_w_out[0]), ln_g[1, 1], ln_b[1, 1], alpha,
                  tm_lat)

    xl = _ffn(xl, zeros_lat_f, mod_p[:, 2], ffn_w, (1, 1), ln_g[1, 2], ln_b[1, 2], alpha, tf_lat)
    return xl.reshape(seq, batch, d).transpose(1, 0, 2)
```
